```python
import math
import jax, jax.numpy as jnp
from jax import lax
import numpy as np

D_MODEL = 1024
BATCH = 16
SEQ = 2048
DEPTH = 4

GRID_W = 64
CTX_LEN = 256
F32 = jnp.float32

N_MOD = 6
DN_ALPHA = (2 * DEPTH) ** 0.25
DN_BETA = (8 * DEPTH) ** -0.25
NORM_EPS = 1e-6
ROPE_THETA = 10000.0
CHUNK = 64
VEC_CHUNK = 16
LB_FLOOR = 1e-30
MIX_W = D_MODEL // 2
N_EVEN = (DEPTH + 1) // 2
N_ODD = DEPTH // 2

HG_DK = 128
HG_DV = 128
HG_HEADS = MIX_W // HG_DV
HG_KW = HG_HEADS * HG_DK
HG_VW = HG_HEADS * HG_DV
AT_DH = 128
AT_HEADS = MIX_W // AT_DH
AT_KV_HEADS = AT_HEADS // 2
AT_QW = AT_HEADS * AT_DH
AT_KVW = AT_KV_HEADS * AT_DH
ROPE_PAIRS = AT_DH // 4
Q_BLOCK = 128
GLA_DV = 128
GLA_HEADS = MIX_W // GLA_DV
GLA_DK = GLA_DV // 2
GLA_KW = GLA_HEADS * GLA_DK
GLA_VW = GLA_HEADS * GLA_DV
GLA_GATE_RANK = 16
GLA_GATE_NORM = 16.0
SSD_DH = 64
SSD_HEADS = MIX_W // SSD_DH
SSD_INNER = SSD_HEADS * SSD_DH
SSD_GROUPS = 2
SSD_STATE = 128
SSD_BCW = SSD_GROUPS * SSD_STATE
SSD_CONV = 5
SSD_CONV_DIM = SSD_INNER + 2 * SSD_BCW
PEER_HEADS = 8
PEER_NKEYS = 128
PEER_EXPERTS = PEER_NKEYS * PEER_NKEYS
PEER_DQ = 256
PEER_TOPK = 16
PEER_TOKEN_BLOCK = 128

EVEN_SIZES = (HG_KW, HG_VW, HG_KW, HG_KW, HG_VW, AT_QW, AT_KVW, AT_KVW)
EVEN_IN = sum(EVEN_SIZES)
EVEN_OUT = HG_VW + AT_QW
ODD_SIZES = (GLA_KW, GLA_KW, GLA_VW, GLA_VW, GLA_GATE_RANK, GLA_GATE_RANK,
             SSD_INNER, SSD_INNER, SSD_BCW, SSD_BCW, SSD_HEADS, SSD_HEADS)
ODD_IN = sum(ODD_SIZES)
ODD_OUT = GLA_VW + SSD_INNER

kernel_name = "hybrid_hgrn2_gqa_gla_ssd_peer_diffusion"


def split_cols(y, sizes):
    return jnp.split(y, np.cumsum(sizes)[:-1].tolist(), axis=-1)


def to_heads(y, n_heads):
    b, n, w = y.shape
    return y.reshape(b, n, n_heads, w // n_heads).transpose(0, 2, 1, 3)


def from_heads(o):
    b, h, n, d = o.shape
    return o.transpose(0, 2, 1, 3).reshape(b, n, h * d)


def layer_norm(x, g, b):
    xf = x.astype(F32)
    mu = jnp.mean(xf, -1, keepdims=True)
    xc = xf - mu
    var = jnp.mean(xc * xc, -1, keepdims=True)
    return (xc * lax.rsqrt(var + NORM_EPS) * g + b).astype(x.dtype)


def rms_norm(x, g):
    xf = x.astype(F32)
    return (xf * lax.rsqrt(jnp.mean(xf * xf, -1, keepdims=True) + NORM_EPS) * g).astype(x.dtype)


def modulation(cvec, w, b):
    m = jax.nn.silu(cvec) @ w + b
    return tuple(t[:, None, :] for t in jnp.split(m, N_MOD, axis=-1))


def axial_rope_tables(rows):
    row = jnp.repeat(jnp.arange(rows), GRID_W).astype(F32)
    col = jnp.tile(jnp.arange(GRID_W), rows).astype(F32)
    inv = ROPE_THETA ** (-jnp.arange(ROPE_PAIRS, dtype=F32) / ROPE_PAIRS)
    ang = jnp.concatenate([row[:, None] * inv, col[:, None] * inv], axis=-1)
    return jnp.cos(ang), jnp.sin(ang)


def apply_rope(x, cos, sin):
    xf = x.astype(F32).reshape(x.shape[:-1] + (x.shape[-1] // 2, 2))
    x0, x1 = xf[..., 0], xf[..., 1]
    c, s = cos[:, None, :], sin[:, None, :]
    return jnp.stack([x0 * c - x1 * s, x0 * s + x1 * c], -1).reshape(x.shape).astype(x.dtype)


def blocked_gqa(q, k, v):
    b, n, h, dh = q.shape
    g = k.shape[2]
    qb = q.reshape(b, n // Q_BLOCK, Q_BLOCK, g, h // g, dh).transpose(1, 0, 2, 3, 4, 5)
    scale = dh ** -0.5

    def one_block(qblk):
        s = jnp.einsum('bqgrd,bkgd->bgrqk', qblk, k, preferred_element_type=F32) * scale
        p = jax.nn.softmax(s, axis=-1).astype(v.dtype)
        return jnp.einsum('bgrqk,bkgd->bqgrd', p, v)

    o = lax.map(one_block, qb)
    return o.transpose(1, 0, 2, 3, 4, 5).reshape(b, n, h, dh)


def chunked_vector_scan(q, k, v, log_f, s0):
    b, h, t, dk = q.shape
    dv = v.shape[-1]
    nc = t // VEC_CHUNK
    qc, kc, fc = (a.astype(F32).reshape(b, h, nc, VEC_CHUNK, dk) for a in (q, k, log_f))
    vc = v.astype(F32).reshape(b, h, nc, VEC_CHUNK, dv)
    if s0 is None:
        s0 = jnp.zeros((b, h, dk, dv), F32)
    cum = jnp.cumsum(fc, axis=3)
    ref = cum[:, :, :, :1]
    causal = jnp.tril(jnp.ones((VEC_CHUNK, VEC_CHUNK), dtype=bool))
    att = jnp.einsum('bhcid,bhcjd->bhcij', qc * jnp.exp(cum - ref), kc * jnp.exp(ref - cum))
    att = jnp.where(causal, att, 0.0)
    o = jnp.einsum('bhcij,bhcjv->bhciv', att, vc)
    last = cum[:, :, :, -1:]
    kv = jnp.einsum('bhcjd,bhcjv->bhcdv', kc * jnp.exp(last - cum), vc)
    decay = jnp.exp(last[:, :, :, 0])

    def step(s, inp):
        kv_c, d_c = inp
        return s * d_c[..., None] + kv_c, s

    s_t, s_in = lax.scan(step, s0, (jnp.moveaxis(kv, 2, 0), jnp.moveaxis(decay, 2, 0)))
    s_in = jnp.moveaxis(s_in, 0, 2)
    o = o + jnp.einsum('bhcid,bhcdv->bhciv', qc * jnp.exp(cum), s_in)
    return o.reshape(b, h, t, dv), s_t


def vector_final_state(_q, k, v, log_f):
    cum = jnp.cumsum(log_f.astype(F32), axis=2)
    w = jnp.exp(cum[:, :, -1:] - cum)
    return jnp.einsum('bhtd,bhtv->bhdv', k.astype(F32) * w, v.astype(F32))


def chunked_ssd_scan(xdt, bm, cm, log_a, s0):
    b, t, h, p = xdt.shape
    g, n = bm.shape[2], bm.shape[3]
    r = h // g
    nc = t // CHUNK
    xc = xdt.astype(F32).reshape(b, nc, CHUNK, g, r, p)
    bc = bm.astype(F32).reshape(b, nc, CHUNK, g, n)
    cc = cm.astype(F32).reshape(b, nc, CHUNK, g, n)
    if s0 is None:
        s0 = jnp.zeros((b, g, r, p, n), F32)
    cum = jnp.cumsum(log_a.astype(F32).reshape(b, nc, CHUNK, g, r), axis=2)
    causal = jnp.tril(jnp.ones((CHUNK, CHUNK), dtype=bool))[None, None, :, :, None, None]
    seg = cum[:, :, :, None] - cum[:, :, None]
    decay = jnp.where(causal, jnp.exp(jnp.where(causal, seg, 0.0)), 0.0)
    scores = jnp.einsum('bcign,bcjgn->bcijg', cc, bc)
    y = jnp.einsum('bcijgr,bcjgrp->bcigrp', scores[..., None] * decay, xc)
    last = cum[:, :, -1:]
    st = jnp.einsum('bcjgn,bcjgrp->bcgrpn', bc, xc * jnp.exp(last - cum)[..., None])
    chunk_decay = jnp.exp(last[:, :, 0])

    def step(s, inp):
        st_c, d_c = inp
        return s * d_c[..., None, None] + st_c, s

    s_t, s_in = lax.scan(step, s0, (jnp.moveaxis(st, 1, 0), jnp.moveaxis(chunk_decay, 1, 0)))
    s_in = jnp.moveaxis(s_in, 0, 1)
    y = y + jnp.einsum('bcign,bcgrpn->bcigrp', cc, s_in) * jnp.exp(cum)[..., None]
    return y.reshape(b, t, h, p), s_t


def ssd_final_state(xdt, bm, _cm, log_a):
    b, t, h, p = xdt.shape
    g = bm.shape[2]
    cum = jnp.cumsum(log_a.astype(F32), axis=1)
    w = jnp.exp(cum[:, -1:] - cum)
    xw = (xdt.astype(F32) * w[..., None]).reshape(b, t, g, h // g, p)
    return jnp.einsum('btgn,btgrp->bgrpn', bm.astype(F32), xw)


def bidir_prefix_scan(scan_fn, final_fn, axis, ctx_dirs, lat_dirs, need_ctx_out):
    o_ctx = None
    o_lat = None
    for d in range(2):
        flip = (lambda a: jnp.flip(a, axis)) if d == 1 else (lambda a: a)
        c_args = [flip(a) for a in ctx_dirs[d]]
        l_args = [flip(a) for a in lat_dirs[d]]
        if need_ctx_out:
            oc, s_ctx = scan_fn(*c_args, None)
            oc = flip(oc)
            o_ctx = oc if o_ctx is None else o_ctx + oc
        else:
            s_ctx = final_fn(*c_args)
        ol = flip(scan_fn(*l_args, s_ctx)[0])
        o_lat = ol if o_lat is None else o_lat + ol
    return o_ctx, o_lat


def depthwise_conv_centred(x, w, bias):
    y = lax.conv_general_dilated(x, w[:, None, :].astype(x.dtype), window_strides=(1,),
                                 padding=((SSD_CONV // 2, SSD_CONV // 2),),
                                 dimension_numbers=('NWC', 'WIO', 'NWC'),
                                 feature_group_count=x.shape[-1])
    return y + bias


def even_mixer(u_ctx, u_lat, cos, sin, w_in, w_out, lb, hg_norm_g, q_norm_g, k_norm_g, need_ctx_out):
    def prep(u):
        b, n, _ = u.shape
        q, i, zf, zb, g, aq, ak, av = split_cols(u @ w_in, EVEN_SIZES)
        qh = to_heads(jax.nn.silu(q.astype(F32)), HG_HEADS) * HG_DK ** -0.5
        vh = to_heads(i.astype(F32), HG_HEADS)
        dirs = []
        for d, zraw in enumerate((zf, zb)):
            z = zraw.astype(F32)
            log_f = jnp.logaddexp(jnp.log(jnp.maximum(lb[d], LB_FLOOR)),
                                  jnp.log1p(-lb[d]) + jax.nn.log_sigmoid(z))
            k = (1.0 - lb[d]) * jax.nn.sigmoid(-z)
            dirs.append((qh, to_heads(k, HG_HEADS), vh, to_heads(log_f, HG_HEADS)))
        aq = rms_norm(aq.reshape(b, n, AT_HEADS, AT_DH), q_norm_g)
        ak = rms_norm(ak.reshape(b, n, AT_KV_HEADS, AT_DH), k_norm_g)
        av = av.reshape(b, n, AT_KV_HEADS, AT_DH)
        return dirs, g, aq, ak, av

    c_dirs, c_g, c_q, c_k, c_v = prep(u_ctx)
    l_dirs, l_g, l_q, l_k, l_v = prep(u_lat)
    hg_c, hg_l = bidir_prefix_scan(chunked_vector_scan, vector_final_state, 2, c_dirs, l_dirs, need_ctx_out)
    at_l = blocked_gqa(apply_rope(l_q, cos, sin),
                       jnp.concatenate([apply_rope(l_k, cos, sin), c_k], axis=1),
                       jnp.concatenate([l_v, c_v], axis=1))

    def finish(hg_o, g, at_o, dtype):
        b, n = g.shape[:2]
        hg = from_heads(hg_o) * jax.nn.sigmoid(g.astype(F32))
        hg = rms_norm(hg.reshape(b, n, HG_HEADS, HG_DV), hg_norm_g).reshape(b, n, HG_VW)
        mixed = jnp.concatenate([hg, at_o.reshape(b, n, AT_QW).astype(F32)], axis=-1)
        return mixed.astype(dtype) @ w_out

    o_lat = finish(hg_l, l_g, at_l, u_lat.dtype)
    o_ctx = finish(hg_c, c_g, blocked_gqa(c_q, c_k, c_v), u_ctx.dtype) if need_ctx_out else None
    return o_ctx, o_lat


def odd_mixer(u_ctx, u_lat, w_in, w_out, gate_w, gate_b, gla_norm_g, conv_w, conv_b,
              dt_bias, a_log, d_skip, ssd_norm_g, need_ctx_out):
    def prep(u):
        b, n, _ = u.shape
        q, k, v, g, lr_f, lr_b, z, xs, bm, cm, dt_f, dt_b = split_cols(u @ w_in, ODD_SIZES)
        qh = to_heads(q.astype(F32), GLA_HEADS) * GLA_DK ** -0.5
        kh = to_heads(k.astype(F32), GLA_HEADS)
        vh = to_heads(v.astype(F32), GLA_HEADS)
        gla_dirs = []
        for d, lr in enumerate((lr_f, lr_b)):
            log_f = jax.nn.log_sigmoid((lr @ gate_w[d]).astype(F32) + gate_b[d]) / GLA_GATE_NORM
            gla_dirs.append((qh, kh, vh, to_heads(log_f, GLA_HEADS)))
        xbc = jax.nn.silu(depthwise_conv_centred(jnp.concatenate([xs, bm, cm], axis=-1), conv_w, conv_b))
        xs, bm, cm = split_cols(xbc, (SSD_INNER, SSD_BCW, SSD_BCW))
        xh = xs.reshape(b, n, SSD_HEADS, SSD_DH).astype(F32)
        bm = bm.reshape(b, n, SSD_GROUPS, SSD_STATE)
        cm = cm.reshape(b, n, SSD_GROUPS, SSD_STATE)
        ssd_dirs = []
        for d, dt_raw in enumerate((dt_f, dt_b)):
            dt = jax.nn.softplus(dt_raw.astype(F32) + dt_bias[d])
            ssd_dirs.append((xh * dt[..., None], bm, cm, -jnp.exp(a_log[d].astype(F32)) * dt))
        return gla_dirs, g, ssd_dirs, xh, z

    c_gla, c_g, c_ssd, c_xh, c_z = prep(u_ctx)
    l_gla, l_g, l_ssd, l_xh, l_z = prep(u_lat)
    gla_c, gla_l = bidir_prefix_scan(chunked_vector_scan, vector_final_state, 2, c_gla, l_gla, need_ctx_out)
    ssd_c, ssd_l = bidir_prefix_scan(chunked_ssd_scan, ssd_final_state, 1, c_ssd, l_ssd, need_ctx_out)

    def finish(gla_o, g, ssd_y, xh, z, dtype):
        b, n = g.shape[:2]
        gla = from_heads(rms_norm(gla_o, gla_norm_g)) * jax.nn.silu(g.astype(F32))
        y = (ssd_y + d_skip[:, None] * xh).reshape(b, n, SSD_INNER) * jax.nn.silu(z.astype(F32))
        y = rms_norm(y.reshape(b, n, SSD_GROUPS, SSD_INNER // SSD_GROUPS),
                     ssd_norm_g.reshape(SSD_GROUPS, -1)).reshape(b, n, SSD_INNER)
        return jnp.concatenate([gla, y], axis=-1).astype(dtype) @ w_out

    o_lat = finish(gla_l, l_g, ssd_l, l_xh, l_z, u_lat.dtype)
    o_ctx = finish(gla_c, c_g, ssd_c, c_xh, c_z, u_ctx.dtype) if need_ctx_out else None
    return o_ctx, o_lat


def peer_ffn(h, wq, subkeys, u_tab, v_tab):
    b, n, d = h.shape
    q = (h @ wq).reshape(b, n, PEER_HEADS, 2, PEER_DQ // 2)
    s = jnp.einsum('bnhsk,hsek->bnhse', q, subkeys, preferred_element_type=F32)
    ts, ti = lax.top_k(s, PEER_TOPK)
    cand_s = (ts[..., 0, :, None] + ts[..., 1, None, :]).reshape(b, n, PEER_HEADS, PEER_TOPK * PEER_TOPK)
    cand_i = (ti[..., 0, :, None] * PEER_NKEYS + ti[..., 1, None, :]).reshape(b, n, PEER_HEADS, PEER_TOPK * PEER_TOPK)
    best_s, pos = lax.top_k(cand_s, PEER_TOPK)
    idx = jnp.take_along_axis(cand_i, pos, axis=-1)
    gate = jax.nn.softmax(best_s, axis=-1)
    nb = (b * n) // PEER_TOKEN_BLOCK
    hk = PEER_HEADS * PEER_TOPK
    xb = h.reshape(nb, PEER_TOKEN_BLOCK, d)
    ib = idx.reshape(nb, PEER_TOKEN_BLOCK, hk)
    gb = gate.reshape(nb, PEER_TOKEN_BLOCK, hk)

    def expert_block(args):
        xt, it, gt = args
        act = jax.nn.gelu(jnp.einsum('td,tkd->tk', xt, jnp.take(u_tab, it, axis=0), preferred_element_type=F32))
        return jnp.einsum('tk,tkd->td', (act * gt).astype(xt.dtype), jnp.take(v_tab, it, axis=0))

    return lax.map(expert_block, (xb, ib, gb)).reshape(b, n, d)


def setup_inputs(seed: int = 0) -> dict:
    key = jax.random.key(seed)
    ks = iter(jax.random.split(key, 40))
    D = D_MODEL

    def nrm(shape, std):
        return std * jax.random.normal(next(ks), shape, F32)

    dt0 = jnp.exp(jax.random.uniform(next(ks), (N_ODD, 2, SSD_HEADS), F32, math.log(1e-3), math.log(1e-1)))
    return {
        "x": nrm((BATCH, SEQ, D), 1.0),
        "c": nrm((BATCH, D), 1.0),
        "ctx": nrm((BATCH, CTX_LEN, D), 1.0),
        "c_ctx": nrm((D,), 1.0),
        "mod_w": nrm((DEPTH, D, N_MOD * D), 0.5 * D ** -0.5),
        "mod_b": nrm((DEPTH, N_MOD * D), 0.02),
        "ln_g": 1.0 + nrm((DEPTH, 2, D), 0.02),
        "ln_b": nrm((DEPTH, 2, D), 0.02),
        "ev_w_in": nrm((N_EVEN, D, EVEN_IN), D ** -0.5),
        "ev_w_out": nrm((N_EVEN, EVEN_OUT, D), DN_BETA * EVEN_OUT ** -0.5),
        "hg_lb_logits": nrm((2, N_EVEN, HG_KW), 1.0),
        "hg_norm_g": 1.0 + nrm((N_EVEN, HG_DV), 0.02),
        "at_q_norm_g": 1.0 + nrm((N_EVEN, AT_DH), 0.02),
        "at_k_norm_g": 1.0 + nrm((N_EVEN, AT_DH), 0.02),
        "od_w_in": nrm((N_ODD, D, ODD_IN), D ** -0.5),
        "od_w_out": nrm((N_ODD, ODD_OUT, D), DN_BETA * ODD_OUT ** -0.5),
        "gla_gate_w": nrm((N_ODD, 2, GLA_GATE_RANK, GLA_KW), GLA_GATE_RANK ** -0.5),
        "gla_gate_b": nrm((N_ODD, 2, GLA_KW), 0.1),
        "gla_norm_g": 1.0 + nrm((N_ODD, GLA_DV), 0.02),
        "ssd_conv_w": nrm((N_ODD, SSD_CONV, SSD_CONV_DIM), SSD_CONV ** -0.5),
        "ssd_conv_b": nrm((N_ODD, SSD_CONV_DIM), 0.02),
        "ssd_dt_bias": dt0 + jnp.log(-jnp.expm1(-dt0)),
        "ssd_a_log": jnp.log(jax.random.uniform(next(ks), (N_ODD, 2, SSD_HEADS), F32, 1.0, 16.0)),
        "ssd_d": 1.0 + nrm((N_ODD, SSD_HEADS), 0.02),
        "ssd_norm_g": 1.0 + nrm((N_ODD, SSD_INNER), 0.02),
        "peer_wq": nrm((DEPTH, D, PEER_HEADS * PEER_DQ), D ** -0.5),
        "peer_subkeys": nrm((DEPTH, PEER_HEADS, 2, PEER_NKEYS, PEER_DQ // 2), (PEER_DQ // 2) ** -0.5),
        "peer_u": nrm((DEPTH, PEER_EXPERTS, D), D ** -0.5),
        "peer_v": nrm((DEPTH, PEER_EXPERTS, D), DN_BETA * (PEER_TOPK / PEER_HEADS) ** 0.5),
    }


def reference(x, c, ctx, c_ctx, mod_w, mod_b, ln_g, ln_b, ev_w_in, ev_w_out, hg_lb_logits,
              hg_norm_g, at_q_norm_g, at_k_norm_g, od_w_in, od_w_out, gla_gate_w, gla_gate_b,
              gla_norm_g, ssd_conv_w, ssd_conv_b, ssd_dt_bias, ssd_a_log, ssd_d, ssd_norm_g,
              peer_wq, peer_subkeys, peer_u, peer_v):
    rows = x.shape[1] // GRID_W
    cos, sin = axial_rope_tables(rows)
    sm = jax.nn.softmax(hg_lb_logits.astype(F32), axis=1)
    hg_lb = jnp.cumsum(sm, axis=1) - sm[:, :1]
    h, hc = x, ctx
    for l in range(DEPTH):
        last = l == DEPTH - 1
        sh1, sc1, g1, sh2, sc2, g2 = modulation(c, mod_w[l], mod_b[l])
        csh1, csc1, cg1, csh2, csc2, cg2 = modulation(c_ctx[None, :], mod_w[l], mod_b[l])
        u_lat = h * (1.0 + sc1) + sh1
        u_ctx = hc * (1.0 + csc1) + csh1
        j = l // 2
        if l % 2 == 0:
            o_ctx, o_lat = even_mixer(u_ctx, u_lat, cos, sin, ev_w_in[j], ev_w_out[j], hg_lb[:, j],
                                      hg_norm_g[j], at_q_norm_g[j], at_k_norm_g[j], not last)
        else:
            o_ctx, o_lat = odd_mixer(u_ctx, u_lat, od_w_in[j], od_w_out[j], gla_gate_w[j], gla_gate_b[j],
                                     gla_norm_g[j], ssd_conv_w[j], ssd_conv_b[j], ssd_dt_bias[j],
                                     ssd_a_log[j], ssd_d[j], ssd_norm_g[j], not last)
        h = layer_norm(DN_ALPHA * h + g1 * o_lat, ln_g[l, 0], ln_b[l, 0])
        h = layer_norm(DN_ALPHA * h + g2 * peer_ffn(h * (1.0 + sc2) + sh2, peer_wq[l], peer_subkeys[l],
                                                    peer_u[l], peer_v[l]), ln_g[l, 1], ln_b[l, 1])
        if not last:
            hc = layer_norm(DN_ALPHA * hc + cg1 * o_ctx, ln_g[l, 0], ln_b[l, 0])
            hc = layer_norm(DN_ALPHA * hc + cg2 * peer_ffn(hc * (1.0 + csc2) + csh2, peer_wq[l], peer_subkeys[l],
                                                           peer_u[l], peer_v[l]), ln_g[l, 1], ln_b[l, 1])
    return h
```

```python
import functools
import math

import numpy as np
import jax
import jax.numpy as jnp
from jax import lax
from jax.experimental import pallas as pl
from jax.experimental.pallas import tpu as pltpu

F32 = jnp.float32
BF16 = jnp.bfloat16
HIGHEST = lax.Precision.HIGHEST

D = 1024
DEPTH = 4
N_MOD = 6
CTX = 256
GRID_W = 64
DN_ALPHA = (2 * DEPTH) ** 0.25
NORM_EPS = 1e-6
ROPE_THETA = 10000.0
LB_FLOOR = 1e-30
TM = 256
MOD_ROWS = 24
VMEM_LIMIT = 56 * 1024 * 1024

HEAD_W = 128
VEC_CHUNK = 16
SSD_CHUNK = 128
SSD_DH = 64
GLA_DK = 64
GLA_GATE_NORM = 16.0
PEER_HEADS = 8
PEER_NKEYS = 128
PEER_TOPK = 16
PEER_T = 512
PEER_EB = 1024

EV_Q, EV_I, EV_ZF, EV_ZB, EV_G, EV_AQ, EV_AK, EV_AV = 0, 512, 1024, 1536, 2048, 2560, 3072, 3328
EV_W = 3584
OD_Q, OD_K, OD_V, OD_G, OD_Z, OD_XS, OD_BM, OD_CM, OD_MISC = 0, 512, 1024, 1536, 2048, 2560, 3072, 3328, 3584
OD_W = 3712
MISC_LRF, MISC_LRB, MISC_DTF, MISC_DTB = 0, 16, 32, 40

NT_DIMS = (((1,), (1,)), ((), ()))
TN_DIMS = (((0,), (0,)), ((), ()))


def _cparams(sem):
    return pltpu.CompilerParams(dimension_semantics=sem, vmem_limit_bytes=VMEM_LIMIT)


def _layer_norm(y, g, b):
    mu = jnp.mean(y, axis=-1, keepdims=True)
    yc = y - mu
    var = jnp.mean(yc * yc, axis=-1, keepdims=True)
    return yc * lax.rsqrt(var + NORM_EPS) * g + b


def _rms(y, g):
    return y * lax.rsqrt(jnp.mean(y * y, axis=-1, keepdims=True) + NORM_EPS) * g


def _log_sigmoid(z):
    return jnp.minimum(z, 0.0) - jnp.log(1.0 + jnp.exp(-jnp.abs(z)))


def _silu(z):
    return z * jax.nn.sigmoid(z)


def _split_hi_lo(a):
    hi = a.astype(BF16)
    lo = (a - hi.astype(F32)).astype(BF16)
    return hi, lo


def _mod_body(c_ref, w_ref, b_ref, o_ref):
    a = _silu(c_ref[...])
    a_hi, a_lo = _split_hi_lo(a)
    w_hi, w_lo = _split_hi_lo(w_ref[0])
    acc = jnp.dot(a_hi, w_hi, preferred_element_type=F32)
    acc += jnp.dot(a_lo, w_hi, preferred_element_type=F32)
    acc += jnp.dot(a_hi, w_lo, preferred_element_type=F32)
    o_ref[0] = acc + b_ref[0]


def _modulation(cc, mod_w, mod_b):
    tn = 1536
    nw = mod_w.shape[-1]
    return pl.pallas_call(
        _mod_body,
        grid=(DEPTH, nw // tn),
        in_specs=[
            pl.BlockSpec((MOD_ROWS, D), lambda l, j: (0, 0)),
            pl.BlockSpec((1, D, tn), lambda l, j: (l, 0, j)),
            pl.BlockSpec((1, 1, tn), lambda l, j: (l, 0, j)),
        ],
        out_specs=pl.BlockSpec((1, MOD_ROWS, tn), lambda l, j: (l, 0, j)),
        out_shape=jax.ShapeDtypeStruct((DEPTH, MOD_ROWS, nw), F32),
        compiler_params=_cparams(("parallel", "parallel")),
        name="modulation",
    )(cc, mod_w, mod_b.reshape(DEPTH, 1, nw))


def _mod_spec(k, n_batch, tiles_per_batch):
    def index(i):
        row = jnp.where(i % tiles_per_batch == 0, n_batch, i // tiles_per_batch)
        return (k * MOD_ROWS + row, 0, 0)
    return pl.BlockSpec((1, 1, D), index)


def _inproj_body(has_pre, *refs):
    if has_pre:
        x_ref, y_ref, g2_ref, lng_ref, lnb_ref, sh_ref, sc_ref, w_ref, o_ref, h_ref = refs
        h = _layer_norm(DN_ALPHA * x_ref[...] + g2_ref[0] * y_ref[...], lng_ref[...], lnb_ref[...])
        h_ref[...] = h
    else:
        x_ref, sh_ref, sc_ref, w_ref, o_ref = refs
        h = x_ref[...]
    u = h * (1.0 + sc_ref[0]) + sh_ref[0]
    o_ref[...] = jnp.dot(u.astype(BF16), w_ref[...], preferred_element_type=F32)


def _inproj(h, w, modl, n_batch, pre=None):
    n = h.shape[0]
    nw = w.shape[1]
    tpb = n // n_batch // TM
    tok = pl.BlockSpec((TM, D), lambda i: (i, 0))
    vec = pl.BlockSpec((1, D), lambda i: (0, 0))
    in_specs = [tok]
    args = [h]
    if pre is not None:
        y, prev_modl, ln_g, ln_b = pre
        in_specs += [tok, _mod_spec(5, n_batch, tpb), vec, vec]
        args += [y, prev_modl, ln_g, ln_b]
    in_specs += [_mod_spec(0, n_batch, tpb), _mod_spec(1, n_batch, tpb),
                 pl.BlockSpec((D, nw), lambda i: (0, 0))]
    args += [modl, modl, w]
    out_specs = [pl.BlockSpec((TM, nw), lambda i: (i, 0))]
    out_shape = [jax.ShapeDtypeStruct((n, nw), F32)]
    if pre is not None:
        out_specs.append(tok)
        out_shape.append(jax.ShapeDtypeStruct((n, D), F32))
    res = pl.pallas_call(
        functools.partial(_inproj_body, pre is not None),
        grid=(n // TM,),
        in_specs=in_specs,
        out_specs=out_specs,
        out_shape=out_shape,
        compiler_params=_cparams(("parallel",)),
        name="inproj",
    )(*args)
    return (res[0], res[1]) if pre is not None else (res[0], h)


def _final_body(x_ref, y_ref, g2_ref, lng_ref, lnb_ref, o_ref):
    o_ref[...] = _layer_norm(DN_ALPHA * x_ref[...] + g2_ref[0] * y_ref[...], lng_ref[...], lnb_ref[...])


def _final(h, y, modl, ln_g, ln_b, n_batch):
    n = h.shape[0]
    tpb = n // n_batch // TM
    lat = tpb - 1
    tok = pl.BlockSpec((TM, D), lambda b, j: (b * tpb + 1 + j, 0))
    vec = pl.BlockSpec((1, D), lambda b, j: (0, 0))
    return pl.pallas_call(
        _final_body,
        grid=(n_batch, lat),
        in_specs=[tok, tok, pl.BlockSpec((1, 1, D), lambda b, j: (5 * MOD_ROWS + b, 0, 0)), vec, vec],
        out_specs=pl.BlockSpec((TM, D), lambda b, j: (b * lat + j, 0)),
        out_shape=jax.ShapeDtypeStruct((n_batch * lat * TM, D), F32),
        compiler_params=_cparams(("parallel", "parallel")),
        name="final_norm",
    )(h, y, modl, ln_g, ln_b)


def _vscan_body(mode, hpb, n_ctx_chunks, n_chunks, *refs):
    C = VEC_CHUNK
    if mode == "hgrn":
        q_ref, v_ref, zf_ref, zb_ref, g_ref, par_ref, o_ref, st_ref = refs
    else:
        q_ref, k_ref, v_ref, g_ref, misc_ref, wg_ref, par_ref, o_ref, st_ref = refs
    rows = o_ref.shape[1]

    o_ref[...] = jnp.zeros(o_ref.shape, F32)
    st_ref[...] = jnp.zeros(st_ref.shape, F32)
    ri = lax.broadcasted_iota(jnp.int32, (C, C), 0)
    ci = lax.broadcasted_iota(jnp.int32, (C, C), 1)
    masks = (ri >= ci, ri <= ci)
    tri = tuple(m.astype(F32) for m in masks)

    def chunk(d, hh, r0):
        lanes = pl.ds(hh * HEAD_W, HEAD_W)
        rs = pl.ds(r0, C)
        q = q_ref[0, rs, lanes]
        v = v_ref[0, rs, lanes]
        if mode == "hgrn":
            z = (zf_ref, zb_ref)[d][0, rs, lanes]
            la = par_ref[hh, d:d + 1, :]
            l1 = par_ref[hh, 2 + d:3 + d, :]
            oml = par_ref[hh, 4 + d:5 + d, :]
            qh = _silu(q) * (HEAD_W ** -0.5)
            bb = l1 + _log_sigmoid(z)
            log_f = jnp.maximum(la, bb) + jnp.log(1.0 + jnp.exp(-jnp.abs(la - bb)))
            k = oml * jax.nn.sigmoid(-z)
        else:
            qh = q * (GLA_DK ** -0.5)
            k = k_ref[0, rs, lanes]
            pre = jnp.dot(misc_ref[0, rs, :].astype(BF16), wg_ref[hh, d], preferred_element_type=F32)
            log_f = _log_sigmoid(pre + par_ref[hh, d:d + 1, :]) * (1.0 / GLA_GATE_NORM)
        cum = jnp.dot(tri[d], log_f, precision=HIGHEST, preferred_element_type=F32)
        first, last = (cum[0:1], cum[C - 1:C]) if d == 0 else (cum[C - 1:C], cum[0:1])
        qa = (qh * jnp.exp(cum - first)).astype(BF16)
        kb = (k * jnp.exp(first - cum)).astype(BF16)
        att = lax.dot_general(qa, kb, NT_DIMS, preferred_element_type=F32)
        att = jnp.where(masks[d], att, 0.0).astype(BF16)
        vb = v.astype(BF16)
        o = jnp.dot(att, vb, preferred_element_type=F32)
        s_t = st_ref[d * hpb + hh]
        qe = (qh * jnp.exp(cum)).astype(BF16)
        o += lax.dot_general(qe, s_t.astype(BF16), NT_DIMS, preferred_element_type=F32)
        kl = (k * jnp.exp(last - cum)).astype(BF16)
        kv_t = lax.dot_general(vb, kl, TN_DIMS, preferred_element_type=F32)
        st_ref[d * hpb + hh] = s_t * jnp.exp(last) + kv_t
        o_ref[0, rs, lanes] += o

    def step(t, carry):
        cf = t
        cb = jnp.where(t < n_ctx_chunks, n_ctx_chunks - 1 - t, n_chunks + n_ctx_chunks - 1 - t)
        for hh in range(hpb):
            chunk(0, hh, pl.multiple_of(cf * C, C))
            chunk(1, hh, pl.multiple_of(cb * C, C))
        return carry

    lax.fori_loop(0, n_chunks, step, 0)

    blk = 256

    def post(i, carry):
        rs = pl.ds(pl.multiple_of(i * blk, blk), blk)
        for hh in range(hpb):
            lanes = pl.ds(hh * HEAD_W, HEAD_W)
            o = o_ref[0, rs, lanes]
            g = g_ref[0, rs, lanes]
            if mode == "hgrn":
                res = _rms(o * jax.nn.sigmoid(g), par_ref[hh, 6:7, :])
            else:
                res = _rms(o, par_ref[hh, 2:3, :]) * _silu(g)
            o_ref[0, rs, lanes] = res
        return carry

    lax.fori_loop(0, rows // blk, post, 0)


def _vscan(mode, y3, col_offsets, params, wg=None, hpb=2):
    nb, rows, _ = y3.shape
    n_heads = 4
    width = hpb * HEAD_W
    seq = lambda off: pl.BlockSpec((1, rows, width), lambda b, p: (b, 0, off // width + p))
    in_specs = [seq(off) for off in col_offsets]
    args = [y3] * len(col_offsets)
    if mode == "gla":
        in_specs.append(pl.BlockSpec((1, rows, HEAD_W), lambda b, p: (b, 0, OD_MISC // HEAD_W)))
        in_specs.append(pl.BlockSpec((hpb, 2, HEAD_W, HEAD_W), lambda b, p: (p, 0, 0, 0)))
        args += [y3, wg]
    in_specs.append(pl.BlockSpec((hpb, 8, HEAD_W), lambda b, p: (p, 0, 0)))
    args.append(params)
    return pl.pallas_call(
        functools.partial(_vscan_body, mode, hpb, CTX // VEC_CHUNK, rows // VEC_CHUNK),
        grid=(nb, n_heads // hpb),
        in_specs=in_specs,
        out_specs=pl.BlockSpec((1, rows, width), lambda b, p: (b, 0, p)),
        out_shape=jax.ShapeDtypeStruct((nb, rows, n_heads * HEAD_W), F32),
        scratch_shapes=[pltpu.VMEM((2 * hpb, HEAD_W, HEAD_W), F32)],
        compiler_params=_cparams(("parallel", "parallel")),
        name="vscan_" + mode,
    )(*args)


def _rope(x, c, sa, sb):
    return x * c + pltpu.roll(x, HEAD_W - 1, 1) * sa + pltpu.roll(x, 1, 1) * sb


def _attn_body(q_ref, k_ref, v_ref, ck_ref, sak_ref, sbk_ref, cq_ref, saq_ref, sbq_ref,
               qg_ref, kg_ref, o_ref, kp_ref, vp_ref):
    qi = pl.program_id(2)
    rows = k_ref.shape[1]
    blk = 256

    @pl.when(qi == 0)
    def _():
        def prep(i, carry):
            rs = pl.ds(pl.multiple_of(i * blk, blk), blk)
            kn = _rms(k_ref[0, rs, :], kg_ref[...])
            kp_ref[rs, :] = _rope(kn, ck_ref[rs, :], sak_ref[rs, :], sbk_ref[rs, :]).astype(BF16)
            vp_ref[rs, :] = v_ref[0, rs, :].astype(BF16)
            return carry
        lax.fori_loop(0, rows // blk, prep, 0)

    def attend(nk):
        outs = []
        for r in range(2):
            qn = _rms(q_ref[0, :, r * HEAD_W:(r + 1) * HEAD_W], qg_ref[...])
            qr = (_rope(qn, cq_ref[...], saq_ref[...], sbq_ref[...]) * (HEAD_W ** -0.5)).astype(BF16)
            s = lax.dot_general(qr, kp_ref[0:nk, :], NT_DIMS, preferred_element_type=F32)
            m = jnp.max(s, axis=-1, keepdims=True)
            p = jnp.exp(s - m)
            l = jnp.sum(p, axis=-1, keepdims=True)
            o = jnp.dot(p.astype(BF16), vp_ref[0:nk, :], preferred_element_type=F32)
            outs.append(o / l)
        o_ref[0] = jnp.concatenate(outs, axis=-1)

    @pl.when(qi == 0)
    def _():
        attend(CTX)

    @pl.when(qi > 0)
    def _():
        attend(rows)


def _attention(y3, tables, q_gain, k_gain):
    nb, rows, _ = y3.shape
    c, sa, sb = tables
    full = pl.BlockSpec((rows, HEAD_W), lambda b, g, i: (0, 0))
    tile = pl.BlockSpec((TM, HEAD_W), lambda b, g, i: (i, 0))
    vec = pl.BlockSpec((1, HEAD_W), lambda b, g, i: (0, 0))
    return pl.pallas_call(
        _attn_body,
        grid=(nb, 2, rows // TM),
        in_specs=[
            pl.BlockSpec((1, TM, 2 * HEAD_W), lambda b, g, i: (b, i, EV_AQ // (2 * HEAD_W) + g)),
            pl.BlockSpec((1, rows, HEAD_W), lambda b, g, i: (b, 0, EV_AK // HEAD_W + g)),
            pl.BlockSpec((1, rows, HEAD_W), lambda b, g, i: (b, 0, EV_AV // HEAD_W + g)),
            full, full, full, tile, tile, tile, vec, vec,
        ],
        out_specs=pl.BlockSpec((1, TM, 2 * HEAD_W), lambda b, g, i: (b, i, g)),
        out_shape=jax.ShapeDtypeStruct((nb, rows, 4 * HEAD_W), F32),
        scratch_shapes=[pltpu.VMEM((rows, HEAD_W), BF16), pltpu.VMEM((rows, HEAD_W), BF16)],
        compiler_params=_cparams(("parallel", "parallel", "arbitrary")),
        name="attention",
    )(y3, y3, y3, c, sa, sb, c, sa, sb, q_gain, k_gain)


def _conv_body(x_ref, w_ref, b_ref, o_ref):
    rows = x_ref.shape[1]
    x = x_ref[0]
    row = lax.broadcasted_iota(jnp.int32, x.shape, 0)
    lo = jnp.where(row < CTX, 0, CTX)
    hi = jnp.where(row < CTX, CTX, rows)
    taps = w_ref.shape[0]
    acc = jnp.zeros(x.shape, F32) + b_ref[...]
    for j in range(taps):
        off = j - taps // 2
        xs = x if off == 0 else pltpu.roll(x, (-off) % rows, 0)
        src = row + off
        xs = jnp.where(src >= lo, jnp.where(src < hi, xs, 0.0), 0.0)
        acc += xs * w_ref[j:j + 1, :]
    o_ref[0] = _silu(acc)


def _ssd_conv(y3, conv_w, conv_b):
    nb, rows, _ = y3.shape
    nc = conv_w.shape[1]
    return pl.pallas_call(
        _conv_body,
        grid=(nb, nc // HEAD_W),
        in_specs=[
            pl.BlockSpec((1, rows, HEAD_W), lambda b, j: (b, 0, OD_XS // HEAD_W + j)),
            pl.BlockSpec((conv_w.shape[0], HEAD_W), lambda b, j: (0, j)),
            pl.BlockSpec((1, HEAD_W), lambda b, j: (0, j)),
        ],
        out_specs=pl.BlockSpec((1, rows, HEAD_W), lambda b, j: (b, 0, j)),
        out_shape=jax.ShapeDtypeStruct((nb, rows, nc), F32),
        compiler_params=_cparams(("parallel", "parallel")),
        name="ssd_conv",
    )(y3, conv_w, conv_b.reshape(1, nc))


def _ssd_body(n_ctx_chunks, n_chunks, x_ref, bm_ref, cm_ref, z_ref, misc_ref, sel_ref, par_ref,
              par2_ref, o_ref, st_ref):
    C = SSD_CHUNK
    P = SSD_DH
    rows = o_ref.shape[1]
    o_ref[...] = jnp.zeros(o_ref.shape, F32)
    st_ref[...] = jnp.zeros(st_ref.shape, F32)
    ri = lax.broadcasted_iota(jnp.int32, (C, C), 0)
    ci = lax.broadcasted_iota(jnp.int32, (C, C), 1)
    masks = (ri >= ci, ri <= ci)
    low = masks[0].astype(F32)
    upp = masks[1].astype(F32)

    def chunk(d, r0):
        rs = pl.ds(r0, C)
        x = x_ref[0, rs, :]
        bmat = bm_ref[0, rs, :]
        cb16 = cm_ref[0, rs, :].astype(BF16)
        b16 = bmat.astype(BF16)
        bt16 = bmat.T.astype(BF16)
        scores = lax.dot_general(cb16, b16, NT_DIMS, preferred_element_type=F32)
        dt = jnp.dot(misc_ref[0, rs, :], sel_ref[0, d], precision=HIGHEST, preferred_element_type=F32)
        dt = dt + par_ref[0, d:d + 1, :]
        dt = jnp.maximum(dt, 0.0) + jnp.log(1.0 + jnp.exp(-jnp.abs(dt)))
        la = dt * par_ref[0, 2 + d:3 + d, :]
        ys = []
        for hh in range(4):
            la_h = la[:, hh * HEAD_W:(hh + 1) * HEAD_W]
            dt_h = dt[:, hh * HEAD_W:hh * HEAD_W + P]
            incl = low if d == 0 else upp
            cb_ = jnp.dot(incl, la_h, precision=HIGHEST, preferred_element_type=F32)
            rb_ = lax.dot_general(la_h, upp if d == 0 else low, TN_DIMS, precision=HIGHEST,
                                  preferred_element_type=F32)
            seg = jnp.where(masks[d], cb_ - rb_, 0.0)
            dec = jnp.where(masks[d], jnp.exp(seg), 0.0)
            xdt = x[:, hh * P:(hh + 1) * P] * dt_h
            y = jnp.dot((scores * dec).astype(BF16), xdt.astype(BF16), preferred_element_type=F32)
            s_t = st_ref[d * 4 + hh]
            y += jnp.dot(cb16, s_t.astype(BF16), preferred_element_type=F32) * jnp.exp(cb_[:, :P])
            last = cb_[C - 1:C, :] if d == 0 else cb_[0:1, :]
            xw = (xdt * jnp.exp(last - cb_)[:, :P]).astype(BF16)
            st_ref[d * 4 + hh] = s_t * jnp.exp(last[:, :P]) + jnp.dot(bt16, xw, preferred_element_type=F32)
            ys.append(y)
        o_ref[0, rs, :] += jnp.concatenate(ys, axis=-1)

    def step(t, carry):
        cf = t
        cb = jnp.where(t < n_ctx_chunks, n_ctx_chunks - 1 - t, n_chunks + n_ctx_chunks - 1 - t)
        chunk(0, pl.multiple_of(cf * C, C))
        chunk(1, pl.multiple_of(cb * C, C))
        return carry

    lax.fori_loop(0, n_chunks, step, 0)

    blk = 256

    def post(i, carry):
        rs = pl.ds(pl.multiple_of(i * blk, blk), blk)
        y = (o_ref[0, rs, :] + par2_ref[0, 0:1, :] * x_ref[0, rs, :]) * _silu(z_ref[0, rs, :])
        o_ref[0, rs, :] = _rms(y, par2_ref[0, 1:2, :])
        return carry

    lax.fori_loop(0, rows // blk, post, 0)


def _ssd_scan(xbc, y3, sel, par, par2):
    nb, rows, _ = xbc.shape
    gw = 4 * SSD_DH
    return pl.pallas_call(
        functools.partial(_ssd_body, CTX // SSD_CHUNK, rows // SSD_CHUNK),
        grid=(nb, 2),
        in_specs=[
            pl.BlockSpec((1, rows, gw), lambda b, g: (b, 0, g)),
            pl.BlockSpec((1, rows, HEAD_W), lambda b, g: (b, 0, 2 * gw // HEAD_W + g)),
            pl.BlockSpec((1, rows, HEAD_W), lambda b, g: (b, 0, 2 * gw // HEAD_W + 2 + g)),
            pl.BlockSpec((1, rows, gw), lambda b, g: (b, 0, OD_Z // gw + g)),
            pl.BlockSpec((1, rows, HEAD_W), lambda b, g: (b, 0, OD_MISC // HEAD_W)),
            pl.BlockSpec((1, 2, HEAD_W, 4 * HEAD_W), lambda b, g: (g, 0, 0, 0)),
            pl.BlockSpec((1, 8, 4 * HEAD_W), lambda b, g: (g, 0, 0)),
            pl.BlockSpec((1, 8, gw), lambda b, g: (g, 0, 0)),
        ],
        out_specs=pl.BlockSpec((1, rows, gw), lambda b, g: (b, 0, g)),
        out_shape=jax.ShapeDtypeStruct((nb, rows, 2 * gw), F32),
        scratch_shapes=[pltpu.VMEM((8, HEAD_W, SSD_DH), F32)],
        compiler_params=_cparams(("parallel", "parallel")),
        name="ssd_scan",
    )(xbc, xbc, xbc, y3, y3, sel, par, par2)


def _outproj_body(h_ref, a_ref, b_ref, g1_ref, sh2_ref, sc2_ref, wo_ref, lng_ref, lnb_ref, wq_ref,
                  sk_ref, h1_ref, hm_ref, st_ref):
    mix = jnp.concatenate([a_ref[...], b_ref[...]], axis=-1).astype(BF16)
    o = jnp.dot(mix, wo_ref[...], preferred_element_type=F32)
    h1 = _layer_norm(DN_ALPHA * h_ref[...] + g1_ref[0] * o, lng_ref[...], lnb_ref[...])
    h1_ref[...] = h1
    hm = (h1 * (1.0 + sc2_ref[0]) + sh2_ref[0]).astype(BF16)
    hm_ref[...] = hm
    qry = jnp.dot(hm, wq_ref[...], preferred_element_type=F32).astype(BF16)
    for j in range(2 * PEER_HEADS):
        qj = qry[:, j * PEER_NKEYS:(j + 1) * PEER_NKEYS]
        st_ref[j * PEER_NKEYS:(j + 1) * PEER_NKEYS, :] = lax.dot_general(
            sk_ref[j], qj, NT_DIMS, preferred_element_type=F32)


def _outproj(h, mix_a, mix_b, modl, w_out, ln_g, ln_b, wq, sk, n_batch):
    n = h.shape[0]
    tpb = n // n_batch // TM
    tok = pl.BlockSpec((TM, D), lambda i: (i, 0))
    half = pl.BlockSpec((TM, D // 2), lambda i: (i, 0))
    vec = pl.BlockSpec((1, D), lambda i: (0, 0))
    nq = wq.shape[1]
    return pl.pallas_call(
        _outproj_body,
        grid=(n // TM,),
        in_specs=[tok, half, half, _mod_spec(2, n_batch, tpb), _mod_spec(3, n_batch, tpb),
                  _mod_spec(4, n_batch, tpb), pl.BlockSpec((D, D), lambda i: (0, 0)), vec, vec,
                  pl.BlockSpec((D, nq), lambda i: (0, 0)),
                  pl.BlockSpec(sk.shape, lambda i: (0, 0, 0))],
        out_specs=[tok, tok, pl.BlockSpec((nq, TM), lambda i: (0, i))],
        out_shape=[jax.ShapeDtypeStruct((n, D), F32), jax.ShapeDtypeStruct((n, D), BF16),
                   jax.ShapeDtypeStruct((nq, n), F32)],
        compiler_params=_cparams(("parallel",)),
        name="outproj",
    )(h, mix_a, mix_b, modl, modl, modl, w_out, ln_g, ln_b, wq, sk)


def _top_values(x, k):
    n = x.shape[0]
    rid = lax.broadcasted_iota(jnp.int32, x.shape, 0)
    outs = []
    for _ in range(k):
        m = jnp.max(x, axis=0, keepdims=True)
        outs.append(m)
        first = jnp.min(jnp.where(x == m, rid, n), axis=0, keepdims=True)
        x = jnp.where(rid == first, -jnp.inf, x)
    return jnp.concatenate(outs, axis=0)


def _stats_body(st_ref, o_ref):
    k = PEER_TOPK
    t1 = _top_values(st_ref[0:PEER_NKEYS, :], k)
    t2 = _top_values(st_ref[PEER_NKEYS:2 * PEER_NKEYS, :], k)
    cand = jnp.concatenate([t1[r:r + 1, :] + t2 for r in range(k)], axis=0)
    best = _top_values(cand, k)
    m = best[0:1, :]
    z = jnp.sum(jnp.exp(best - m), axis=0, keepdims=True)
    tau = best[k - 1:k, :]
    pad = jnp.zeros((5, m.shape[1]), F32)
    o_ref[0] = jnp.concatenate([t1[0:1, :], t2[0:1, :] + jnp.log(z), tau, pad], axis=0)


def _peer_stats(st):
    n = st.shape[1]
    ts = 256
    return pl.pallas_call(
        _stats_body,
        grid=(n // ts, PEER_HEADS),
        in_specs=[pl.BlockSpec((2 * PEER_NKEYS, ts), lambda i, h: (h, i))],
        out_specs=pl.BlockSpec((1, 8, ts), lambda i, h: (h, 0, i)),
        out_shape=jax.ShapeDtypeStruct((PEER_HEADS, 8, n), F32),
        compiler_params=_cparams(("parallel", "parallel")),
        name="peer_stats",
    )(st)


def _expert_body(hm_ref, st_ref, stats_ref, u_ref, vt_ref, y_ref, e_ref, a_ref, p_ref, acc_ref):
    e = pl.program_id(1)
    ne = pl.num_programs(1)
    t = hm_ref.shape[0]
    nk = PEER_NKEYS
    m = PEER_EB // nk
    ntc = t // 128

    @pl.when(e == 0)
    def _():
        for h in range(PEER_HEADS):
            r1 = slice(2 * h * nk, (2 * h + 1) * nk)
            r2 = slice((2 * h + 1) * nk, (2 * h + 2) * nk)
            e_ref[r1, :] = jnp.exp(st_ref[r1, :] - stats_ref[h, 0:1, :])
            e_ref[r2, :] = jnp.exp(st_ref[r2, :] - stats_ref[h, 1:2, :])
        acc_ref[...] = jnp.zeros(acc_ref.shape, F32)

    a_ref[...] = lax.dot_general(u_ref[...], hm_ref[...], NT_DIMS, preferred_element_type=F32)

    def build(tc, carry):
        col = pl.ds(pl.multiple_of(tc * 128, 128), 128)
        own = [pl.ds(pl.multiple_of(2 * h * nk + e * m, m), m) for h in range(PEER_HEADS)]
        s1 = [st_ref[own[h], col] for h in range(PEER_HEADS)]
        e1 = [e_ref[own[h], col] for h in range(PEER_HEADS)]
        for i1l in range(m):
            gate = jnp.zeros((nk, 128), F32)
            for h in range(PEER_HEADS):
                r2 = slice((2 * h + 1) * nk, (2 * h + 2) * nk)
                c = s1[h][i1l:i1l + 1, :] + st_ref[r2, col]
                gate += jnp.where(c >= stats_ref[h, 2:3, col], e1[h][i1l:i1l + 1, :] * e_ref[r2, col], 0.0)
            rows = slice(i1l * nk, (i1l + 1) * nk)
            act = jax.nn.gelu(a_ref[rows, col], approximate=True)
            p_ref[rows, col] = (act * gate).astype(BF16)
        return carry

    lax.fori_loop(0, ntc, build, 0)
    acc_ref[...] += jnp.dot(vt_ref[...], p_ref[...], preferred_element_type=F32)

    @pl.when(e == ne - 1)
    def _():
        y_ref[...] = acc_ref[...].T


def _peer_experts(hm, st, stats, u, vt):
    n = hm.shape[0]
    ne = u.shape[0] // PEER_EB
    t = PEER_T
    return pl.pallas_call(
        _expert_body,
        grid=(n // t, ne),
        in_specs=[
            pl.BlockSpec((t, D), lambda i, e: (i, 0)),
            pl.BlockSpec((st.shape[0], t), lambda i, e: (0, i)),
            pl.BlockSpec((PEER_HEADS, 8, t), lambda i, e: (0, 0, i)),
            pl.BlockSpec((PEER_EB, D), lambda i, e: (e, 0)),
            pl.BlockSpec((D, PEER_EB), lambda i, e: (0, e)),
        ],
        out_specs=pl.BlockSpec((t, D), lambda i, e: (i, 0)),
        out_shape=jax.ShapeDtypeStruct((n, D), F32),
        scratch_shapes=[pltpu.VMEM((st.shape[0], t), F32), pltpu.VMEM((PEER_EB, t), F32),
                        pltpu.VMEM((PEER_EB, t), BF16), pltpu.VMEM((D, t), F32)],
        compiler_params=_cparams(("parallel", "arbitrary")),
        name="peer_experts",
    )(hm, st, stats, u, vt)


def _rope_tables(rows_lat):
    pairs = HEAD_W // 4
    t = jnp.arange(rows_lat)
    inv = ROPE_THETA ** (-jnp.arange(pairs, dtype=F32) / pairs)
    ang = jnp.concatenate([(t // GRID_W).astype(F32)[:, None] * inv,
                           (t % GRID_W).astype(F32)[:, None] * inv], axis=-1)
    cos, sin = jnp.cos(ang), jnp.sin(ang)
    zero = jnp.zeros_like(sin)
    c = jnp.stack([cos, cos], -1).reshape(rows_lat, HEAD_W)
    sa = jnp.stack([-sin, zero], -1).reshape(rows_lat, HEAD_W)
    sb = jnp.stack([zero, sin], -1).reshape(rows_lat, HEAD_W)
    ident = jnp.ones((CTX, HEAD_W), F32)
    none = jnp.zeros((CTX, HEAD_W), F32)
    return (jnp.concatenate([ident, c]), jnp.concatenate([none, sa]), jnp.concatenate([none, sb]))


def _odd_column_map():
    src = -np.ones((OD_W,), np.int64)
    for h in range(4):
        src[OD_Q + h * HEAD_W:OD_Q + h * HEAD_W + GLA_DK] = np.arange(h * GLA_DK, (h + 1) * GLA_DK)
        src[OD_K + h * HEAD_W:OD_K + h * HEAD_W + GLA_DK] = 256 + np.arange(h * GLA_DK, (h + 1) * GLA_DK)
    src[OD_V:OD_V + 512] = 512 + np.arange(512)
    src[OD_G:OD_G + 512] = 1024 + np.arange(512)
    src[OD_Z:OD_Z + 512] = 1568 + np.arange(512)
    src[OD_XS:OD_XS + 512] = 2080 + np.arange(512)
    src[OD_BM:OD_BM + 256] = 2592 + np.arange(256)
    src[OD_CM:OD_CM + 256] = 2848 + np.arange(256)
    src[OD_MISC + MISC_LRF:OD_MISC + MISC_LRF + 16] = 1536 + np.arange(16)
    src[OD_MISC + MISC_LRB:OD_MISC + MISC_LRB + 16] = 1552 + np.arange(16)
    src[OD_MISC + MISC_DTF:OD_MISC + MISC_DTF + 8] = 3104 + np.arange(8)
    src[OD_MISC + MISC_DTB:OD_MISC + MISC_DTB + 8] = 3112 + np.arange(8)
    return src


def _relayout_odd_w(w):
    src = _odd_column_map()
    cols = jnp.take(w, jnp.asarray(np.maximum(src, 0)), axis=1)
    return jnp.where(jnp.asarray(src >= 0)[None, :], cols, 0.0)


def _pad_rows(rows_list, width):
    out = [jnp.broadcast_to(jnp.asarray(r, F32).reshape(1, width), (1, width)) for r in rows_list]
    out += [jnp.zeros((1, width), F32)] * (8 - len(out))
    return jnp.concatenate(out, axis=0)


def _hgrn_params(lb, gain):
    la = jnp.log(jnp.maximum(lb, LB_FLOOR))
    l1 = jnp.log1p(-lb)
    oml = 1.0 - lb
    heads = []
    for h in range(4):
        s = slice(h * HEAD_W, (h + 1) * HEAD_W)
        heads.append(_pad_rows([la[0, s], la[1, s], l1[0, s], l1[1, s], oml[0, s], oml[1, s], gain], HEAD_W))
    return jnp.stack(heads)


def _gla_params(gate_w, gate_b, gain):
    wg = jnp.zeros((4, 2, HEAD_W, HEAD_W), F32)
    heads = []
    for h in range(4):
        s = slice(h * GLA_DK, (h + 1) * GLA_DK)
        for d, off in enumerate((MISC_LRF, MISC_LRB)):
            wg = wg.at[h, d, off:off + 16, :GLA_DK].set(gate_w[d][:, s])
        pad = jnp.zeros((HEAD_W - GLA_DK,), F32)
        heads.append(_pad_rows([jnp.concatenate([gate_b[0, s], pad]), jnp.concatenate([gate_b[1, s], pad]), gain],
                               HEAD_W))
    return wg.astype(BF16), jnp.stack(heads)


def _ssd_params(dt_bias, a_log, d_skip, norm_g):
    sel = np.zeros((2, 2, HEAD_W, 4 * HEAD_W), np.float32)
    for g in range(2):
        for d, off in enumerate((MISC_DTF, MISC_DTB)):
            for hh in range(4):
                sel[g, d, off + 4 * g + hh, hh * HEAD_W:(hh + 1) * HEAD_W] = 1.0
    par, par2 = [], []
    neg_a = -jnp.exp(a_log.astype(F32))
    for g in range(2):
        hs = slice(4 * g, 4 * g + 4)
        rep = lambda v: jnp.repeat(v[hs], HEAD_W)
        par.append(_pad_rows([rep(dt_bias[0]), rep(dt_bias[1]), rep(neg_a[0]), rep(neg_a[1])], 4 * HEAD_W))
        par2.append(_pad_rows([jnp.repeat(d_skip[hs], SSD_DH), norm_g[g * 256:(g + 1) * 256]], 4 * SSD_DH))
    return jnp.asarray(sel), jnp.stack(par), jnp.stack(par2)


def kernel(x, c, ctx, c_ctx, mod_w, mod_b, ln_g, ln_b, ev_w_in, ev_w_out, hg_lb_logits, hg_norm_g, at_q_norm_g, at_k_norm_g, od_w_in, od_w_out, gla_gate_w, gla_gate_b, gla_norm_g, ssd_conv_w, ssd_conv_b, ssd_dt_bias, ssd_a_log, ssd_d, ssd_norm_g, peer_wq, peer_subkeys, peer_u, peer_v):
    nb, seq, _ = x.shape
    assert ctx.shape[1] == CTX and seq % TM == 0 and nb + 1 <= MOD_ROWS
    rows = CTX + seq
    n = nb * rows

    cc = jnp.concatenate([c, c_ctx[None, :], jnp.zeros((MOD_ROWS - nb - 1, D), F32)], axis=0)
    mods = _modulation(cc, mod_w, mod_b)
    mods = mods.reshape(DEPTH, MOD_ROWS, N_MOD, D).transpose(0, 2, 1, 3).reshape(DEPTH, N_MOD * MOD_ROWS, 1, D)

    sm = jax.nn.softmax(hg_lb_logits.astype(F32), axis=1)
    hg_lb = jnp.cumsum(sm, axis=1) - sm[:, :1]
    tables = _rope_tables(seq)

    h = jnp.concatenate([ctx, x], axis=1).reshape(n, D)
    pre = None
    for l in range(DEPTH):
        j = l // 2
        modl = mods[l]
        if l % 2 == 0:
            y, h = _inproj(h, ev_w_in[j].astype(BF16), modl, nb, pre)
            y3 = y.reshape(nb, rows, EV_W)
            mix_a = _vscan("hgrn", y3, (EV_Q, EV_I, EV_ZF, EV_ZB, EV_G),
                           _hgrn_params(hg_lb[:, j], hg_norm_g[j]))
            mix_b = _attention(y3, tables, at_q_norm_g[j].reshape(1, HEAD_W), at_k_norm_g[j].reshape(1, HEAD_W))
            w_out = ev_w_out[j]
        else:
            y, h = _inproj(h, _relayout_odd_w(od_w_in[j]).astype(BF16), modl, nb, pre)
            y3 = y.reshape(nb, rows, OD_W)
            wg, gpar = _gla_params(gla_gate_w[j], gla_gate_b[j], gla_norm_g[j])
            mix_a = _vscan("gla", y3, (OD_Q, OD_K, OD_V, OD_G), gpar, wg)
            xbc = _ssd_conv(y3, ssd_conv_w[j], ssd_conv_b[j])
            sel, spar, spar2 = _ssd_params(ssd_dt_bias[j], ssd_a_log[j], ssd_d[j], ssd_norm_g[j])
            mix_b = _ssd_scan(xbc, y3, sel, spar, spar2)
            w_out = od_w_out[j]
        lg1, lb1 = ln_g[l, 0].reshape(1, D), ln_b[l, 0].reshape(1, D)
        sk = peer_subkeys[l].reshape(2 * PEER_HEADS, PEER_NKEYS, PEER_NKEYS).astype(BF16)
        h1, hm, st = _outproj(h, mix_a.reshape(n, D // 2), mix_b.reshape(n, D // 2), modl,
                              w_out.astype(BF16), lg1, lb1, peer_wq[l].astype(BF16), sk, nb)
        stats = _peer_stats(st)
        yp = _peer_experts(hm, st, stats, peer_u[l].astype(BF16), peer_v[l].astype(BF16).T)
        pre = (yp, modl, ln_g[l, 1].reshape(1, D), ln_b[l, 1].reshape(1, D))
        h = h1
    out = _final(h, pre[0], pre[1], pre[2], pre[3], nb)
    return out.reshape(nb, seq, D)
```

```python
import functools
import math

import numpy as np
import jax
import jax.numpy as jnp
from jax import lax
from jax.experimental import pallas as pl
from jax.experimental.pallas import tpu as pltpu

F32 = jnp.float32
BF16 = jnp.bfloat16
HIGHEST = lax.Precision.HIGHEST

D = 1024
DEPTH = 4
N_MOD = 6
CTX = 256
GRID_W = 64
DN_ALPHA = (2 * DEPTH) ** 0.25
NORM_EPS = 1e-6
ROPE_THETA = 10000.0
LB_FLOOR = 1e-30
TM = 256
MOD_ROWS = 24
VMEM_LIMIT = 56 * 1024 * 1024

HEAD_W = 128
VEC_CHUNK = 16
SSD_CHUNK = 128
SSD_DH = 64
GLA_DK = 64
GLA_GATE_NORM = 16.0
PEER_HEADS = 8
PEER_NKEYS = 128
PEER_TOPK = 16
PEER_T = 512
PEER_EB = 1024

EV_Q, EV_I, EV_ZF, EV_ZB, EV_G, EV_AQ, EV_AK, EV_AV = 0, 512, 1024, 1536, 2048, 2560, 3072, 3328
EV_W = 3584
OD_Q, OD_K, OD_V, OD_G, OD_Z, OD_XS, OD_BM, OD_CM, OD_MISC = 0, 512, 1024, 1536, 2048, 2560, 3072, 3328, 3584
OD_W = 3712
MISC_LRF, MISC_LRB, MISC_DTF, MISC_DTB = 0, 16, 32, 40

NT_DIMS = (((1,), (1,)), ((), ()))
TN_DIMS = (((0,), (0,)), ((), ()))


def _cparams(sem):
    return pltpu.CompilerParams(dimension_semantics=sem, vmem_limit_bytes=VMEM_LIMIT)


def _layer_norm(y, g, b):
    mu = jnp.mean(y, axis=-1, keepdims=True)
    yc = y - mu
    var = jnp.mean(yc * yc, axis=-1, keepdims=True)
    return yc * lax.rsqrt(var + NORM_EPS) * g + b


def _rms(y, g):
    return y * lax.rsqrt(jnp.mean(y * y, axis=-1, keepdims=True) + NORM_EPS) * g


def _log_sigmoid(z):
    return jnp.minimum(z, 0.0) - jnp.log(1.0 + jnp.exp(-jnp.abs(z)))


def _silu(z):
    return z * jax.nn.sigmoid(z)


def _split_hi_lo(a):
    hi = a.astype(BF16)
    lo = (a - hi.astype(F32)).astype(BF16)
    return hi, lo


def _mod_body(c_ref, w_ref, b_ref, o_ref):
    a = _silu(c_ref[...])
    a_hi, a_lo = _split_hi_lo(a)
    w_hi, w_lo = _split_hi_lo(w_ref[0])
    acc = jnp.dot(a_hi, w_hi, preferred_element_type=F32)
    acc += jnp.dot(a_lo, w_hi, preferred_element_type=F32)
    acc += jnp.dot(a_hi, w_lo, preferred_element_type=F32)
    o_ref[0] = acc + b_ref[0]


def _modulation(cc, mod_w, mod_b):
    tn = 1536
    nw = mod_w.shape[-1]
    return pl.pallas_call(
        _mod_body,
        grid=(DEPTH, nw // tn),
        in_specs=[
            pl.BlockSpec((MOD_ROWS, D), lambda l, j: (0, 0)),
            pl.BlockSpec((1, D, tn), lambda l, j: (l, 0, j)),
            pl.BlockSpec((1, 1, tn), lambda l, j: (l, 0, j)),
        ],
        out_specs=pl.BlockSpec((1, MOD_ROWS, tn), lambda l, j: (l, 0, j)),
        out_shape=jax.ShapeDtypeStruct((DEPTH, MOD_ROWS, nw), F32),
        compiler_params=_cparams(("parallel", "parallel")),
        name="modulation",
    )(cc, mod_w, mod_b.reshape(DEPTH, 1, nw))


def _mod_spec(k, n_batch, tiles_per_batch):
    def index(i):
        row = jnp.where(i % tiles_per_batch == 0, n_batch, i // tiles_per_batch)
        return (k * MOD_ROWS + row, 0, 0)
    return pl.BlockSpec((1, 1, D), index)


def _inproj_body(has_pre, *refs):
    if has_pre:
        x_ref, y_ref, g2_ref, lng_ref, lnb_ref, sh_ref, sc_ref, w_ref, o_ref, h_ref = refs
        h = _layer_norm(DN_ALPHA * x_ref[...] + g2_ref[0] * y_ref[...], lng_ref[...], lnb_ref[...])
        h_ref[...] = h
    else:
        x_ref, sh_ref, sc_ref, w_ref, o_ref = refs
        h = x_ref[...]
    u = h * (1.0 + sc_ref[0]) + sh_ref[0]
    o_ref[...] = jnp.dot(u.astype(BF16), w_ref[...], preferred_element_type=F32)


def _inproj(h, w, modl, n_batch, pre=None):
    n = h.shape[0]
    nw = w.shape[1]
    tpb = n // n_batch // TM
    tok = pl.BlockSpec((TM, D), lambda i: (i, 0))
    vec = pl.BlockSpec((1, D), lambda i: (0, 0))
    in_specs = [tok]
    args = [h]
    if pre is not None:
        y, prev_modl, ln_g, ln_b = pre
        in_specs += [tok, _mod_spec(5, n_batch, tpb), vec, vec]
        args += [y, prev_modl, ln_g, ln_b]
    in_specs += [_mod_spec(0, n_batch, tpb), _mod_spec(1, n_batch, tpb),
                 pl.BlockSpec((D, nw), lambda i: (0, 0))]
    args += [modl, modl, w]
    out_specs = [pl.BlockSpec((TM, nw), lambda i: (i, 0))]
    out_shape = [jax.ShapeDtypeStruct((n, nw), F32)]
    if pre is not None:
        out_specs.append(tok)
        out_shape.append(jax.ShapeDtypeStruct((n, D), F32))
    res = pl.pallas_call(
        functools.partial(_inproj_body, pre is not None),
        grid=(n // TM,),
        in_specs=in_specs,
        out_specs=out_specs,
        out_shape=out_shape,
        compiler_params=_cparams(("parallel",)),
        name="inproj",
    )(*args)
    return (res[0], res[1]) if pre is not None else (res[0], h)


def _final_body(x_ref, y_ref, g2_ref, lng_ref, lnb_ref, o_ref):
    o_ref[...] = _layer_norm(DN_ALPHA * x_ref[...] + g2_ref[0] * y_ref[...], lng_ref[...], lnb_ref[...])


def _final(h, y, modl, ln_g, ln_b, n_batch):
    n = h.shape[0]
    tpb = n // n_batch // TM
    lat = tpb - 1
    tok = pl.BlockSpec((TM, D), lambda b, j: (b * tpb + 1 + j, 0))
    vec = pl.BlockSpec((1, D), lambda b, j: (0, 0))
    return pl.pallas_call(
        _final_body,
        grid=(n_batch, lat),
        in_specs=[tok, tok, pl.BlockSpec((1, 1, D), lambda b, j: (5 * MOD_ROWS + b, 0, 0)), vec, vec],
        out_specs=pl.BlockSpec((TM, D), lambda b, j: (b * lat + j, 0)),
        out_shape=jax.ShapeDtypeStruct((n_batch * lat * TM, D), F32),
        compiler_params=_cparams(("parallel", "parallel")),
        name="final_norm",
    )(h, y, modl, ln_g, ln_b)


def _vscan_body(mode, nblk, *refs):
    C = VEC_CHUNK
    nh = 4
    if mode == "hgrn":
        qf_ref, vf_ref, zf_ref, qb_ref, vb_ref, zb_ref, g_ref, par_ref, o_ref, st_ref, ob_ref = refs
        dir_refs = ((qf_ref, vf_ref, zf_ref), (qb_ref, vb_ref, zb_ref))
    else:
        (qf_ref, kf_ref, vf_ref, mf_ref, qb_ref, kb_ref, vb_ref, mb_ref, g_ref, wg_ref, par_ref,
         o_ref, st_ref, ob_ref) = refs
        dir_refs = ((qf_ref, kf_ref, vf_ref, mf_ref), (qb_ref, kb_ref, vb_ref, mb_ref))
    rows = o_ref.shape[1]
    rb = qf_ref.shape[1]
    nch = rb // C
    i = pl.program_id(1)
    bi = jnp.where(i == 0, 0, nblk - i)

    @pl.when(i == 0)
    def _():
        st_ref[...] = jnp.zeros(st_ref.shape, F32)

    w = nh * HEAD_W
    ri = lax.broadcasted_iota(jnp.int32, (rb, rb), 0)
    ci = lax.broadcasted_iota(jnp.int32, (rb, rb), 1)
    rid = lax.broadcasted_iota(jnp.int32, (rb, w), 0)
    levels = [C << s for s in range((rb // C).bit_length() - 1)]
    neg = -1e30

    def group_row(x, grp, r):
        g = x.reshape(rb // grp, grp, w)
        return jnp.broadcast_to(g[:, r:r + 1, :], g.shape).reshape(rb, w)

    def nt(a, b):
        return lax.dot_general(a, b, NT_DIMS, preferred_element_type=F32)

    def same_group(grp):
        sh = grp.bit_length() - 1
        return (ri >> sh) == (ci >> sh)

    def direction(d, out_ref, row0):
        refs_d = dir_refs[d]
        q = refs_d[0][0]
        v16 = refs_d[-2 if mode == "gla" else 1][0].astype(BF16)
        if mode == "hgrn":
            z = refs_d[2][0]
            qh = _silu(q) * (HEAD_W ** -0.5)
            la, l1, oml = par_ref[d:d + 1, :], par_ref[2 + d:3 + d, :], par_ref[4 + d:5 + d, :]
            bb = l1 + _log_sigmoid(z)
            log_f = jnp.maximum(la, bb) + jnp.log(1.0 + jnp.exp(-jnp.abs(la - bb)))
            k = oml * jax.nn.sigmoid(-z)
        else:
            qh = q * (GLA_DK ** -0.5)
            k = refs_d[1][0]
            pre = jnp.dot(refs_d[3][0].astype(BF16), wg_ref[d], preferred_element_type=F32)
            log_f = _log_sigmoid(pre + par_ref[d:d + 1, :]) * (1.0 / GLA_GATE_NORM)
        tri = (ri >= ci if d == 0 else ri <= ci).astype(BF16)
        hi, lo = _split_hi_lo(log_f)
        cc = jnp.dot(tri, jnp.concatenate([hi, lo], axis=-1), preferred_element_type=F32)
        cum = cc[:, :w] + cc[:, w:]
        first_row, last_row = (0, rb - 1) if d == 0 else (rb - 1, 0)

        x0 = cum - group_row(cum, C, first_row % C)
        qs = [(qh * jnp.exp(x0)).astype(BF16)]
        ks = [(k * jnp.exp(-x0)).astype(BF16)]
        for m in levels:
            is_query = (rid & m) != 0 if d == 0 else (rid & m) == 0
            bnd = group_row(cum, 2 * m, m - 1 if d == 0 else m)
            qs.append((qh * jnp.exp(jnp.where(is_query, cum - bnd, neg))).astype(BF16))
            ks.append((k * jnp.exp(jnp.where(is_query, neg, bnd - cum))).astype(BF16))
        last = cum[last_row:last_row + 1, :]
        qe = (qh * jnp.exp(cum)).astype(BF16)
        kl = (k * jnp.exp(last - cum)).astype(BF16)
        dec = jnp.exp(last)

        outs = []
        for hh in range(nh):
            sl = slice(hh * HEAD_W, (hh + 1) * HEAD_W)
            p0 = nt(qs[0][:, sl], ks[0][:, sl])
            causal = ri >= ci if d == 0 else ri <= ci
            att = jnp.where(same_group(C), jnp.where(causal, p0, 0.0), 0.0)
            for lv, m in enumerate(levels):
                p = nt(qs[lv + 1][:, sl], ks[lv + 1][:, sl])
                att += p if 2 * m == rb else jnp.where(same_group(2 * m), p, 0.0)
            o = jnp.dot(att.astype(BF16), v16[:, sl], preferred_element_type=F32)
            s_t = st_ref[d * nh + hh]
            o += nt(qe[:, sl], s_t.astype(BF16))
            kv_t = lax.dot_general(v16[:, sl], kl[:, sl], TN_DIMS, preferred_element_type=F32)
            st_ref[d * nh + hh] = s_t * dec[:, sl] + kv_t
            outs.append(o)
        out_ref[row0, :] = jnp.concatenate(outs, axis=-1)

    direction(0, o_ref.at[0], pl.ds(pl.multiple_of(i * rb, rb), rb))
    direction(1, ob_ref, pl.ds(pl.multiple_of(bi * rb, rb), rb))

    @pl.when(i == nblk - 1)
    def _():
        def post(j, carry):
            rs = pl.ds(pl.multiple_of(j * rb, rb), rb)
            for hh in range(nh):
                lanes = pl.ds(hh * HEAD_W, HEAD_W)
                o = o_ref[0, rs, lanes] + ob_ref[rs, lanes]
                g = g_ref[0, rs, lanes]
                if mode == "hgrn":
                    res = _rms(o * jax.nn.sigmoid(g), par_ref[6:7, lanes])
                else:
                    res = _rms(o, par_ref[2:3, lanes]) * _silu(g)
                o_ref[0, rs, lanes] = res
            return carry

        lax.fori_loop(0, rows // rb, post, 0)


def _vscan(mode, y3, col_offsets, g_offset, params, wg=None):
    nb, rows, _ = y3.shape
    width = 4 * HEAD_W
    rb = CTX
    nblk = rows // rb
    fwd = lambda w, off: pl.BlockSpec((1, rb, w), lambda b, i: (b, i, off // w))
    bwd = lambda w, off: pl.BlockSpec((1, rb, w), lambda b, i: (b, jnp.where(i == 0, 0, nblk - i), off // w))
    in_specs, args = [], []
    for mk in (fwd, bwd):
        for off in col_offsets[mk is bwd]:
            in_specs.append(mk(width, off))
            args.append(y3)
        if mode == "gla":
            in_specs.append(mk(HEAD_W, OD_MISC))
            args.append(y3)
    in_specs.append(pl.BlockSpec((1, rows, width), lambda b, i: (b, 0, g_offset // width)))
    args.append(y3)
    if mode == "gla":
        in_specs.append(pl.BlockSpec((2, HEAD_W, width), lambda b, i: (0, 0, 0)))
        args.append(wg)
    in_specs.append(pl.BlockSpec((8, width), lambda b, i: (0, 0)))
    args.append(params)
    return pl.pallas_call(
        functools.partial(_vscan_body, mode, nblk),
        grid=(nb, nblk),
        in_specs=in_specs,
        out_specs=pl.BlockSpec((1, rows, width), lambda b, i: (b, 0, 0)),
        out_shape=jax.ShapeDtypeStruct((nb, rows, width), F32),
        scratch_shapes=[pltpu.VMEM((8, HEAD_W, HEAD_W), F32), pltpu.VMEM((rows, width), F32)],
        compiler_params=_cparams(("parallel", "arbitrary")),
        name="vscan_" + mode,
    )(*args)


def _rope(x, c, sa, sb):
    return x * c + pltpu.roll(x, HEAD_W - 1, 1) * sa + pltpu.roll(x, 1, 1) * sb


def _attn_body(q_ref, k_ref, v_ref, ck_ref, sak_ref, sbk_ref, cq_ref, saq_ref, sbq_ref,
               qg_ref, kg_ref, o_ref, kp_ref, vp_ref):
    qi = pl.program_id(2)
    rows = k_ref.shape[1]
    blk = 256

    @pl.when(qi == 0)
    def _():
        def prep(i, carry):
            rs = pl.ds(pl.multiple_of(i * blk, blk), blk)
            kn = _rms(k_ref[0, rs, :], kg_ref[...])
            kp_ref[rs, :] = _rope(kn, ck_ref[rs, :], sak_ref[rs, :], sbk_ref[rs, :]).astype(BF16)
            vp_ref[rs, :] = v_ref[0, rs, :].astype(BF16)
            return carry
        lax.fori_loop(0, rows // blk, prep, 0)

    def attend(nk):
        outs = []
        for r in range(2):
            qn = _rms(q_ref[0, :, r * HEAD_W:(r + 1) * HEAD_W], qg_ref[...])
            qr = (_rope(qn, cq_ref[...], saq_ref[...], sbq_ref[...]) * (HEAD_W ** -0.5)).astype(BF16)
            s = lax.dot_general(qr, kp_ref[0:nk, :], NT_DIMS, preferred_element_type=F32)
            m = jnp.max(s, axis=-1, keepdims=True)
            p = jnp.exp(s - m)
            l = jnp.sum(p, axis=-1, keepdims=True)
            o = jnp.dot(p.astype(BF16), vp_ref[0:nk, :], preferred_element_type=F32)
            outs.append(o / l)
        o_ref[0] = jnp.concatenate(outs, axis=-1)

    @pl.when(qi == 0)
    def _():
        attend(CTX)

    @pl.when(qi > 0)
    def _():
        attend(rows)


def _attention(y3, tables, q_gain, k_gain):
    nb, rows, _ = y3.shape
    c, sa, sb = tables
    full = pl.BlockSpec((rows, HEAD_W), lambda b, g, i: (0, 0))
    tile = pl.BlockSpec((TM, HEAD_W), lambda b, g, i: (i, 0))
    vec = pl.BlockSpec((1, HEAD_W), lambda b, g, i: (0, 0))
    return pl.pallas_call(
        _attn_body,
        grid=(nb, 2, rows // TM),
        in_specs=[
            pl.BlockSpec((1, TM, 2 * HEAD_W), lambda b, g, i: (b, i, EV_AQ // (2 * HEAD_W) + g)),
            pl.BlockSpec((1, rows, HEAD_W), lambda b, g, i: (b, 0, EV_AK // HEAD_W + g)),
            pl.BlockSpec((1, rows, HEAD_W), lambda b, g, i: (b, 0, EV_AV // HEAD_W + g)),
            full, full, full, tile, tile, tile, vec, vec,
        ],
        out_specs=pl.BlockSpec((1, TM, 2 * HEAD_W), lambda b, g, i: (b, i, g)),
        out_shape=jax.ShapeDtypeStruct((nb, rows, 4 * HEAD_W), F32),
        scratch_shapes=[pltpu.VMEM((rows, HEAD_W), BF16), pltpu.VMEM((rows, HEAD_W), BF16)],
        compiler_params=_cparams(("parallel", "parallel", "arbitrary")),
        name="attention",
    )(y3, y3, y3, c, sa, sb, c, sa, sb, q_gain, k_gain)


def _conv_body(x_ref, w_ref, b_ref, o_ref):
    rows = x_ref.shape[1]
    x = x_ref[0]
    row = lax.broadcasted_iota(jnp.int32, x.shape, 0)
    lo = jnp.where(row < CTX, 0, CTX)
    hi = jnp.where(row < CTX, CTX, rows)
    taps = w_ref.shape[0]
    acc = jnp.zeros(x.shape, F32) + b_ref[...]
    for j in range(taps):
        off = j - taps // 2
        xs = x if off == 0 else pltpu.roll(x, (-off) % rows, 0)
        src = row + off
        xs = jnp.where(src >= lo, jnp.where(src < hi, xs, 0.0), 0.0)
        acc += xs * w_ref[j:j + 1, :]
    o_ref[0] = _silu(acc)


def _ssd_conv(y3, conv_w, conv_b):
    nb, rows, _ = y3.shape
    nc = conv_w.shape[1]
    return pl.pallas_call(
        _conv_body,
        grid=(nb, nc // HEAD_W),
        in_specs=[
            pl.BlockSpec((1, rows, HEAD_W), lambda b, j: (b, 0, OD_XS // HEAD_W + j)),
            pl.BlockSpec((conv_w.shape[0], HEAD_W), lambda b, j: (0, j)),
            pl.BlockSpec((1, HEAD_W), lambda b, j: (0, j)),
        ],
        out_specs=pl.BlockSpec((1, rows, HEAD_W), lambda b, j: (b, 0, j)),
        out_shape=jax.ShapeDtypeStruct((nb, rows, nc), F32),
        compiler_params=_cparams(("parallel", "parallel")),
        name="ssd_conv",
    )(y3, conv_w, conv_b.reshape(1, nc))


def _ssd_body(n_ctx_chunks, n_chunks, x_ref, bm_ref, cm_ref, z_ref, misc_ref, sel_ref, par_ref,
              par2_ref, o_ref, st_ref):
    C = SSD_CHUNK
    P = SSD_DH
    rows = o_ref.shape[1]
    o_ref[...] = jnp.zeros(o_ref.shape, F32)
    st_ref[...] = jnp.zeros(st_ref.shape, F32)
    ri = lax.broadcasted_iota(jnp.int32, (C, C), 0)
    ci = lax.broadcasted_iota(jnp.int32, (C, C), 1)
    masks = (ri >= ci, ri <= ci)
    low = masks[0].astype(F32)
    upp = masks[1].astype(F32)

    def chunk(d, r0):
        rs = pl.ds(r0, C)
        x = x_ref[0, rs, :]
        bmat = bm_ref[0, rs, :]
        cb16 = cm_ref[0, rs, :].astype(BF16)
        b16 = bmat.astype(BF16)
        bt16 = bmat.T.astype(BF16)
        scores = lax.dot_general(cb16, b16, NT_DIMS, preferred_element_type=F32)
        dt = jnp.dot(misc_ref[0, rs, :], sel_ref[0, d], precision=HIGHEST, preferred_element_type=F32)
        dt = dt + par_ref[0, d:d + 1, :]
        dt = jnp.maximum(dt, 0.0) + jnp.log(1.0 + jnp.exp(-jnp.abs(dt)))
        la = dt * par_ref[0, 2 + d:3 + d, :]
        ys = []
        for hh in range(4):
            la_h = la[:, hh * HEAD_W:(hh + 1) * HEAD_W]
            dt_h = dt[:, hh * HEAD_W:hh * HEAD_W + P]
            incl = low if d == 0 else upp
            cb_ = jnp.dot(incl, la_h, precision=HIGHEST, preferred_element_type=F32)
            rb_ = lax.dot_general(la_h, upp if d == 0 else low, TN_DIMS, precision=HIGHEST,
                                  preferred_element_type=F32)
            seg = jnp.where(masks[d], cb_ - rb_, 0.0)
            dec = jnp.where(masks[d], jnp.exp(seg), 0.0)
            xdt = x[:, hh * P:(hh + 1) * P] * dt_h
            y = jnp.dot((scores * dec).astype(BF16), xdt.astype(BF16), preferred_element_type=F32)
            s_t = st_ref[d * 4 + hh]
            y += jnp.dot(cb16, s_t.astype(BF16), preferred_element_type=F32) * jnp.exp(cb_[:, :P])
            last = cb_[C - 1:C, :] if d == 0 else cb_[0:1, :]
            xw = (xdt * jnp.exp(last - cb_)[:, :P]).astype(BF16)
            st_ref[d * 4 + hh] = s_t * jnp.exp(last[:, :P]) + jnp.dot(bt16, xw, preferred_element_type=F32)
            ys.append(y)
        o_ref[0, rs, :] += jnp.concatenate(ys, axis=-1)

    def step(t, carry):
        cf = t
        cb = jnp.where(t < n_ctx_chunks, n_ctx_chunks - 1 - t, n_chunks + n_ctx_chunks - 1 - t)
        chunk(0, pl.multiple_of(cf * C, C))
        chunk(1, pl.multiple_of(cb * C, C))
        return carry

    lax.fori_loop(0, n_chunks, step, 0)

    blk = 256

    def post(i, carry):
        rs = pl.ds(pl.multiple_of(i * blk, blk), blk)
        y = (o_ref[0, rs, :] + par2_ref[0, 0:1, :] * x_ref[0, rs, :]) * _silu(z_ref[0, rs, :])
        o_ref[0, rs, :] = _rms(y, par2_ref[0, 1:2, :])
        return carry

    lax.fori_loop(0, rows // blk, post, 0)


def _ssd_scan(xbc, y3, sel, par, par2):
    nb, rows, _ = xbc.shape
    gw = 4 * SSD_DH
    return pl.pallas_call(
        functools.partial(_ssd_body, CTX // SSD_CHUNK, rows // SSD_CHUNK),
        grid=(nb, 2),
        in_specs=[
            pl.BlockSpec((1, rows, gw), lambda b, g: (b, 0, g)),
            pl.BlockSpec((1, rows, HEAD_W), lambda b, g: (b, 0, 2 * gw // HEAD_W + g)),
            pl.BlockSpec((1, rows, HEAD_W), lambda b, g: (b, 0, 2 * gw // HEAD_W + 2 + g)),
            pl.BlockSpec((1, rows, gw), lambda b, g: (b, 0, OD_Z // gw + g)),
            pl.BlockSpec((1, rows, HEAD_W), lambda b, g: (b, 0, OD_MISC // HEAD_W)),
            pl.BlockSpec((1, 2, HEAD_W, 4 * HEAD_W), lambda b, g: (g, 0, 0, 0)),
            pl.BlockSpec((1, 8, 4 * HEAD_W), lambda b, g: (g, 0, 0)),
            pl.BlockSpec((1, 8, gw), lambda b, g: (g, 0, 0)),
        ],
        out_specs=pl.BlockSpec((1, rows, gw), lambda b, g: (b, 0, g)),
        out_shape=jax.ShapeDtypeStruct((nb, rows, 2 * gw), F32),
        scratch_shapes=[pltpu.VMEM((8, HEAD_W, SSD_DH), F32)],
        compiler_params=_cparams(("parallel", "parallel")),
        name="ssd_scan",
    )(xbc, xbc, xbc, y3, y3, sel, par, par2)


def _outproj_body(h_ref, a_ref, b_ref, g1_ref, sh2_ref, sc2_ref, wo_ref, lng_ref, lnb_ref, wq_ref,
                  sk_ref, h1_ref, hm_ref, st_ref):
    mix = jnp.concatenate([a_ref[...], b_ref[...]], axis=-1).astype(BF16)
    o = jnp.dot(mix, wo_ref[...], preferred_element_type=F32)
    h1 = _layer_norm(DN_ALPHA * h_ref[...] + g1_ref[0] * o, lng_ref[...], lnb_ref[...])
    h1_ref[...] = h1
    hm = (h1 * (1.0 + sc2_ref[0]) + sh2_ref[0]).astype(BF16)
    hm_ref[...] = hm
    qry = jnp.dot(hm, wq_ref[...], preferred_element_type=F32).astype(BF16)
    for j in range(2 * PEER_HEADS):
        qj = qry[:, j * PEER_NKEYS:(j + 1) * PEER_NKEYS]
        st_ref[j * PEER_NKEYS:(j + 1) * PEER_NKEYS, :] = lax.dot_general(
            sk_ref[j], qj, NT_DIMS, preferred_element_type=F32)


def _outproj(h, mix_a, mix_b, modl, w_out, ln_g, ln_b, wq, sk, n_batch):
    n = h.shape[0]
    tpb = n // n_batch // TM
    tok = pl.BlockSpec((TM, D), lambda i: (i, 0))
    half = pl.BlockSpec((TM, D // 2), lambda i: (i, 0))
    vec = pl.BlockSpec((1, D), lambda i: (0, 0))
    nq = wq.shape[1]
    return pl.pallas_call(
        _outproj_body,
        grid=(n // TM,),
        in_specs=[tok, half, half, _mod_spec(2, n_batch, tpb), _mod_spec(3, n_batch, tpb),
                  _mod_spec(4, n_batch, tpb), pl.BlockSpec((D, D), lambda i: (0, 0)), vec, vec,
                  pl.BlockSpec((D, nq), lambda i: (0, 0)),
                  pl.BlockSpec(sk.shape, lambda i: (0, 0, 0))],
        out_specs=[tok, tok, pl.BlockSpec((nq, TM), lambda i: (0, i))],
        out_shape=[jax.ShapeDtypeStruct((n, D), F32), jax.ShapeDtypeStruct((n, D), BF16),
                   jax.ShapeDtypeStruct((nq, n), F32)],
        compiler_params=_cparams(("parallel",)),
        name="outproj",
    )(h, mix_a, mix_b, modl, modl, modl, w_out, ln_g, ln_b, wq, sk)


def _sort16_network():
    comps = []

    def merge(lo, hi, r):
        step = r * 2
        if step < hi - lo:
            merge(lo, hi, step)
            merge(lo + r, hi, step)
            comps.extend((j, j + r) for j in range(lo + r, hi - r, step))
        else:
            comps.append((lo, lo + r))

    def sort(lo, hi):
        if hi - lo >= 1:
            mid = lo + (hi - lo) // 2
            sort(lo, mid)
            sort(mid + 1, hi)
            merge(lo, hi, 1)

    sort(0, 15)
    return comps


SORT16 = _sort16_network()
BITONIC16 = [(j, j + s) for s in (8, 4, 2, 1) for j in range(16) if not j & s]


def _compare_exchange(v, comps):
    v = list(v)
    for a, b in comps:
        v[a], v[b] = jnp.maximum(v[a], v[b]), jnp.minimum(v[a], v[b])
    return v


def _merge_top16(a, b_rev):
    c = [a[k] if b_rev[k] is None else jnp.maximum(a[k], b_rev[k]) for k in range(16)]
    return _compare_exchange(c, BITONIC16)


def _top16_keys(ref, row0, col):
    v = [ref[row0 + 8 * j:row0 + 8 * j + 8, col] for j in range(16)]
    v = _compare_exchange(v, SORT16)
    for shift in (4, 2, 1):
        rolled = [pltpu.roll(x, shift, 0) for x in v]
        v = _merge_top16(v, rolled[::-1])
    return v


def _stats_body(st_ref, o_ref):
    k = PEER_TOPK
    nk = PEER_NKEYS
    sub = lax.broadcasted_iota(jnp.int32, (8, 128), 0)

    def column(tc, carry):
        col = pl.ds(pl.multiple_of(tc * 128, 128), 128)
        tops = []
        for side in range(2):
            packed = None
            for h in range(PEER_HEADS):
                a = _top16_keys(st_ref, (2 * h + side) * nk, col)
                packed = a if h == 0 else [jnp.where(sub == h, a[r], packed[r]) for r in range(k)]
            tops.append(packed)
        t1, t2 = tops
        best = [t1[0] + t2[r] for r in range(k)]
        for r1 in range(1, k):
            ln = k // (r1 + 1)
            lst = [t1[r1] + t2[r] for r in range(ln)]
            best = _merge_top16(best, [None] * (k - ln) + lst[::-1])
        z = 1.0
        for r in range(1, k):
            z = z + jnp.exp(best[r] - best[0])
        o_ref[0, :, col] = t1[0]
        o_ref[1, :, col] = t2[0] + jnp.log(z)
        o_ref[2, :, col] = best[k - 1]
        for r in range(k):
            o_ref[3 + r, :, col] = t2[r]
        return carry

    lax.fori_loop(0, o_ref.shape[2] // 128, column, 0)


def _peer_stats(st):
    n = st.shape[1]
    ts = 512
    return pl.pallas_call(
        _stats_body,
        grid=(n // ts,),
        in_specs=[pl.BlockSpec((st.shape[0], ts), lambda i: (0, i))],
        out_specs=pl.BlockSpec((3 + PEER_TOPK, PEER_HEADS, ts), lambda i: (0, 0, i)),
        out_shape=jax.ShapeDtypeStruct((3 + PEER_TOPK, PEER_HEADS, n), F32),
        compiler_params=_cparams(("parallel",)),
        name="peer_stats",
    )(st)


def _expert_body(hm_ref, st_ref, stats_ref, u_ref, vt_ref, y_ref, e1_ref, n1_ref, e2_ref, r2_ref,
                 a_ref, p_ref, acc_ref):
    e = pl.program_id(1)
    ne = pl.num_programs(1)
    t = hm_ref.shape[0]
    nk = PEER_NKEYS
    k = PEER_TOPK
    m = PEER_EB // nk
    ntc = t // 128

    @pl.when(e == 0)
    def _():
        def prep(tc, carry):
            col = pl.ds(pl.multiple_of(tc * 128, 128), 128)
            for h in range(PEER_HEADS):
                rows = slice(h * nk, (h + 1) * nk)
                s1 = st_ref[2 * h * nk:(2 * h + 1) * nk, col]
                s2 = st_ref[(2 * h + 1) * nk:(2 * h + 2) * nk, col]
                tau = stats_ref[2, h:h + 1, col]
                n1 = jnp.zeros((nk, 128), F32)
                r2 = jnp.zeros((nk, 128), F32)
                for r in range(k):
                    t2r = stats_ref[3 + r, h:h + 1, col]
                    n1 += jnp.where(s1 + t2r >= tau, 1.0, 0.0)
                    r2 += jnp.where(t2r > s2, 1.0, 0.0)
                e1_ref[rows, col] = jnp.exp(s1 - stats_ref[0, h:h + 1, col])
                n1_ref[rows, col] = n1
                e2_ref[rows, col] = jnp.exp(s2 - stats_ref[1, h:h + 1, col]).astype(BF16)
                r2_ref[rows, col] = r2.astype(BF16)
            return carry

        lax.fori_loop(0, ntc, prep, 0)
        acc_ref[...] = jnp.zeros(acc_ref.shape, F32)

    a_ref[...] = lax.dot_general(u_ref[...], hm_ref[...], NT_DIMS, preferred_element_type=F32)

    def build(tc, carry):
        col = pl.ds(pl.multiple_of(tc * 128, 128), 128)
        own = [pl.ds(pl.multiple_of(h * nk + e * m, m), m) for h in range(PEER_HEADS)]
        e1 = [e1_ref[own[h], col] for h in range(PEER_HEADS)]
        n1 = [n1_ref[own[h], col] for h in range(PEER_HEADS)]
        for i1l in range(m):
            gate = jnp.zeros((nk, 128), BF16)
            for h in range(PEER_HEADS):
                rows2 = slice(h * nk, (h + 1) * nk)
                n1b = jnp.broadcast_to(n1[h][i1l:i1l + 1, :], (nk, 128)).astype(BF16)
                e1b = jnp.broadcast_to(e1[h][i1l:i1l + 1, :], (nk, 128)).astype(BF16)
                gate += jnp.where(r2_ref[rows2, col] < n1b, e1b * e2_ref[rows2, col], jnp.zeros((), BF16))
            rows = slice(i1l * nk, (i1l + 1) * nk)
            act = jax.nn.gelu(a_ref[rows, col], approximate=True)
            p_ref[rows, col] = act.astype(BF16) * gate
        return carry

    lax.fori_loop(0, ntc, build, 0)
    acc_ref[...] += jnp.dot(vt_ref[...], p_ref[...], preferred_element_type=F32)

    @pl.when(e == ne - 1)
    def _():
        y_ref[...] = acc_ref[...].T


def _peer_experts(hm, st, stats, u, vt):
    n = hm.shape[0]
    ne = u.shape[0] // PEER_EB
    assert PEER_EB // PEER_NKEYS == 8
    t = PEER_T
    half = (PEER_HEADS * PEER_NKEYS, t)
    return pl.pallas_call(
        _expert_body,
        grid=(n // t, ne),
        in_specs=[
            pl.BlockSpec((t, D), lambda i, e: (i, 0)),
            pl.BlockSpec((st.shape[0], t), lambda i, e: (0, i)),
            pl.BlockSpec((stats.shape[0], PEER_HEADS, t), lambda i, e: (0, 0, i)),
            pl.BlockSpec((PEER_EB, D), lambda i, e: (e, 0)),
            pl.BlockSpec((D, PEER_EB), lambda i, e: (0, e)),
        ],
        out_specs=pl.BlockSpec((t, D), lambda i, e: (i, 0)),
        out_shape=jax.ShapeDtypeStruct((n, D), F32),
        scratch_shapes=[pltpu.VMEM(half, F32), pltpu.VMEM(half, F32), pltpu.VMEM(half, BF16),
                        pltpu.VMEM(half, BF16), pltpu.VMEM((PEER_EB, t), F32),
                        pltpu.VMEM((PEER_EB, t), BF16), pltpu.VMEM((D, t), F32)],
        compiler_params=_cparams(("parallel", "arbitrary")),
        name="peer_experts",
    )(hm, st, stats, u, vt)


def _rope_tables(rows_lat):
    pairs = HEAD_W // 4
    t = jnp.arange(rows_lat)
    inv = ROPE_THETA ** (-jnp.arange(pairs, dtype=F32) / pairs)
    ang = jnp.concatenate([(t // GRID_W).astype(F32)[:, None] * inv,
                           (t % GRID_W).astype(F32)[:, None] * inv], axis=-1)
    cos, sin = jnp.cos(ang), jnp.sin(ang)
    zero = jnp.zeros_like(sin)
    c = jnp.stack([cos, cos], -1).reshape(rows_lat, HEAD_W)
    sa = jnp.stack([-sin, zero], -1).reshape(rows_lat, HEAD_W)
    sb = jnp.stack([zero, sin], -1).reshape(rows_lat, HEAD_W)
    ident = jnp.ones((CTX, HEAD_W), F32)
    none = jnp.zeros((CTX, HEAD_W), F32)
    return (jnp.concatenate([ident, c]), jnp.concatenate([none, sa]), jnp.concatenate([none, sb]))


def _odd_column_map():
    src = -np.ones((OD_W,), np.int64)
    for h in range(4):
        src[OD_Q + h * HEAD_W:OD_Q + h * HEAD_W + GLA_DK] = np.arange(h * GLA_DK, (h + 1) * GLA_DK)
        src[OD_K + h * HEAD_W:OD_K + h * HEAD_W + GLA_DK] = 256 + np.arange(h * GLA_DK, (h + 1) * GLA_DK)
    src[OD_V:OD_V + 512] = 512 + np.arange(512)
    src[OD_G:OD_G + 512] = 1024 + np.arange(512)
    src[OD_Z:OD_Z + 512] = 1568 + np.arange(512)
    src[OD_XS:OD_XS + 512] = 2080 + np.arange(512)
    src[OD_BM:OD_BM + 256] = 2592 + np.arange(256)
    src[OD_CM:OD_CM + 256] = 2848 + np.arange(256)
    src[OD_MISC + MISC_LRF:OD_MISC + MISC_LRF + 16] = 1536 + np.arange(16)
    src[OD_MISC + MISC_LRB:OD_MISC + MISC_LRB + 16] = 1552 + np.arange(16)
    src[OD_MISC + MISC_DTF:OD_MISC + MISC_DTF + 8] = 3104 + np.arange(8)
    src[OD_MISC + MISC_DTB:OD_MISC + MISC_DTB + 8] = 3112 + np.arange(8)
    return src


def _relayout_odd_w(w):
    src = _odd_column_map()
    cols = jnp.take(w, jnp.asarray(np.maximum(src, 0)), axis=1)
    return jnp.where(jnp.asarray(src >= 0)[None, :], cols, 0.0)


def _pad_rows(rows_list, width):
    out = [jnp.broadcast_to(jnp.asarray(r, F32).reshape(1, width), (1, width)) for r in rows_list]
    out += [jnp.zeros((1, width), F32)] * (8 - len(out))
    return jnp.concatenate(out, axis=0)


def _hgrn_params(lb, gain):
    la = jnp.log(jnp.maximum(lb, LB_FLOOR))
    l1 = jnp.log1p(-lb)
    oml = 1.0 - lb
    return _pad_rows([la[0], la[1], l1[0], l1[1], oml[0], oml[1], jnp.tile(gain, 4)], 4 * HEAD_W)


def _gla_params(gate_w, gate_b, gain):
    wg = jnp.zeros((2, HEAD_W, 4 * HEAD_W), F32)
    gb = jnp.zeros((2, 4 * HEAD_W), F32)
    for h in range(4):
        s = slice(h * GLA_DK, (h + 1) * GLA_DK)
        dst = slice(h * HEAD_W, h * HEAD_W + GLA_DK)
        for d, off in enumerate((MISC_LRF, MISC_LRB)):
            wg = wg.at[d, off:off + 16, dst].set(gate_w[d][:, s])
            gb = gb.at[d, dst].set(gate_b[d, s])
    return wg.astype(BF16), _pad_rows([gb[0], gb[1], jnp.tile(gain, 4)], 4 * HEAD_W)


def _ssd_params(dt_bias, a_log, d_skip, norm_g):
    sel = np.zeros((2, 2, HEAD_W, 4 * HEAD_W), np.float32)
    for g in range(2):
        for d, off in enumerate((MISC_DTF, MISC_DTB)):
            for hh in range(4):
                sel[g, d, off + 4 * g + hh, hh * HEAD_W:(hh + 1) * HEAD_W] = 1.0
    par, par2 = [], []
    neg_a = -jnp.exp(a_log.astype(F32))
    for g in range(2):
        hs = slice(4 * g, 4 * g + 4)
        rep = lambda v: jnp.repeat(v[hs], HEAD_W)
        par.append(_pad_rows([rep(dt_bias[0]), rep(dt_bias[1]), rep(neg_a[0]), rep(neg_a[1])], 4 * HEAD_W))
        par2.append(_pad_rows([jnp.repeat(d_skip[hs], SSD_DH), norm_g[g * 256:(g + 1) * 256]], 4 * SSD_DH))
    return jnp.asarray(sel), jnp.stack(par), jnp.stack(par2)


def kernel(x, c, ctx, c_ctx, mod_w, mod_b, ln_g, ln_b, ev_w_in, ev_w_out, hg_lb_logits, hg_norm_g, at_q_norm_g, at_k_norm_g, od_w_in, od_w_out, gla_gate_w, gla_gate_b, gla_norm_g, ssd_conv_w, ssd_conv_b, ssd_dt_bias, ssd_a_log, ssd_d, ssd_norm_g, peer_wq, peer_subkeys, peer_u, peer_v):
    nb, seq, _ = x.shape
    assert ctx.shape[1] == CTX and seq % TM == 0 and nb + 1 <= MOD_ROWS
    rows = CTX + seq
    n = nb * rows

    cc = jnp.concatenate([c, c_ctx[None, :], jnp.zeros((MOD_ROWS - nb - 1, D), F32)], axis=0)
    mods = _modulation(cc, mod_w, mod_b)
    mods = mods.reshape(DEPTH, MOD_ROWS, N_MOD, D).transpose(0, 2, 1, 3).reshape(DEPTH, N_MOD * MOD_ROWS, 1, D)

    sm = jax.nn.softmax(hg_lb_logits.astype(F32), axis=1)
    hg_lb = jnp.cumsum(sm, axis=1) - sm[:, :1]
    tables = _rope_tables(seq)

    h = jnp.concatenate([ctx, x], axis=1).reshape(n, D)
    pre = None
    for l in range(DEPTH):
        j = l // 2
        modl = mods[l]
        if l % 2 == 0:
            y, h = _inproj(h, ev_w_in[j].astype(BF16), modl, nb, pre)
            y3 = y.reshape(nb, rows, EV_W)
            mix_a = _vscan("hgrn", y3, ((EV_Q, EV_I, EV_ZF), (EV_Q, EV_I, EV_ZB)), EV_G,
                           _hgrn_params(hg_lb[:, j], hg_norm_g[j]))
            mix_b = _attention(y3, tables, at_q_norm_g[j].reshape(1, HEAD_W), at_k_norm_g[j].reshape(1, HEAD_W))
            w_out = ev_w_out[j]
        else:
            y, h = _inproj(h, _relayout_odd_w(od_w_in[j]).astype(BF16), modl, nb, pre)
            y3 = y.reshape(nb, rows, OD_W)
            wg, gpar = _gla_params(gla_gate_w[j], gla_gate_b[j], gla_norm_g[j])
            mix_a = _vscan("gla", y3, ((OD_Q, OD_K, OD_V),) * 2, OD_G, gpar, wg)
            xbc = _ssd_conv(y3, ssd_conv_w[j], ssd_conv_b[j])
            sel, spar, spar2 = _ssd_params(ssd_dt_bias[j], ssd_a_log[j], ssd_d[j], ssd_norm_g[j])
            mix_b = _ssd_scan(xbc, y3, sel, spar, spar2)
            w_out = od_w_out[j]
        lg1, lb1 = ln_g[l, 0].reshape(1, D), ln_b[l, 0].reshape(1, D)
        sk = peer_subkeys[l].reshape(2 * PEER_HEADS, PEER_NKEYS, PEER_NKEYS).astype(BF16)
        h1, hm, st = _outproj(h, mix_a.reshape(n, D // 2), mix_b.reshape(n, D // 2), modl,
                              w_out.astype(BF16), lg1, lb1, peer_wq[l].astype(BF16), sk, nb)
        stats = _peer_stats(st)
        yp = _peer_experts(hm, st, stats, peer_u[l].astype(BF16), peer_v[l].astype(BF16).T)
        pre = (yp, modl, ln_g[l, 1].reshape(1, D), ln_b[l, 1].reshape(1, D))
        h = h1
    out = _final(h, pre[0], pre[1], pre[2], pre[3], nb)
    return out.reshape(nb, seq, D)
```

```python
import functools
import math

import numpy as np
import jax
import jax.numpy as jnp
from jax import lax
from jax.experimental import pallas as pl
from jax.experimental.pallas import tpu as pltpu

F32 = jnp.float32
BF16 = jnp.bfloat16
HIGHEST = lax.Precision.HIGHEST

D = 1024
DEPTH = 4
N_MOD = 6
CTX = 256
GRID_W = 64
DN_ALPHA = (2 * DEPTH) ** 0.25
NORM_EPS = 1e-6
ROPE_THETA = 10000.0
LB_FLOOR = 1e-30
TM = 256
MOD_ROWS = 24
VMEM_LIMIT = 56 * 1024 * 1024

HEAD_W = 128
VEC_CHUNK = 16
SSD_CHUNK = 128
SSD_DH = 64
GLA_DK = 64
GLA_GATE_NORM = 16.0
PEER_HEADS = 8
PEER_NKEYS = 128
PEER_TOPK = 16
PEER_T = 512
PEER_EB = 1024

EV_Q, EV_I, EV_ZF, EV_ZB, EV_G, EV_AQ, EV_AK, EV_AV = 0, 512, 1024, 1536, 2048, 2560, 3072, 3328
EV_W = 3584
OD_Q, OD_K, OD_V, OD_G, OD_Z, OD_XS, OD_BM, OD_CM, OD_MISC = 0, 512, 1024, 1536, 2048, 2560, 3072, 3328, 3584
OD_W = 3712
MISC_LRF, MISC_LRB, MISC_DTF, MISC_DTB = 0, 16, 32, 40

NT_DIMS = (((1,), (1,)), ((), ()))
TN_DIMS = (((0,), (0,)), ((), ()))


def _cparams(sem):
    return pltpu.CompilerParams(dimension_semantics=sem, vmem_limit_bytes=VMEM_LIMIT)


def _layer_norm(y, g, b):
    mu = jnp.mean(y, axis=-1, keepdims=True)
    yc = y - mu
    var = jnp.mean(yc * yc, axis=-1, keepdims=True)
    return yc * lax.rsqrt(var + NORM_EPS) * g + b


def _rms(y, g):
    return y * lax.rsqrt(jnp.mean(y * y, axis=-1, keepdims=True) + NORM_EPS) * g


def _log_sigmoid(z):
    return jnp.minimum(z, 0.0) - jnp.log(1.0 + jnp.exp(-jnp.abs(z)))


def _silu(z):
    return z * jax.nn.sigmoid(z)


def _split_hi_lo(a):
    hi = a.astype(BF16)
    lo = (a - hi.astype(F32)).astype(BF16)
    return hi, lo


def _mod_body(c_ref, w_ref, b_ref, o_ref):
    a = _silu(c_ref[...])
    a_hi, a_lo = _split_hi_lo(a)
    w_hi, w_lo = _split_hi_lo(w_ref[0])
    acc = jnp.dot(a_hi, w_hi, preferred_element_type=F32)
    acc += jnp.dot(a_lo, w_hi, preferred_element_type=F32)
    acc += jnp.dot(a_hi, w_lo, preferred_element_type=F32)
    o_ref[0] = acc + b_ref[0]


def _modulation(cc, mod_w, mod_b):
    tn = 1536
    nw = mod_w.shape[-1]
    return pl.pallas_call(
        _mod_body,
        grid=(DEPTH, nw // tn),
        in_specs=[
            pl.BlockSpec((MOD_ROWS, D), lambda l, j: (0, 0)),
            pl.BlockSpec((1, D, tn), lambda l, j: (l, 0, j)),
            pl.BlockSpec((1, 1, tn), lambda l, j: (l, 0, j)),
        ],
        out_specs=pl.BlockSpec((1, MOD_ROWS, tn), lambda l, j: (l, 0, j)),
        out_shape=jax.ShapeDtypeStruct((DEPTH, MOD_ROWS, nw), F32),
        compiler_params=_cparams(("parallel", "parallel")),
        name="modulation",
    )(cc, mod_w, mod_b.reshape(DEPTH, 1, nw))


def _mod_spec(k, n_batch, tiles_per_batch):
    def index(i):
        row = jnp.where(i % tiles_per_batch == 0, n_batch, i // tiles_per_batch)
        return (k * MOD_ROWS + row, 0, 0)
    return pl.BlockSpec((1, 1, D), index)


def _inproj_body(has_pre, *refs):
    if has_pre:
        x_ref, y_ref, g2_ref, lng_ref, lnb_ref, sh_ref, sc_ref, w_ref, o_ref, h_ref = refs
        h = _layer_norm(DN_ALPHA * x_ref[...] + g2_ref[0] * y_ref[...], lng_ref[...], lnb_ref[...])
        h_ref[...] = h
    else:
        x_ref, sh_ref, sc_ref, w_ref, o_ref = refs
        h = x_ref[...]
    u = h * (1.0 + sc_ref[0]) + sh_ref[0]
    o_ref[...] = jnp.dot(u.astype(BF16), w_ref[...], preferred_element_type=F32)


def _inproj(h, w, modl, n_batch, pre=None):
    n = h.shape[0]
    nw = w.shape[1]
    tpb = n // n_batch // TM
    tok = pl.BlockSpec((TM, D), lambda i: (i, 0))
    vec = pl.BlockSpec((1, D), lambda i: (0, 0))
    in_specs = [tok]
    args = [h]
    if pre is not None:
        y, prev_modl, ln_g, ln_b = pre
        in_specs += [tok, _mod_spec(5, n_batch, tpb), vec, vec]
        args += [y, prev_modl, ln_g, ln_b]
    in_specs += [_mod_spec(0, n_batch, tpb), _mod_spec(1, n_batch, tpb),
                 pl.BlockSpec((D, nw), lambda i: (0, 0))]
    args += [modl, modl, w]
    out_specs = [pl.BlockSpec((TM, nw), lambda i: (i, 0))]
    out_shape = [jax.ShapeDtypeStruct((n, nw), F32)]
    if pre is not None:
        out_specs.append(tok)
        out_shape.append(jax.ShapeDtypeStruct((n, D), F32))
    res = pl.pallas_call(
        functools.partial(_inproj_body, pre is not None),
        grid=(n // TM,),
        in_specs=in_specs,
        out_specs=out_specs,
        out_shape=out_shape,
        compiler_params=_cparams(("parallel",)),
        name="inproj",
    )(*args)
    return (res[0], res[1]) if pre is not None else (res[0], h)


def _final_body(x_ref, y_ref, g2_ref, lng_ref, lnb_ref, o_ref):
    o_ref[...] = _layer_norm(DN_ALPHA * x_ref[...] + g2_ref[0] * y_ref[...], lng_ref[...], lnb_ref[...])


def _final(h, y, modl, ln_g, ln_b, n_batch):
    n = h.shape[0]
    tpb = n // n_batch // TM
    lat = tpb - 1
    tok = pl.BlockSpec((TM, D), lambda b, j: (b * tpb + 1 + j, 0))
    vec = pl.BlockSpec((1, D), lambda b, j: (0, 0))
    return pl.pallas_call(
        _final_body,
        grid=(n_batch, lat),
        in_specs=[tok, tok, pl.BlockSpec((1, 1, D), lambda b, j: (5 * MOD_ROWS + b, 0, 0)), vec, vec],
        out_specs=pl.BlockSpec((TM, D), lambda b, j: (b * lat + j, 0)),
        out_shape=jax.ShapeDtypeStruct((n_batch * lat * TM, D), F32),
        compiler_params=_cparams(("parallel", "parallel")),
        name="final_norm",
    )(h, y, modl, ln_g, ln_b)


def _vscan_body(mode, nblk, *refs):
    C = VEC_CHUNK
    nh = 4
    if mode == "hgrn":
        qf_ref, vf_ref, zf_ref, qb_ref, vb_ref, zb_ref, g_ref, par_ref, o_ref, st_ref, ob_ref = refs
        dir_refs = ((qf_ref, vf_ref, zf_ref), (qb_ref, vb_ref, zb_ref))
    else:
        (qf_ref, kf_ref, vf_ref, mf_ref, qb_ref, kb_ref, vb_ref, mb_ref, g_ref, wg_ref, par_ref,
         o_ref, st_ref, ob_ref) = refs
        dir_refs = ((qf_ref, kf_ref, vf_ref, mf_ref), (qb_ref, kb_ref, vb_ref, mb_ref))
    rows = o_ref.shape[1]
    rb = qf_ref.shape[1]
    nch = rb // C
    i = pl.program_id(1)
    bi = jnp.where(i == 0, 0, nblk - i)

    @pl.when(i == 0)
    def _():
        st_ref[...] = jnp.zeros(st_ref.shape, F32)

    w = nh * HEAD_W
    ri = lax.broadcasted_iota(jnp.int32, (rb, rb), 0)
    ci = lax.broadcasted_iota(jnp.int32, (rb, rb), 1)
    rid = lax.broadcasted_iota(jnp.int32, (rb, w), 0)
    levels = [C << s for s in range((rb // C).bit_length() - 1)]
    neg = -1e30

    def group_row(x, grp, r):
        g = x.reshape(rb // grp, grp, w)
        return jnp.broadcast_to(g[:, r:r + 1, :], g.shape).reshape(rb, w)

    def nt(a, b):
        return lax.dot_general(a, b, NT_DIMS, preferred_element_type=F32)

    def same_group(grp):
        sh = grp.bit_length() - 1
        return (ri >> sh) == (ci >> sh)

    def direction(d, out_ref, row0):
        refs_d = dir_refs[d]
        q = refs_d[0][0]
        v16 = refs_d[-2 if mode == "gla" else 1][0].astype(BF16)
        if mode == "hgrn":
            z = refs_d[2][0]
            qh = _silu(q) * (HEAD_W ** -0.5)
            la, l1, oml = par_ref[d:d + 1, :], par_ref[2 + d:3 + d, :], par_ref[4 + d:5 + d, :]
            bb = l1 + _log_sigmoid(z)
            log_f = jnp.maximum(la, bb) + jnp.log(1.0 + jnp.exp(-jnp.abs(la - bb)))
            k = oml * jax.nn.sigmoid(-z)
        else:
            qh = q * (GLA_DK ** -0.5)
            k = refs_d[1][0]
            pre = jnp.dot(refs_d[3][0].astype(BF16), wg_ref[d], preferred_element_type=F32)
            log_f = _log_sigmoid(pre + par_ref[d:d + 1, :]) * (1.0 / GLA_GATE_NORM)
        tri = (ri >= ci if d == 0 else ri <= ci).astype(BF16)
        hi, lo = _split_hi_lo(log_f)
        cc = jnp.dot(tri, jnp.concatenate([hi, lo], axis=-1), preferred_element_type=F32)
        cum = cc[:, :w] + cc[:, w:]
        first_row, last_row = (0, rb - 1) if d == 0 else (rb - 1, 0)

        x0 = cum - group_row(cum, C, first_row % C)
        qs = [(qh * jnp.exp(x0)).astype(BF16)]
        ks = [(k * jnp.exp(-x0)).astype(BF16)]
        for m in levels:
            is_query = (rid & m) != 0 if d == 0 else (rid & m) == 0
            bnd = group_row(cum, 2 * m, m - 1 if d == 0 else m)
            qs.append((qh * jnp.exp(jnp.where(is_query, cum - bnd, neg))).astype(BF16))
            ks.append((k * jnp.exp(jnp.where(is_query, neg, bnd - cum))).astype(BF16))
        last = cum[last_row:last_row + 1, :]
        qe = (qh * jnp.exp(cum)).astype(BF16)
        kl = (k * jnp.exp(last - cum)).astype(BF16)
        dec = jnp.exp(last)

        outs = []
        for hh in range(nh):
            sl = slice(hh * HEAD_W, (hh + 1) * HEAD_W)
            p0 = nt(qs[0][:, sl], ks[0][:, sl])
            causal = ri >= ci if d == 0 else ri <= ci
            att = jnp.where(same_group(C), jnp.where(causal, p0, 0.0), 0.0)
            for lv, m in enumerate(levels):
                p = nt(qs[lv + 1][:, sl], ks[lv + 1][:, sl])
                att += p if 2 * m == rb else jnp.where(same_group(2 * m), p, 0.0)
            o = jnp.dot(att.astype(BF16), v16[:, sl], preferred_element_type=F32)
            s_t = st_ref[d * nh + hh]
            o += nt(qe[:, sl], s_t.astype(BF16))
            kv_t = lax.dot_general(v16[:, sl], kl[:, sl], TN_DIMS, preferred_element_type=F32)
            st_ref[d * nh + hh] = s_t * dec[:, sl] + kv_t
            outs.append(o)
        out_ref[row0, :] = jnp.concatenate(outs, axis=-1)

    direction(0, o_ref.at[0], pl.ds(pl.multiple_of(i * rb, rb), rb))
    direction(1, ob_ref, pl.ds(pl.multiple_of(bi * rb, rb), rb))

    @pl.when(i == nblk - 1)
    def _():
        def post(j, carry):
            rs = pl.ds(pl.multiple_of(j * rb, rb), rb)
            for hh in range(nh):
                lanes = pl.ds(hh * HEAD_W, HEAD_W)
                o = o_ref[0, rs, lanes] + ob_ref[rs, lanes]
                g = g_ref[0, rs, lanes]
                if mode == "hgrn":
                    res = _rms(o * jax.nn.sigmoid(g), par_ref[6:7, lanes])
                else:
                    res = _rms(o, par_ref[2:3, lanes]) * _silu(g)
                o_ref[0, rs, lanes] = res
            return carry

        lax.fori_loop(0, rows // rb, post, 0)


def _vscan(mode, y3, col_offsets, g_offset, params, wg=None):
    nb, rows, _ = y3.shape
    width = 4 * HEAD_W
    rb = CTX
    nblk = rows // rb
    fwd = lambda w, off: pl.BlockSpec((1, rb, w), lambda b, i: (b, i, off // w))
    bwd = lambda w, off: pl.BlockSpec((1, rb, w), lambda b, i: (b, jnp.where(i == 0, 0, nblk - i), off // w))
    in_specs, args = [], []
    for mk in (fwd, bwd):
        for off in col_offsets[mk is bwd]:
            in_specs.append(mk(width, off))
            args.append(y3)
        if mode == "gla":
            in_specs.append(mk(HEAD_W, OD_MISC))
            args.append(y3)
    in_specs.append(pl.BlockSpec((1, rows, width), lambda b, i: (b, 0, g_offset // width)))
    args.append(y3)
    if mode == "gla":
        in_specs.append(pl.BlockSpec((2, HEAD_W, width), lambda b, i: (0, 0, 0)))
        args.append(wg)
    in_specs.append(pl.BlockSpec((8, width), lambda b, i: (0, 0)))
    args.append(params)
    return pl.pallas_call(
        functools.partial(_vscan_body, mode, nblk),
        grid=(nb, nblk),
        in_specs=in_specs,
        out_specs=pl.BlockSpec((1, rows, width), lambda b, i: (b, 0, 0)),
        out_shape=jax.ShapeDtypeStruct((nb, rows, width), F32),
        scratch_shapes=[pltpu.VMEM((8, HEAD_W, HEAD_W), F32), pltpu.VMEM((rows, width), F32)],
        compiler_params=_cparams(("parallel", "arbitrary")),
        name="vscan_" + mode,
    )(*args)


def _rope(x, c, sa, sb):
    return x * c + pltpu.roll(x, HEAD_W - 1, 1) * sa + pltpu.roll(x, 1, 1) * sb


def _attn_body(q_ref, k_ref, v_ref, ck_ref, sak_ref, sbk_ref, cq_ref, saq_ref, sbq_ref,
               qg_ref, kg_ref, o_ref, kp_ref, vp_ref):
    qi = pl.program_id(2)
    rows = k_ref.shape[1]
    blk = 256

    @pl.when(qi == 0)
    def _():
        def prep(i, carry):
            rs = pl.ds(pl.multiple_of(i * blk, blk), blk)
            kn = _rms(k_ref[0, rs, :], kg_ref[...])
            kp_ref[rs, :] = _rope(kn, ck_ref[rs, :], sak_ref[rs, :], sbk_ref[rs, :]).astype(BF16)
            vp_ref[rs, :] = v_ref[0, rs, :].astype(BF16)
            return carry
        lax.fori_loop(0, rows // blk, prep, 0)

    def attend(nk):
        outs = []
        for r in range(2):
            qn = _rms(q_ref[0, :, r * HEAD_W:(r + 1) * HEAD_W], qg_ref[...])
            qr = (_rope(qn, cq_ref[...], saq_ref[...], sbq_ref[...]) * (HEAD_W ** -0.5)).astype(BF16)
            s = lax.dot_general(qr, kp_ref[0:nk, :], NT_DIMS, preferred_element_type=F32)
            m = jnp.max(s, axis=-1, keepdims=True)
            p = jnp.exp(s - m)
            l = jnp.sum(p, axis=-1, keepdims=True)
            o = jnp.dot(p.astype(BF16), vp_ref[0:nk, :], preferred_element_type=F32)
            outs.append(o / l)
        o_ref[0] = jnp.concatenate(outs, axis=-1)

    @pl.when(qi == 0)
    def _():
        attend(CTX)

    @pl.when(qi > 0)
    def _():
        attend(rows)


def _attention(y3, tables, q_gain, k_gain):
    nb, rows, _ = y3.shape
    c, sa, sb = tables
    full = pl.BlockSpec((rows, HEAD_W), lambda b, g, i: (0, 0))
    tile = pl.BlockSpec((TM, HEAD_W), lambda b, g, i: (i, 0))
    vec = pl.BlockSpec((1, HEAD_W), lambda b, g, i: (0, 0))
    return pl.pallas_call(
        _attn_body,
        grid=(nb, 2, rows // TM),
        in_specs=[
            pl.BlockSpec((1, TM, 2 * HEAD_W), lambda b, g, i: (b, i, EV_AQ // (2 * HEAD_W) + g)),
            pl.BlockSpec((1, rows, HEAD_W), lambda b, g, i: (b, 0, EV_AK // HEAD_W + g)),
            pl.BlockSpec((1, rows, HEAD_W), lambda b, g, i: (b, 0, EV_AV // HEAD_W + g)),
            full, full, full, tile, tile, tile, vec, vec,
        ],
        out_specs=pl.BlockSpec((1, TM, 2 * HEAD_W), lambda b, g, i: (b, i, g)),
        out_shape=jax.ShapeDtypeStruct((nb, rows, 4 * HEAD_W), F32),
        scratch_shapes=[pltpu.VMEM((rows, HEAD_W), BF16), pltpu.VMEM((rows, HEAD_W), BF16)],
        compiler_params=_cparams(("parallel", "parallel", "arbitrary")),
        name="attention",
    )(y3, y3, y3, c, sa, sb, c, sa, sb, q_gain, k_gain)


def _conv_body(x_ref, w_ref, b_ref, o_ref):
    rows = x_ref.shape[1]
    x = x_ref[0]
    row = lax.broadcasted_iota(jnp.int32, x.shape, 0)
    lo = jnp.where(row < CTX, 0, CTX)
    hi = jnp.where(row < CTX, CTX, rows)
    taps = w_ref.shape[0]
    acc = jnp.zeros(x.shape, F32) + b_ref[...]
    for j in range(taps):
        off = j - taps // 2
        xs = x if off == 0 else pltpu.roll(x, (-off) % rows, 0)
        src = row + off
        xs = jnp.where(src >= lo, jnp.where(src < hi, xs, 0.0), 0.0)
        acc += xs * w_ref[j:j + 1, :]
    o_ref[0] = _silu(acc)


def _ssd_conv(y3, conv_w, conv_b):
    nb, rows, _ = y3.shape
    nc = conv_w.shape[1]
    return pl.pallas_call(
        _conv_body,
        grid=(nb, nc // HEAD_W),
        in_specs=[
            pl.BlockSpec((1, rows, HEAD_W), lambda b, j: (b, 0, OD_XS // HEAD_W + j)),
            pl.BlockSpec((conv_w.shape[0], HEAD_W), lambda b, j: (0, j)),
            pl.BlockSpec((1, HEAD_W), lambda b, j: (0, j)),
        ],
        out_specs=pl.BlockSpec((1, rows, HEAD_W), lambda b, j: (b, 0, j)),
        out_shape=jax.ShapeDtypeStruct((nb, rows, nc), F32),
        compiler_params=_cparams(("parallel", "parallel")),
        name="ssd_conv",
    )(y3, conv_w, conv_b.reshape(1, nc))


def _ssd_body(n_ctx_chunks, n_chunks, x_ref, bm_ref, cm_ref, z_ref, misc_ref, sel_ref, par_ref,
              par2_ref, o_ref, st_ref, ob_ref):
    C = SSD_CHUNK
    P = SSD_DH
    rows = o_ref.shape[1]
    w = 4 * HEAD_W
    st_ref[...] = jnp.zeros(st_ref.shape, F32)
    ri = lax.broadcasted_iota(jnp.int32, (C, C), 0)
    ci = lax.broadcasted_iota(jnp.int32, (C, C), 1)
    masks = (ri >= ci, ri <= ci)
    tri = tuple(mk.astype(BF16) for mk in masks)

    def narrow(a):
        return jnp.concatenate([a[:, hh * HEAD_W:hh * HEAD_W + P] for hh in range(4)], axis=-1)

    def chunk(d, r0, out_ref):
        rs = pl.ds(r0, C)
        x = x_ref[0, rs, :]
        bmat = bm_ref[0, rs, :]
        cb16 = cm_ref[0, rs, :].astype(BF16)
        b16 = bmat.astype(BF16)
        bt16 = bmat.T.astype(BF16)
        scores = lax.dot_general(cb16, b16, NT_DIMS, preferred_element_type=F32)
        mh, ml = _split_hi_lo(misc_ref[0, rs, :])
        dd = jnp.dot(jnp.concatenate([mh, ml], axis=0), sel_ref[0, d], preferred_element_type=F32)
        dt = dd[:C] + dd[C:] + par_ref[0, d:d + 1, :]
        dt = jnp.maximum(dt, 0.0) + jnp.log(1.0 + jnp.exp(-jnp.abs(dt)))
        la = dt * par_ref[0, 2 + d:3 + d, :]
        lh, ll = _split_hi_lo(la)
        cc = jnp.dot(tri[d], jnp.concatenate([lh, ll], axis=-1), preferred_element_type=F32)
        cum = cc[:, :w] + cc[:, w:]
        last = cum[C - 1:C, :] if d == 0 else cum[0:1, :]
        xdt = x * narrow(dt)
        s_t = st_ref[d]
        y_in = jnp.dot(cb16, s_t.astype(BF16), preferred_element_type=F32) * narrow(jnp.exp(cum))
        xw = (xdt * narrow(jnp.exp(last - cum))).astype(BF16)
        st_ref[d] = s_t * narrow(jnp.exp(last)) + jnp.dot(bt16, xw, preferred_element_type=F32)
        xdt16 = xdt.astype(BF16)
        ys = []
        for hh in range(4):
            cum_h = cum[:, hh * HEAD_W:(hh + 1) * HEAD_W]
            seg = jnp.where(masks[d], cum_h - cum_h.T, 0.0)
            dec = jnp.where(masks[d], jnp.exp(seg), 0.0)
            ys.append(jnp.dot((scores * dec).astype(BF16), xdt16[:, hh * P:(hh + 1) * P],
                              preferred_element_type=F32))
        out_ref[rs, :] = jnp.concatenate(ys, axis=-1) + y_in

    def step(t, carry):
        cf = t
        cb = jnp.where(t < n_ctx_chunks, n_ctx_chunks - 1 - t, n_chunks + n_ctx_chunks - 1 - t)
        chunk(0, pl.multiple_of(cf * C, C), o_ref.at[0])
        chunk(1, pl.multiple_of(cb * C, C), ob_ref)
        return carry

    lax.fori_loop(0, n_chunks, step, 0)

    blk = 256

    def post(i, carry):
        rs = pl.ds(pl.multiple_of(i * blk, blk), blk)
        y = o_ref[0, rs, :] + ob_ref[rs, :] + par2_ref[0, 0:1, :] * x_ref[0, rs, :]
        o_ref[0, rs, :] = _rms(y * _silu(z_ref[0, rs, :]), par2_ref[0, 1:2, :])
        return carry

    lax.fori_loop(0, rows // blk, post, 0)


def _ssd_scan(xbc, y3, sel, par, par2):
    nb, rows, _ = xbc.shape
    gw = 4 * SSD_DH
    return pl.pallas_call(
        functools.partial(_ssd_body, CTX // SSD_CHUNK, rows // SSD_CHUNK),
        grid=(nb, 2),
        in_specs=[
            pl.BlockSpec((1, rows, gw), lambda b, g: (b, 0, g)),
            pl.BlockSpec((1, rows, HEAD_W), lambda b, g: (b, 0, 2 * gw // HEAD_W + g)),
            pl.BlockSpec((1, rows, HEAD_W), lambda b, g: (b, 0, 2 * gw // HEAD_W + 2 + g)),
            pl.BlockSpec((1, rows, gw), lambda b, g: (b, 0, OD_Z // gw + g)),
            pl.BlockSpec((1, rows, HEAD_W), lambda b, g: (b, 0, OD_MISC // HEAD_W)),
            pl.BlockSpec((1, 2, HEAD_W, 4 * HEAD_W), lambda b, g: (g, 0, 0, 0)),
            pl.BlockSpec((1, 8, 4 * HEAD_W), lambda b, g: (g, 0, 0)),
            pl.BlockSpec((1, 8, gw), lambda b, g: (g, 0, 0)),
        ],
        out_specs=pl.BlockSpec((1, rows, gw), lambda b, g: (b, 0, g)),
        out_shape=jax.ShapeDtypeStruct((nb, rows, 2 * gw), F32),
        scratch_shapes=[pltpu.VMEM((2, HEAD_W, gw), F32), pltpu.VMEM((rows, gw), F32)],
        compiler_params=_cparams(("parallel", "parallel")),
        name="ssd_scan",
    )(xbc, xbc, xbc, y3, y3, sel, par, par2)


def _outproj_body(h_ref, a_ref, b_ref, g1_ref, sh2_ref, sc2_ref, wo_ref, lng_ref, lnb_ref, wq_ref,
                  sk_ref, h1_ref, hm_ref, st_ref):
    mix = jnp.concatenate([a_ref[...], b_ref[...]], axis=-1).astype(BF16)
    o = jnp.dot(mix, wo_ref[...], preferred_element_type=F32)
    h1 = _layer_norm(DN_ALPHA * h_ref[...] + g1_ref[0] * o, lng_ref[...], lnb_ref[...])
    h1_ref[...] = h1
    hm = (h1 * (1.0 + sc2_ref[0]) + sh2_ref[0]).astype(BF16)
    hm_ref[...] = hm
    qry = jnp.dot(hm, wq_ref[...], preferred_element_type=F32).astype(BF16)
    for j in range(2 * PEER_HEADS):
        qj = qry[:, j * PEER_NKEYS:(j + 1) * PEER_NKEYS]
        st_ref[j * PEER_NKEYS:(j + 1) * PEER_NKEYS, :] = lax.dot_general(
            sk_ref[j], qj, NT_DIMS, preferred_element_type=F32)


def _outproj(h, mix_a, mix_b, modl, w_out, ln_g, ln_b, wq, sk, n_batch):
    n = h.shape[0]
    tpb = n // n_batch // TM
    tok = pl.BlockSpec((TM, D), lambda i: (i, 0))
    half = pl.BlockSpec((TM, D // 2), lambda i: (i, 0))
    vec = pl.BlockSpec((1, D), lambda i: (0, 0))
    nq = wq.shape[1]
    return pl.pallas_call(
        _outproj_body,
        grid=(n // TM,),
        in_specs=[tok, half, half, _mod_spec(2, n_batch, tpb), _mod_spec(3, n_batch, tpb),
                  _mod_spec(4, n_batch, tpb), pl.BlockSpec((D, D), lambda i: (0, 0)), vec, vec,
                  pl.BlockSpec((D, nq), lambda i: (0, 0)),
                  pl.BlockSpec(sk.shape, lambda i: (0, 0, 0))],
        out_specs=[tok, tok, pl.BlockSpec((nq, TM), lambda i: (0, i))],
        out_shape=[jax.ShapeDtypeStruct((n, D), F32), jax.ShapeDtypeStruct((n, D), BF16),
                   jax.ShapeDtypeStruct((nq, n), F32)],
        compiler_params=_cparams(("parallel",)),
        name="outproj",
    )(h, mix_a, mix_b, modl, modl, modl, w_out, ln_g, ln_b, wq, sk)


def _sort16_network():
    comps = []

    def merge(lo, hi, r):
        step = r * 2
        if step < hi - lo:
            merge(lo, hi, step)
            merge(lo + r, hi, step)
            comps.extend((j, j + r) for j in range(lo + r, hi - r, step))
        else:
            comps.append((lo, lo + r))

    def sort(lo, hi):
        if hi - lo >= 1:
            mid = lo + (hi - lo) // 2
            sort(lo, mid)
            sort(mid + 1, hi)
            merge(lo, hi, 1)

    sort(0, 15)
    return comps


SORT16 = _sort16_network()
BITONIC16 = [(j, j + s) for s in (8, 4, 2, 1) for j in range(16) if not j & s]


def _compare_exchange(v, comps):
    v = list(v)
    for a, b in comps:
        v[a], v[b] = jnp.maximum(v[a], v[b]), jnp.minimum(v[a], v[b])
    return v


def _merge_top16(a, b_rev):
    c = [a[k] if b_rev[k] is None else jnp.maximum(a[k], b_rev[k]) for k in range(16)]
    return _compare_exchange(c, BITONIC16)


def _top16_keys(ref, row0, col):
    v = [ref[row0 + 8 * j:row0 + 8 * j + 8, col] for j in range(16)]
    v = _compare_exchange(v, SORT16)
    for shift in (4, 2, 1):
        rolled = [pltpu.roll(x, shift, 0) for x in v]
        v = _merge_top16(v, rolled[::-1])
    return v


def _stats_body(st_ref, o_ref):
    k = PEER_TOPK
    nk = PEER_NKEYS
    sub = lax.broadcasted_iota(jnp.int32, (8, 128), 0)

    def column(tc, carry):
        col = pl.ds(pl.multiple_of(tc * 128, 128), 128)
        tops = []
        for side in range(2):
            packed = None
            for h in range(PEER_HEADS):
                a = _top16_keys(st_ref, (2 * h + side) * nk, col)
                packed = a if h == 0 else [jnp.where(sub == h, a[r], packed[r]) for r in range(k)]
            tops.append(packed)
        t1, t2 = tops
        best = [t1[0] + t2[r] for r in range(k)]
        for r1 in range(1, k):
            ln = k // (r1 + 1)
            lst = [t1[r1] + t2[r] for r in range(ln)]
            best = _merge_top16(best, [None] * (k - ln) + lst[::-1])
        z = 1.0
        for r in range(1, k):
            z = z + jnp.exp(best[r] - best[0])
        o_ref[0, :, col] = t1[0]
        o_ref[1, :, col] = t2[0] + jnp.log(z)
        o_ref[2, :, col] = best[k - 1]
        for r in range(k):
            o_ref[3 + r, :, col] = t2[r]
        return carry

    lax.fori_loop(0, o_ref.shape[2] // 128, column, 0)


def _peer_stats(st):
    n = st.shape[1]
    ts = 512
    return pl.pallas_call(
        _stats_body,
        grid=(n // ts,),
        in_specs=[pl.BlockSpec((st.shape[0], ts), lambda i: (0, i))],
        out_specs=pl.BlockSpec((3 + PEER_TOPK, PEER_HEADS, ts), lambda i: (0, 0, i)),
        out_shape=jax.ShapeDtypeStruct((3 + PEER_TOPK, PEER_HEADS, n), F32),
        compiler_params=_cparams(("parallel",)),
        name="peer_stats",
    )(st)


def _expert_body(hm_ref, st_ref, stats_ref, u_ref, vt_ref, y_ref, e1_ref, n1_ref, e2_ref, r2_ref,
                 a_ref, p_ref, acc_ref, hmt_ref):
    e = pl.program_id(1)
    ne = pl.num_programs(1)
    t = hm_ref.shape[0]
    nk = PEER_NKEYS
    k = PEER_TOPK
    m = PEER_EB // nk
    ntc = t // 128

    @pl.when(e == 0)
    def _():
        def prep(tc, carry):
            col = pl.ds(pl.multiple_of(tc * 128, 128), 128)
            for h in range(PEER_HEADS):
                rows = slice(h * nk, (h + 1) * nk)
                s1 = st_ref[2 * h * nk:(2 * h + 1) * nk, col]
                s2 = st_ref[(2 * h + 1) * nk:(2 * h + 2) * nk, col]
                tau = stats_ref[2, h:h + 1, col]
                n1 = jnp.zeros((nk, 128), F32)
                r2 = jnp.zeros((nk, 128), F32)
                for r in range(k):
                    t2r = stats_ref[3 + r, h:h + 1, col]
                    n1 += jnp.where(s1 + t2r >= tau, 1.0, 0.0)
                    r2 += jnp.where(t2r > s2, 1.0, 0.0)
                e1_ref[rows, col] = jnp.exp(s1 - stats_ref[0, h:h + 1, col])
                n1_ref[rows, col] = n1
                e2_ref[rows, col] = jnp.exp(s2 - stats_ref[1, h:h + 1, col]).astype(BF16)
                r2_ref[rows, col] = r2.astype(BF16)
            return carry

        lax.fori_loop(0, ntc, prep, 0)
        acc_ref[...] = jnp.zeros(acc_ref.shape, F32)
        hmt_ref[...] = hm_ref[...].astype(F32).T.astype(BF16)

    a_ref[...] = jnp.dot(u_ref[...], hmt_ref[...], preferred_element_type=F32)

    def build(tc, carry):
        col = pl.ds(pl.multiple_of(tc * 128, 128), 128)
        own = [pl.ds(pl.multiple_of(h * nk + e * m, m), m) for h in range(PEER_HEADS)]
        e1 = [e1_ref[own[h], col].astype(BF16) for h in range(PEER_HEADS)]
        n1 = [n1_ref[own[h], col].astype(BF16) for h in range(PEER_HEADS)]
        zero = jnp.zeros((), BF16)
        group = 2
        for g0 in range(0, m, group):
            gates = [jnp.zeros((nk, 128), BF16) for _ in range(group)]
            for h in range(PEER_HEADS):
                rows2 = slice(h * nk, (h + 1) * nk)
                r2 = r2_ref[rows2, col]
                e2 = e2_ref[rows2, col]
                for j in range(group):
                    n1b = jnp.broadcast_to(n1[h][g0 + j:g0 + j + 1, :], (nk, 128))
                    e1b = jnp.broadcast_to(e1[h][g0 + j:g0 + j + 1, :], (nk, 128))
                    gates[j] += jnp.where(r2 < n1b, e1b * e2, zero)
            for j in range(group):
                rows = slice((g0 + j) * nk, (g0 + j + 1) * nk)
                act = jax.nn.gelu(a_ref[rows, col].astype(BF16), approximate=True)
                p_ref[rows, col] = act * gates[j]
        return carry

    lax.fori_loop(0, ntc, build, 0)
    acc_ref[...] += jnp.dot(vt_ref[...], p_ref[...], preferred_element_type=F32)

    @pl.when(e == ne - 1)
    def _():
        y_ref[...] = acc_ref[...].T


def _peer_experts(hm, st, stats, u, vt):
    n = hm.shape[0]
    ne = u.shape[0] // PEER_EB
    assert PEER_EB // PEER_NKEYS == 8
    t = PEER_T
    half = (PEER_HEADS * PEER_NKEYS, t)
    return pl.pallas_call(
        _expert_body,
        grid=(n // t, ne),
        in_specs=[
            pl.BlockSpec((t, D), lambda i, e: (i, 0)),
            pl.BlockSpec((st.shape[0], t), lambda i, e: (0, i)),
            pl.BlockSpec((stats.shape[0], PEER_HEADS, t), lambda i, e: (0, 0, i)),
            pl.BlockSpec((PEER_EB, D), lambda i, e: (e, 0)),
            pl.BlockSpec((D, PEER_EB), lambda i, e: (0, e)),
        ],
        out_specs=pl.BlockSpec((t, D), lambda i, e: (i, 0)),
        out_shape=jax.ShapeDtypeStruct((n, D), F32),
        scratch_shapes=[pltpu.VMEM(half, F32), pltpu.VMEM(half, F32), pltpu.VMEM(half, BF16),
                        pltpu.VMEM(half, BF16), pltpu.VMEM((PEER_EB, t), F32),
                        pltpu.VMEM((PEER_EB, t), BF16), pltpu.VMEM((D, t), F32),
                        pltpu.VMEM((D, t), BF16)],
        compiler_params=_cparams(("parallel", "arbitrary")),
        name="peer_experts",
    )(hm, st, stats, u, vt)


def _rope_tables(rows_lat):
    pairs = HEAD_W // 4
    t = jnp.arange(rows_lat)
    inv = ROPE_THETA ** (-jnp.arange(pairs, dtype=F32) / pairs)
    ang = jnp.concatenate([(t // GRID_W).astype(F32)[:, None] * inv,
                           (t % GRID_W).astype(F32)[:, None] * inv], axis=-1)
    cos, sin = jnp.cos(ang), jnp.sin(ang)
    zero = jnp.zeros_like(sin)
    c = jnp.stack([cos, cos], -1).reshape(rows_lat, HEAD_W)
    sa = jnp.stack([-sin, zero], -1).reshape(rows_lat, HEAD_W)
    sb = jnp.stack([zero, sin], -1).reshape(rows_lat, HEAD_W)
    ident = jnp.ones((CTX, HEAD_W), F32)
    none = jnp.zeros((CTX, HEAD_W), F32)
    return (jnp.concatenate([ident, c]), jnp.concatenate([none, sa]), jnp.concatenate([none, sb]))


def _odd_column_map():
    src = -np.ones((OD_W,), np.int64)
    for h in range(4):
        src[OD_Q + h * HEAD_W:OD_Q + h * HEAD_W + GLA_DK] = np.arange(h * GLA_DK, (h + 1) * GLA_DK)
        src[OD_K + h * HEAD_W:OD_K + h * HEAD_W + GLA_DK] = 256 + np.arange(h * GLA_DK, (h + 1) * GLA_DK)
    src[OD_V:OD_V + 512] = 512 + np.arange(512)
    src[OD_G:OD_G + 512] = 1024 + np.arange(512)
    src[OD_Z:OD_Z + 512] = 1568 + np.arange(512)
    src[OD_XS:OD_XS + 512] = 2080 + np.arange(512)
    src[OD_BM:OD_BM + 256] = 2592 + np.arange(256)
    src[OD_CM:OD_CM + 256] = 2848 + np.arange(256)
    src[OD_MISC + MISC_LRF:OD_MISC + MISC_LRF + 16] = 1536 + np.arange(16)
    src[OD_MISC + MISC_LRB:OD_MISC + MISC_LRB + 16] = 1552 + np.arange(16)
    src[OD_MISC + MISC_DTF:OD_MISC + MISC_DTF + 8] = 3104 + np.arange(8)
    src[OD_MISC + MISC_DTB:OD_MISC + MISC_DTB + 8] = 3112 + np.arange(8)
    return src


def _relayout_odd_w(w):
    src = _odd_column_map()
    cols = jnp.take(w, jnp.asarray(np.maximum(src, 0)), axis=1)
    return jnp.where(jnp.asarray(src >= 0)[None, :], cols, 0.0)


def _pad_rows(rows_list, width):
    out = [jnp.broadcast_to(jnp.asarray(r, F32).reshape(1, width), (1, width)) for r in rows_list]
    out += [jnp.zeros((1, width), F32)] * (8 - len(out))
    return jnp.concatenate(out, axis=0)


def _hgrn_params(lb, gain):
    la = jnp.log(jnp.maximum(lb, LB_FLOOR))
    l1 = jnp.log1p(-lb)
    oml = 1.0 - lb
    return _pad_rows([la[0], la[1], l1[0], l1[1], oml[0], oml[1], jnp.tile(gain, 4)], 4 * HEAD_W)


def _gla_params(gate_w, gate_b, gain):
    wg = jnp.zeros((2, HEAD_W, 4 * HEAD_W), F32)
    gb = jnp.zeros((2, 4 * HEAD_W), F32)
    for h in range(4):
        s = slice(h * GLA_DK, (h + 1) * GLA_DK)
        dst = slice(h * HEAD_W, h * HEAD_W + GLA_DK)
        for d, off in enumerate((MISC_LRF, MISC_LRB)):
            wg = wg.at[d, off:off + 16, dst].set(gate_w[d][:, s])
            gb = gb.at[d, dst].set(gate_b[d, s])
    return wg.astype(BF16), _pad_rows([gb[0], gb[1], jnp.tile(gain, 4)], 4 * HEAD_W)


def _ssd_params(dt_bias, a_log, d_skip, norm_g):
    sel = np.zeros((2, 2, HEAD_W, 4 * HEAD_W), np.float32)
    for g in range(2):
        for d, off in enumerate((MISC_DTF, MISC_DTB)):
            for hh in range(4):
                sel[g, d, off + 4 * g + hh, hh * HEAD_W:(hh + 1) * HEAD_W] = 1.0
    par, par2 = [], []
    neg_a = -jnp.exp(a_log.astype(F32))
    for g in range(2):
        hs = slice(4 * g, 4 * g + 4)
        rep = lambda v: jnp.repeat(v[hs], HEAD_W)
        par.append(_pad_rows([rep(dt_bias[0]), rep(dt_bias[1]), rep(neg_a[0]), rep(neg_a[1])], 4 * HEAD_W))
        par2.append(_pad_rows([jnp.repeat(d_skip[hs], SSD_DH), norm_g[g * 256:(g + 1) * 256]], 4 * SSD_DH))
    return jnp.asarray(sel, BF16), jnp.stack(par), jnp.stack(par2)


def kernel(x, c, ctx, c_ctx, mod_w, mod_b, ln_g, ln_b, ev_w_in, ev_w_out, hg_lb_logits, hg_norm_g, at_q_norm_g, at_k_norm_g, od_w_in, od_w_out, gla_gate_w, gla_gate_b, gla_norm_g, ssd_conv_w, ssd_conv_b, ssd_dt_bias, ssd_a_log, ssd_d, ssd_norm_g, peer_wq, peer_subkeys, peer_u, peer_v):
    nb, seq, _ = x.shape
    assert ctx.shape[1] == CTX and seq % TM == 0 and nb + 1 <= MOD_ROWS
    rows = CTX + seq
    n = nb * rows

    cc = jnp.concatenate([c, c_ctx[None, :], jnp.zeros((MOD_ROWS - nb - 1, D), F32)], axis=0)
    mods = _modulation(cc, mod_w, mod_b)
    mods = mods.reshape(DEPTH, MOD_ROWS, N_MOD, D).transpose(0, 2, 1, 3).reshape(DEPTH, N_MOD * MOD_ROWS, 1, D)

    sm = jax.nn.softmax(hg_lb_logits.astype(F32), axis=1)
    hg_lb = jnp.cumsum(sm, axis=1) - sm[:, :1]
    tables = _rope_tables(seq)

    h = jnp.concatenate([ctx, x], axis=1).reshape(n, D)
    pre = None
    for l in range(DEPTH):
        j = l // 2
        modl = mods[l]
        if l % 2 == 0:
            y, h = _inproj(h, ev_w_in[j].astype(BF16), modl, nb, pre)
            y3 = y.reshape(nb, rows, EV_W)
            mix_a = _vscan("hgrn", y3, ((EV_Q, EV_I, EV_ZF), (EV_Q, EV_I, EV_ZB)), EV_G,
                           _hgrn_params(hg_lb[:, j], hg_norm_g[j]))
            mix_b = _attention(y3, tables, at_q_norm_g[j].reshape(1, HEAD_W), at_k_norm_g[j].reshape(1, HEAD_W))
            w_out = ev_w_out[j]
        else:
            y, h = _inproj(h, _relayout_odd_w(od_w_in[j]).astype(BF16), modl, nb, pre)
            y3 = y.reshape(nb, rows, OD_W)
            wg, gpar = _gla_params(gla_gate_w[j], gla_gate_b[j], gla_norm_g[j])
            mix_a = _vscan("gla", y3, ((OD_Q, OD_K, OD_V),) * 2, OD_G, gpar, wg)
            xbc = _ssd_conv(y3, ssd_conv_w[j], ssd_conv_b[j])
            sel, spar, spar2 = _ssd_params(ssd_dt_bias[j], ssd_a_log[j], ssd_d[j], ssd_norm_g[j])
            mix_b = _ssd_scan(xbc, y3, sel, spar, spar2)
            w_out = od_w_out[j]
        lg1, lb1 = ln_g[l, 0].reshape(1, D), ln_b[l, 0].reshape(1, D)
        sk = peer_subkeys[l].reshape(2 * PEER_HEADS, PEER_NKEYS, PEER_NKEYS).astype(BF16)
        h1, hm, st = _outproj(h, mix_a.reshape(n, D // 2), mix_b.reshape(n, D // 2), modl,
                              w_out.astype(BF16), lg1, lb1, peer_wq[l].astype(BF16), sk, nb)
        stats = _peer_stats(st)
        yp = _peer_experts(hm, st, stats, peer_u[l].astype(BF16), peer_v[l].astype(BF16).T)
        pre = (yp, modl, ln_g[l, 1].reshape(1, D), ln_b[l, 1].reshape(1, D))
        h = h1
    out = _final(h, pre[0], pre[1], pre[2], pre[3], nb)
    return out.reshape(nb, seq, D)
```

```python
import functools
import math

import numpy as np
import jax
import jax.numpy as jnp
from jax import lax
from jax.experimental import pallas as pl
from jax.experimental.pallas import tpu as pltpu

F32 = jnp.float32
BF16 = jnp.bfloat16
HIGHEST = lax.Precision.HIGHEST

D = 1024
DEPTH = 4
N_MOD = 6
CTX = 256
GRID_W = 64
DN_ALPHA = (2 * DEPTH) ** 0.25
NORM_EPS = 1e-6
ROPE_THETA = 10000.0
LB_FLOOR = 1e-30
TM = 256
MOD_ROWS = 24
VMEM_LIMIT = 56 * 1024 * 1024

HEAD_W = 128
VEC_CHUNK = 16
SSD_CHUNK = 128
SSD_DH = 64
GLA_DK = 64
GLA_GATE_NORM = 16.0
PEER_HEADS = 8
PEER_NKEYS = 128
PEER_TOPK = 16
PEER_T = 512
PEER_EB = 2048

EV_Q, EV_I, EV_ZF, EV_ZB, EV_G, EV_AQ, EV_AK, EV_AV = 0, 512, 1024, 1536, 2048, 2560, 3072, 3328
EV_W = 3584
OD_Q, OD_K, OD_V, OD_G, OD_Z, OD_XS, OD_BM, OD_CM, OD_MISC = 0, 512, 1024, 1536, 2048, 2560, 3072, 3328, 3584
OD_W = 3712
MISC_LRF, MISC_LRB, MISC_DTF, MISC_DTB = 0, 16, 32, 40

NT_DIMS = (((1,), (1,)), ((), ()))
TN_DIMS = (((0,), (0,)), ((), ()))


def _cparams(sem):
    return pltpu.CompilerParams(dimension_semantics=sem, vmem_limit_bytes=VMEM_LIMIT)


def _layer_norm(y, g, b):
    mu = jnp.mean(y, axis=-1, keepdims=True)
    yc = y - mu
    var = jnp.mean(yc * yc, axis=-1, keepdims=True)
    return yc * lax.rsqrt(var + NORM_EPS) * g + b


def _rms(y, g):
    return y * lax.rsqrt(jnp.mean(y * y, axis=-1, keepdims=True) + NORM_EPS) * g


def _log_sigmoid(z):
    return jnp.minimum(z, 0.0) - jnp.log(1.0 + jnp.exp(-jnp.abs(z)))


def _silu(z):
    return z * jax.nn.sigmoid(z)


def _split_hi_lo(a):
    hi = a.astype(BF16)
    lo = (a - hi.astype(F32)).astype(BF16)
    return hi, lo


def _mod_body(c_ref, w_ref, b_ref, o_ref):
    a = _silu(c_ref[...])
    a_hi, a_lo = _split_hi_lo(a)
    w_hi, w_lo = _split_hi_lo(w_ref[0])
    acc = jnp.dot(a_hi, w_hi, preferred_element_type=F32)
    acc += jnp.dot(a_lo, w_hi, preferred_element_type=F32)
    acc += jnp.dot(a_hi, w_lo, preferred_element_type=F32)
    o_ref[0] = acc + b_ref[0]


def _modulation(cc, mod_w, mod_b):
    tn = 1536
    nw = mod_w.shape[-1]
    return pl.pallas_call(
        _mod_body,
        grid=(DEPTH, nw // tn),
        in_specs=[
            pl.BlockSpec((MOD_ROWS, D), lambda l, j: (0, 0)),
            pl.BlockSpec((1, D, tn), lambda l, j: (l, 0, j)),
            pl.BlockSpec((1, 1, tn), lambda l, j: (l, 0, j)),
        ],
        out_specs=pl.BlockSpec((1, MOD_ROWS, tn), lambda l, j: (l, 0, j)),
        out_shape=jax.ShapeDtypeStruct((DEPTH, MOD_ROWS, nw), F32),
        compiler_params=_cparams(("parallel", "parallel")),
        name="modulation",
    )(cc, mod_w, mod_b.reshape(DEPTH, 1, nw))


def _mod_spec(k, n_batch, tiles_per_batch):
    def index(i):
        row = jnp.where(i % tiles_per_batch == 0, n_batch, i // tiles_per_batch)
        return (k * MOD_ROWS + row, 0, 0)
    return pl.BlockSpec((1, 1, D), index)


def _inproj_body(has_pre, *refs):
    if has_pre:
        x_ref, y_ref, g2_ref, lng_ref, lnb_ref, sh_ref, sc_ref, w_ref, o_ref, h_ref = refs
        h = _layer_norm(DN_ALPHA * x_ref[...] + g2_ref[0] * y_ref[...], lng_ref[...], lnb_ref[...])
        h_ref[...] = h
    else:
        x_ref, sh_ref, sc_ref, w_ref, o_ref = refs
        h = x_ref[...]
    u = h * (1.0 + sc_ref[0]) + sh_ref[0]
    o_ref[...] = jnp.dot(u.astype(BF16), w_ref[...], preferred_element_type=F32)


def _inproj(h, w, modl, n_batch, pre=None):
    n = h.shape[0]
    nw = w.shape[1]
    tpb = n // n_batch // TM
    tok = pl.BlockSpec((TM, D), lambda i: (i, 0))
    vec = pl.BlockSpec((1, D), lambda i: (0, 0))
    in_specs = [tok]
    args = [h]
    if pre is not None:
        y, prev_modl, ln_g, ln_b = pre
        in_specs += [tok, _mod_spec(5, n_batch, tpb), vec, vec]
        args += [y, prev_modl, ln_g, ln_b]
    in_specs += [_mod_spec(0, n_batch, tpb), _mod_spec(1, n_batch, tpb),
                 pl.BlockSpec((D, nw), lambda i: (0, 0))]
    args += [modl, modl, w]
    out_specs = [pl.BlockSpec((TM, nw), lambda i: (i, 0))]
    out_shape = [jax.ShapeDtypeStruct((n, nw), F32)]
    if pre is not None:
        out_specs.append(tok)
        out_shape.append(jax.ShapeDtypeStruct((n, D), F32))
    res = pl.pallas_call(
        functools.partial(_inproj_body, pre is not None),
        grid=(n // TM,),
        in_specs=in_specs,
        out_specs=out_specs,
        out_shape=out_shape,
        compiler_params=_cparams(("parallel",)),
        name="inproj",
    )(*args)
    return (res[0], res[1]) if pre is not None else (res[0], h)


def _final_body(x_ref, y_ref, g2_ref, lng_ref, lnb_ref, o_ref):
    o_ref[...] = _layer_norm(DN_ALPHA * x_ref[...] + g2_ref[0] * y_ref[...], lng_ref[...], lnb_ref[...])


def _final(h, y, modl, ln_g, ln_b, n_batch):
    n = h.shape[0]
    tpb = n // n_batch // TM
    lat = tpb - 1
    tok = pl.BlockSpec((TM, D), lambda b, j: (b * tpb + 1 + j, 0))
    vec = pl.BlockSpec((1, D), lambda b, j: (0, 0))
    return pl.pallas_call(
        _final_body,
        grid=(n_batch, lat),
        in_specs=[tok, tok, pl.BlockSpec((1, 1, D), lambda b, j: (5 * MOD_ROWS + b, 0, 0)), vec, vec],
        out_specs=pl.BlockSpec((TM, D), lambda b, j: (b * lat + j, 0)),
        out_shape=jax.ShapeDtypeStruct((n_batch * lat * TM, D), F32),
        compiler_params=_cparams(("parallel", "parallel")),
        name="final_norm",
    )(h, y, modl, ln_g, ln_b)


def _vscan_body(mode, nblk, *refs):
    C = VEC_CHUNK
    nh = 4
    if mode == "hgrn":
        qf_ref, vf_ref, zf_ref, qb_ref, vb_ref, zb_ref, g_ref, par_ref, o_ref, st_ref, ob_ref = refs
        dir_refs = ((qf_ref, vf_ref, zf_ref), (qb_ref, vb_ref, zb_ref))
    else:
        (qf_ref, kf_ref, vf_ref, mf_ref, qb_ref, kb_ref, vb_ref, mb_ref, g_ref, wg_ref, par_ref,
         o_ref, st_ref, ob_ref) = refs
        dir_refs = ((qf_ref, kf_ref, vf_ref, mf_ref), (qb_ref, kb_ref, vb_ref, mb_ref))
    rows = o_ref.shape[1]
    rb = qf_ref.shape[1]
    nch = rb // C
    i = pl.program_id(1)
    bi = jnp.where(i == 0, 0, nblk - i)

    @pl.when(i == 0)
    def _():
        st_ref[...] = jnp.zeros(st_ref.shape, F32)

    w = nh * HEAD_W
    ri = lax.broadcasted_iota(jnp.int32, (rb, rb), 0)
    ci = lax.broadcasted_iota(jnp.int32, (rb, rb), 1)
    rid = lax.broadcasted_iota(jnp.int32, (rb, w), 0)
    levels = [C << s for s in range((rb // C).bit_length() - 1)]
    neg = -1e30

    def group_row(x, grp, r):
        g = x.reshape(rb // grp, grp, w)
        return jnp.broadcast_to(g[:, r:r + 1, :], g.shape).reshape(rb, w)

    def nt(a, b):
        return lax.dot_general(a, b, NT_DIMS, preferred_element_type=F32)

    def same_group(grp):
        sh = grp.bit_length() - 1
        return (ri >> sh) == (ci >> sh)

    def direction(d, out_ref, row0):
        refs_d = dir_refs[d]
        q = refs_d[0][0]
        v16 = refs_d[-2 if mode == "gla" else 1][0].astype(BF16)
        if mode == "hgrn":
            z = refs_d[2][0]
            qh = _silu(q) * (HEAD_W ** -0.5)
            la, l1, oml = par_ref[d:d + 1, :], par_ref[2 + d:3 + d, :], par_ref[4 + d:5 + d, :]
            bb = l1 + _log_sigmoid(z)
            log_f = jnp.maximum(la, bb) + jnp.log(1.0 + jnp.exp(-jnp.abs(la - bb)))
            k = oml * jax.nn.sigmoid(-z)
        else:
            qh = q * (GLA_DK ** -0.5)
            k = refs_d[1][0]
            pre = jnp.dot(refs_d[3][0].astype(BF16), wg_ref[d], preferred_element_type=F32)
            log_f = _log_sigmoid(pre + par_ref[d:d + 1, :]) * (1.0 / GLA_GATE_NORM)
        tri = (ri >= ci if d == 0 else ri <= ci).astype(BF16)
        hi, lo = _split_hi_lo(log_f)
        cc = jnp.dot(tri, jnp.concatenate([hi, lo], axis=-1), preferred_element_type=F32)
        cum = cc[:, :w] + cc[:, w:]
        first_row, last_row = (0, rb - 1) if d == 0 else (rb - 1, 0)

        x0 = cum - group_row(cum, C, first_row % C)
        qs = [(qh * jnp.exp(x0)).astype(BF16)]
        ks = [(k * jnp.exp(-x0)).astype(BF16)]
        for m in levels:
            is_query = (rid & m) != 0 if d == 0 else (rid & m) == 0
            bnd = group_row(cum, 2 * m, m - 1 if d == 0 else m)
            qs.append((qh * jnp.exp(jnp.where(is_query, cum - bnd, neg))).astype(BF16))
            ks.append((k * jnp.exp(jnp.where(is_query, neg, bnd - cum))).astype(BF16))
        last = cum[last_row:last_row + 1, :]
        qe = (qh * jnp.exp(cum)).astype(BF16)
        kl = (k * jnp.exp(last - cum)).astype(BF16)
        dec = jnp.exp(last)

        outs = []
        for hh in range(nh):
            sl = slice(hh * HEAD_W, (hh + 1) * HEAD_W)
            p0 = nt(qs[0][:, sl], ks[0][:, sl])
            causal = ri >= ci if d == 0 else ri <= ci
            att = jnp.where(same_group(C), jnp.where(causal, p0, 0.0), 0.0)
            for lv, m in enumerate(levels):
                p = nt(qs[lv + 1][:, sl], ks[lv + 1][:, sl])
                att += p if 2 * m == rb else jnp.where(same_group(2 * m), p, 0.0)
            o = jnp.dot(att.astype(BF16), v16[:, sl], preferred_element_type=F32)
            s_t = st_ref[d * nh + hh]
            o += nt(qe[:, sl], s_t.astype(BF16))
            kv_t = lax.dot_general(v16[:, sl], kl[:, sl], TN_DIMS, preferred_element_type=F32)
            st_ref[d * nh + hh] = s_t * dec[:, sl] + kv_t
            outs.append(o)
        out_ref[row0, :] = jnp.concatenate(outs, axis=-1)

    direction(0, o_ref.at[0], pl.ds(pl.multiple_of(i * rb, rb), rb))
    direction(1, ob_ref, pl.ds(pl.multiple_of(bi * rb, rb), rb))

    @pl.when(i == nblk - 1)
    def _():
        def post(j, carry):
            rs = pl.ds(pl.multiple_of(j * rb, rb), rb)
            for hh in range(nh):
                lanes = pl.ds(hh * HEAD_W, HEAD_W)
                o = o_ref[0, rs, lanes] + ob_ref[rs, lanes]
                g = g_ref[0, rs, lanes]
                if mode == "hgrn":
                    res = _rms(o * jax.nn.sigmoid(g), par_ref[6:7, lanes])
                else:
                    res = _rms(o, par_ref[2:3, lanes]) * _silu(g)
                o_ref[0, rs, lanes] = res
            return carry

        lax.fori_loop(0, rows // rb, post, 0)


def _vscan(mode, y3, col_offsets, g_offset, params, wg=None):
    nb, rows, _ = y3.shape
    width = 4 * HEAD_W
    rb = CTX
    nblk = rows // rb
    fwd = lambda w, off: pl.BlockSpec((1, rb, w), lambda b, i: (b, i, off // w))
    bwd = lambda w, off: pl.BlockSpec((1, rb, w), lambda b, i: (b, jnp.where(i == 0, 0, nblk - i), off // w))
    in_specs, args = [], []
    for mk in (fwd, bwd):
        for off in col_offsets[mk is bwd]:
            in_specs.append(mk(width, off))
            args.append(y3)
        if mode == "gla":
            in_specs.append(mk(HEAD_W, OD_MISC))
            args.append(y3)
    in_specs.append(pl.BlockSpec((1, rows, width), lambda b, i: (b, 0, g_offset // width)))
    args.append(y3)
    if mode == "gla":
        in_specs.append(pl.BlockSpec((2, HEAD_W, width), lambda b, i: (0, 0, 0)))
        args.append(wg)
    in_specs.append(pl.BlockSpec((8, width), lambda b, i: (0, 0)))
    args.append(params)
    return pl.pallas_call(
        functools.partial(_vscan_body, mode, nblk),
        grid=(nb, nblk),
        in_specs=in_specs,
        out_specs=pl.BlockSpec((1, rows, width), lambda b, i: (b, 0, 0)),
        out_shape=jax.ShapeDtypeStruct((nb, rows, width), F32),
        scratch_shapes=[pltpu.VMEM((8, HEAD_W, HEAD_W), F32), pltpu.VMEM((rows, width), F32)],
        compiler_params=_cparams(("parallel", "arbitrary")),
        name="vscan_" + mode,
    )(*args)


def _rope(x, c, sa, sb):
    return x * c + pltpu.roll(x, HEAD_W - 1, 1) * sa + pltpu.roll(x, 1, 1) * sb


def _attn_body(q_ref, k_ref, v_ref, ck_ref, sak_ref, sbk_ref, cq_ref, saq_ref, sbq_ref,
               qg_ref, kg_ref, o_ref, kp_ref, vp_ref):
    qi = pl.program_id(2)
    rows = k_ref.shape[1]
    blk = 256

    @pl.when(qi == 0)
    def _():
        def prep(i, carry):
            rs = pl.ds(pl.multiple_of(i * blk, blk), blk)
            kn = _rms(k_ref[0, rs, :], kg_ref[...])
            kp_ref[rs, :] = _rope(kn, ck_ref[rs, :], sak_ref[rs, :], sbk_ref[rs, :]).astype(BF16)
            vp_ref[rs, :] = v_ref[0, rs, :].astype(BF16)
            return carry
        lax.fori_loop(0, rows // blk, prep, 0)

    def attend(nk):
        outs = []
        for r in range(2):
            qn = _rms(q_ref[0, :, r * HEAD_W:(r + 1) * HEAD_W], qg_ref[...])
            qr = (_rope(qn, cq_ref[...], saq_ref[...], sbq_ref[...]) * (HEAD_W ** -0.5)).astype(BF16)
            s = lax.dot_general(qr, kp_ref[0:nk, :], NT_DIMS, preferred_element_type=F32)
            m = jnp.max(s, axis=-1, keepdims=True)
            p = jnp.exp(s - m)
            l = jnp.sum(p, axis=-1, keepdims=True)
            o = jnp.dot(p.astype(BF16), vp_ref[0:nk, :], preferred_element_type=F32)
            outs.append(o / l)
        o_ref[0] = jnp.concatenate(outs, axis=-1)

    @pl.when(qi == 0)
    def _():
        attend(CTX)

    @pl.when(qi > 0)
    def _():
        attend(rows)


def _attention(y3, tables, q_gain, k_gain):
    nb, rows, _ = y3.shape
    c, sa, sb = tables
    full = pl.BlockSpec((rows, HEAD_W), lambda b, g, i: (0, 0))
    tile = pl.BlockSpec((TM, HEAD_W), lambda b, g, i: (i, 0))
    vec = pl.BlockSpec((1, HEAD_W), lambda b, g, i: (0, 0))
    return pl.pallas_call(
        _attn_body,
        grid=(nb, 2, rows // TM),
        in_specs=[
            pl.BlockSpec((1, TM, 2 * HEAD_W), lambda b, g, i: (b, i, EV_AQ // (2 * HEAD_W) + g)),
            pl.BlockSpec((1, rows, HEAD_W), lambda b, g, i: (b, 0, EV_AK // HEAD_W + g)),
            pl.BlockSpec((1, rows, HEAD_W), lambda b, g, i: (b, 0, EV_AV // HEAD_W + g)),
            full, full, full, tile, tile, tile, vec, vec,
        ],
        out_specs=pl.BlockSpec((1, TM, 2 * HEAD_W), lambda b, g, i: (b, i, g)),
        out_shape=jax.ShapeDtypeStruct((nb, rows, 4 * HEAD_W), F32),
        scratch_shapes=[pltpu.VMEM((rows, HEAD_W), BF16), pltpu.VMEM((rows, HEAD_W), BF16)],
        compiler_params=_cparams(("parallel", "parallel", "arbitrary")),
        name="attention",
    )(y3, y3, y3, c, sa, sb, c, sa, sb, q_gain, k_gain)


def _conv_body(x_ref, w_ref, b_ref, o_ref):
    rows = x_ref.shape[1]
    x = x_ref[0]
    row = lax.broadcasted_iota(jnp.int32, x.shape, 0)
    lo = jnp.where(row < CTX, 0, CTX)
    hi = jnp.where(row < CTX, CTX, rows)
    taps = w_ref.shape[0]
    acc = jnp.zeros(x.shape, F32) + b_ref[...]
    for j in range(taps):
        off = j - taps // 2
        xs = x if off == 0 else pltpu.roll(x, (-off) % rows, 0)
        src = row + off
        xs = jnp.where(src >= lo, jnp.where(src < hi, xs, 0.0), 0.0)
        acc += xs * w_ref[j:j + 1, :]
    o_ref[0] = _silu(acc)


def _ssd_conv(y3, conv_w, conv_b):
    nb, rows, _ = y3.shape
    nc = conv_w.shape[1]
    return pl.pallas_call(
        _conv_body,
        grid=(nb, nc // HEAD_W),
        in_specs=[
            pl.BlockSpec((1, rows, HEAD_W), lambda b, j: (b, 0, OD_XS // HEAD_W + j)),
            pl.BlockSpec((conv_w.shape[0], HEAD_W), lambda b, j: (0, j)),
            pl.BlockSpec((1, HEAD_W), lambda b, j: (0, j)),
        ],
        out_specs=pl.BlockSpec((1, rows, HEAD_W), lambda b, j: (b, 0, j)),
        out_shape=jax.ShapeDtypeStruct((nb, rows, nc), F32),
        compiler_params=_cparams(("parallel", "parallel")),
        name="ssd_conv",
    )(y3, conv_w, conv_b.reshape(1, nc))


def _ssd_body(n_ctx_chunks, n_chunks, x_ref, bm_ref, cm_ref, z_ref, misc_ref, sel_ref, par_ref,
              par2_ref, o_ref, st_ref, ob_ref):
    C = SSD_CHUNK
    P = SSD_DH
    rows = o_ref.shape[1]
    w = 4 * HEAD_W
    st_ref[...] = jnp.zeros(st_ref.shape, F32)
    ri = lax.broadcasted_iota(jnp.int32, (C, C), 0)
    ci = lax.broadcasted_iota(jnp.int32, (C, C), 1)
    masks = (ri >= ci, ri <= ci)
    tri = tuple(mk.astype(BF16) for mk in masks)

    def narrow(a):
        return jnp.concatenate([a[:, hh * HEAD_W:hh * HEAD_W + P] for hh in range(4)], axis=-1)

    def chunk(d, r0, out_ref):
        rs = pl.ds(r0, C)
        x = x_ref[0, rs, :]
        bmat = bm_ref[0, rs, :]
        cb16 = cm_ref[0, rs, :].astype(BF16)
        b16 = bmat.astype(BF16)
        bt16 = bmat.T.astype(BF16)
        scores = lax.dot_general(cb16, b16, NT_DIMS, preferred_element_type=F32)
        mh, ml = _split_hi_lo(misc_ref[0, rs, :])
        dd = jnp.dot(jnp.concatenate([mh, ml], axis=0), sel_ref[0, d], preferred_element_type=F32)
        dt = dd[:C] + dd[C:] + par_ref[0, d:d + 1, :]
        dt = jnp.maximum(dt, 0.0) + jnp.log(1.0 + jnp.exp(-jnp.abs(dt)))
        la = dt * par_ref[0, 2 + d:3 + d, :]
        lh, ll = _split_hi_lo(la)
        cc = jnp.dot(tri[d], jnp.concatenate([lh, ll], axis=-1), preferred_element_type=F32)
        cum = cc[:, :w] + cc[:, w:]
        last = cum[C - 1:C, :] if d == 0 else cum[0:1, :]
        xdt = x * narrow(dt)
        s_t = st_ref[d]
        y_in = jnp.dot(cb16, s_t.astype(BF16), preferred_element_type=F32) * narrow(jnp.exp(cum))
        xw = (xdt * narrow(jnp.exp(last - cum))).astype(BF16)
        st_ref[d] = s_t * narrow(jnp.exp(last)) + jnp.dot(bt16, xw, preferred_element_type=F32)
        xdt16 = xdt.astype(BF16)
        ys = []
        for hh in range(4):
            cum_h = cum[:, hh * HEAD_W:(hh + 1) * HEAD_W]
            seg = jnp.where(masks[d], cum_h - cum_h.T, 0.0)
            dec = jnp.where(masks[d], jnp.exp(seg), 0.0)
            ys.append(jnp.dot((scores * dec).astype(BF16), xdt16[:, hh * P:(hh + 1) * P],
                              preferred_element_type=F32))
        out_ref[rs, :] = jnp.concatenate(ys, axis=-1) + y_in

    def step(t, carry):
        cf = t
        cb = jnp.where(t < n_ctx_chunks, n_ctx_chunks - 1 - t, n_chunks + n_ctx_chunks - 1 - t)
        chunk(0, pl.multiple_of(cf * C, C), o_ref.at[0])
        chunk(1, pl.multiple_of(cb * C, C), ob_ref)
        return carry

    lax.fori_loop(0, n_chunks, step, 0)

    blk = 256

    def post(i, carry):
        rs = pl.ds(pl.multiple_of(i * blk, blk), blk)
        y = o_ref[0, rs, :] + ob_ref[rs, :] + par2_ref[0, 0:1, :] * x_ref[0, rs, :]
        o_ref[0, rs, :] = _rms(y * _silu(z_ref[0, rs, :]), par2_ref[0, 1:2, :])
        return carry

    lax.fori_loop(0, rows // blk, post, 0)


def _ssd_scan(xbc, y3, sel, par, par2):
    nb, rows, _ = xbc.shape
    gw = 4 * SSD_DH
    return pl.pallas_call(
        functools.partial(_ssd_body, CTX // SSD_CHUNK, rows // SSD_CHUNK),
        grid=(nb, 2),
        in_specs=[
            pl.BlockSpec((1, rows, gw), lambda b, g: (b, 0, g)),
            pl.BlockSpec((1, rows, HEAD_W), lambda b, g: (b, 0, 2 * gw // HEAD_W + g)),
            pl.BlockSpec((1, rows, HEAD_W), lambda b, g: (b, 0, 2 * gw // HEAD_W + 2 + g)),
            pl.BlockSpec((1, rows, gw), lambda b, g: (b, 0, OD_Z // gw + g)),
            pl.BlockSpec((1, rows, HEAD_W), lambda b, g: (b, 0, OD_MISC // HEAD_W)),
            pl.BlockSpec((1, 2, HEAD_W, 4 * HEAD_W), lambda b, g: (g, 0, 0, 0)),
            pl.BlockSpec((1, 8, 4 * HEAD_W), lambda b, g: (g, 0, 0)),
            pl.BlockSpec((1, 8, gw), lambda b, g: (g, 0, 0)),
        ],
        out_specs=pl.BlockSpec((1, rows, gw), lambda b, g: (b, 0, g)),
        out_shape=jax.ShapeDtypeStruct((nb, rows, 2 * gw), F32),
        scratch_shapes=[pltpu.VMEM((2, HEAD_W, gw), F32), pltpu.VMEM((rows, gw), F32)],
        compiler_params=_cparams(("parallel", "parallel")),
        name="ssd_scan",
    )(xbc, xbc, xbc, y3, y3, sel, par, par2)


def _outproj_body(h_ref, a_ref, b_ref, g1_ref, sh2_ref, sc2_ref, wo_ref, lng_ref, lnb_ref, wq_ref,
                  sk_ref, h1_ref, hm_ref, st_ref):
    mix = jnp.concatenate([a_ref[...], b_ref[...]], axis=-1).astype(BF16)
    o = jnp.dot(mix, wo_ref[...], preferred_element_type=F32)
    h1 = _layer_norm(DN_ALPHA * h_ref[...] + g1_ref[0] * o, lng_ref[...], lnb_ref[...])
    h1_ref[...] = h1
    hm = (h1 * (1.0 + sc2_ref[0]) + sh2_ref[0]).astype(BF16)
    hm_ref[...] = hm
    qry = jnp.dot(hm, wq_ref[...], preferred_element_type=F32).astype(BF16)
    for j in range(2 * PEER_HEADS):
        qj = qry[:, j * PEER_NKEYS:(j + 1) * PEER_NKEYS]
        st_ref[j * PEER_NKEYS:(j + 1) * PEER_NKEYS, :] = lax.dot_general(
            sk_ref[j], qj, NT_DIMS, preferred_element_type=F32)


def _outproj(h, mix_a, mix_b, modl, w_out, ln_g, ln_b, wq, sk, n_batch):
    n = h.shape[0]
    tpb = n // n_batch // TM
    tok = pl.BlockSpec((TM, D), lambda i: (i, 0))
    half = pl.BlockSpec((TM, D // 2), lambda i: (i, 0))
    vec = pl.BlockSpec((1, D), lambda i: (0, 0))
    nq = wq.shape[1]
    return pl.pallas_call(
        _outproj_body,
        grid=(n // TM,),
        in_specs=[tok, half, half, _mod_spec(2, n_batch, tpb), _mod_spec(3, n_batch, tpb),
                  _mod_spec(4, n_batch, tpb), pl.BlockSpec((D, D), lambda i: (0, 0)), vec, vec,
                  pl.BlockSpec((D, nq), lambda i: (0, 0)),
                  pl.BlockSpec(sk.shape, lambda i: (0, 0, 0))],
        out_specs=[tok, tok, pl.BlockSpec((nq, TM), lambda i: (0, i))],
        out_shape=[jax.ShapeDtypeStruct((n, D), F32), jax.ShapeDtypeStruct((n, D), BF16),
                   jax.ShapeDtypeStruct((nq, n), F32)],
        compiler_params=_cparams(("parallel",)),
        name="outproj",
    )(h, mix_a, mix_b, modl, modl, modl, w_out, ln_g, ln_b, wq, sk)


def _sort16_network():
    comps = []

    def merge(lo, hi, r):
        step = r * 2
        if step < hi - lo:
            merge(lo, hi, step)
            merge(lo + r, hi, step)
            comps.extend((j, j + r) for j in range(lo + r, hi - r, step))
        else:
            comps.append((lo, lo + r))

    def sort(lo, hi):
        if hi - lo >= 1:
            mid = lo + (hi - lo) // 2
            sort(lo, mid)
            sort(mid + 1, hi)
            merge(lo, hi, 1)

    sort(0, 15)
    return comps


SORT16 = _sort16_network()
BITONIC16 = [(j, j + s) for s in (8, 4, 2, 1) for j in range(16) if not j & s]


def _compare_exchange(v, comps):
    v = list(v)
    for a, b in comps:
        v[a], v[b] = jnp.maximum(v[a], v[b]), jnp.minimum(v[a], v[b])
    return v


def _merge_top16(a, b_rev):
    c = [a[k] if b_rev[k] is None else jnp.maximum(a[k], b_rev[k]) for k in range(16)]
    return _compare_exchange(c, BITONIC16)


def _top16_keys(ref, row0, col):
    v = [ref[row0 + 8 * j:row0 + 8 * j + 8, col] for j in range(16)]
    v = _compare_exchange(v, SORT16)
    for shift in (4, 2, 1):
        rolled = [pltpu.roll(x, shift, 0) for x in v]
        v = _merge_top16(v, rolled[::-1])
    return v


def _stats_body(st_ref, o_ref):
    k = PEER_TOPK
    nk = PEER_NKEYS
    sub = lax.broadcasted_iota(jnp.int32, (8, 128), 0)

    def column(tc, carry):
        col = pl.ds(pl.multiple_of(tc * 128, 128), 128)
        tops = []
        for side in range(2):
            packed = None
            for h in range(PEER_HEADS):
                a = _top16_keys(st_ref, (2 * h + side) * nk, col)
                packed = a if h == 0 else [jnp.where(sub == h, a[r], packed[r]) for r in range(k)]
            tops.append(packed)
        t1, t2 = tops
        best = [t1[0] + t2[r] for r in range(k)]
        for r1 in range(1, k):
            ln = k // (r1 + 1)
            lst = [t1[r1] + t2[r] for r in range(ln)]
            best = _merge_top16(best, [None] * (k - ln) + lst[::-1])
        z = 1.0
        for r in range(1, k):
            z = z + jnp.exp(best[r] - best[0])
        o_ref[0, :, col] = t1[0]
        o_ref[1, :, col] = t2[0] + jnp.log(z)
        o_ref[2, :, col] = best[k - 1]
        for r in range(k):
            o_ref[3 + r, :, col] = t2[r]
        return carry

    lax.fori_loop(0, o_ref.shape[2] // 128, column, 0)


def _peer_stats(st):
    n = st.shape[1]
    ts = 512
    return pl.pallas_call(
        _stats_body,
        grid=(n // ts,),
        in_specs=[pl.BlockSpec((st.shape[0], ts), lambda i: (0, i))],
        out_specs=pl.BlockSpec((3 + PEER_TOPK, PEER_HEADS, ts), lambda i: (0, 0, i)),
        out_shape=jax.ShapeDtypeStruct((3 + PEER_TOPK, PEER_HEADS, n), F32),
        compiler_params=_cparams(("parallel",)),
        name="peer_stats",
    )(st)


def _count_prefix(pred, rows):
    pick = lambda a, b, c: jnp.where(c, b, a)
    c8 = pred(rows[7])
    c4 = pred(pick(rows[3], rows[11], c8))
    c2 = pred(pick(pick(rows[1], rows[5], c4), pick(rows[9], rows[13], c4), c8))
    ev = [rows[2 * j] for j in range(8)]
    c1 = pred(pick(pick(pick(ev[0], ev[1], c2), pick(ev[2], ev[3], c2), c4),
                   pick(pick(ev[4], ev[5], c2), pick(ev[6], ev[7], c2), c4), c8))
    c16 = pred(rows[15])
    val = lambda c, v: jnp.where(c, v, 0.0)
    return val(c8, 8.0) + val(c4, 4.0) + val(c2, 2.0) + val(c1, 1.0) + val(c16, 1.0)


def _expert_body(hm_ref, st_ref, stats_ref, u_ref, vt_ref, y_ref, e1_ref, n1_ref, e2_ref, r2_ref,
                 a_ref, p_ref, acc_ref, hmt_ref):
    e = pl.program_id(1)
    ne = pl.num_programs(1)
    t = hm_ref.shape[0]
    nk = PEER_NKEYS
    k = PEER_TOPK
    m = PEER_EB // nk
    ntc = t // 128

    @pl.when(e == 0)
    def _():
        def prep(tc, carry):
            col = pl.ds(pl.multiple_of(tc * 128, 128), 128)
            for h in range(PEER_HEADS):
                rows = slice(h * nk, (h + 1) * nk)
                s1 = st_ref[2 * h * nk:(2 * h + 1) * nk, col]
                s2 = st_ref[(2 * h + 1) * nk:(2 * h + 2) * nk, col]
                tau = stats_ref[2, h:h + 1, col]
                t2 = [stats_ref[3 + r, h:h + 1, col] for r in range(k)]
                n1 = _count_prefix(lambda thr: s1 + thr >= tau, t2)
                r2 = _count_prefix(lambda thr: thr > s2, t2)
                e1_ref[rows, col] = jnp.exp(s1 - stats_ref[0, h:h + 1, col])
                n1_ref[rows, col] = n1
                e2_ref[rows, col] = jnp.exp(s2 - stats_ref[1, h:h + 1, col]).astype(BF16)
                r2_ref[rows, col] = r2.astype(BF16)
            return carry

        lax.fori_loop(0, ntc, prep, 0)
        acc_ref[...] = jnp.zeros(acc_ref.shape, F32)
        hmt_ref[...] = hm_ref[...].astype(F32).T.astype(BF16)

    a_ref[...] = jnp.dot(u_ref[...], hmt_ref[...], preferred_element_type=F32)

    def build(tc, carry):
        col = pl.ds(pl.multiple_of(tc * 128, 128), 128)
        own = [pl.ds(pl.multiple_of(h * nk + e * m, m), m) for h in range(PEER_HEADS)]
        e1 = [e1_ref[own[h], col].astype(BF16) for h in range(PEER_HEADS)]
        n1 = [n1_ref[own[h], col].astype(BF16) for h in range(PEER_HEADS)]
        zero = jnp.zeros((), BF16)
        group = 4
        for g0 in range(0, m, group):
            gates = [jnp.zeros((nk, 128), BF16) for _ in range(group)]
            for h in range(PEER_HEADS):
                rows2 = slice(h * nk, (h + 1) * nk)
                r2 = r2_ref[rows2, col]
                e2 = e2_ref[rows2, col]
                for j in range(group):
                    n1b = jnp.broadcast_to(n1[h][g0 + j:g0 + j + 1, :], (nk, 128))
                    e1b = jnp.broadcast_to(e1[h][g0 + j:g0 + j + 1, :], (nk, 128))
                    gates[j] = gates[j] + e1b * jnp.where(r2 < n1b, e2, zero)
            for j in range(group):
                rows = slice((g0 + j) * nk, (g0 + j + 1) * nk)
                act = jax.nn.gelu(a_ref[rows, col], approximate=True)
                p_ref[rows, col] = act.astype(BF16) * gates[j]
        return carry

    lax.fori_loop(0, ntc, build, 0)
    acc_ref[...] += jnp.dot(vt_ref[...], p_ref[...], preferred_element_type=F32)

    @pl.when(e == ne - 1)
    def _():
        y_ref[...] = acc_ref[...].T


def _peer_experts(hm, st, stats, u, vt):
    n = hm.shape[0]
    ne = u.shape[0] // PEER_EB
    assert (PEER_EB // PEER_NKEYS) % 8 == 0
    t = PEER_T
    half = (PEER_HEADS * PEER_NKEYS, t)
    return pl.pallas_call(
        _expert_body,
        grid=(n // t, ne),
        in_specs=[
            pl.BlockSpec((t, D), lambda i, e: (i, 0)),
            pl.BlockSpec((st.shape[0], t), lambda i, e: (0, i)),
            pl.BlockSpec((stats.shape[0], PEER_HEADS, t), lambda i, e: (0, 0, i)),
            pl.BlockSpec((PEER_EB, D), lambda i, e: (e, 0)),
            pl.BlockSpec((D, PEER_EB), lambda i, e: (0, e)),
        ],
        out_specs=pl.BlockSpec((t, D), lambda i, e: (i, 0)),
        out_shape=jax.ShapeDtypeStruct((n, D), F32),
        scratch_shapes=[pltpu.VMEM(half, F32), pltpu.VMEM(half, F32), pltpu.VMEM(half, BF16),
                        pltpu.VMEM(half, BF16), pltpu.VMEM((PEER_EB, t), F32),
                        pltpu.VMEM((PEER_EB, t), BF16), pltpu.VMEM((D, t), F32),
                        pltpu.VMEM((D, t), BF16)],
        compiler_params=_cparams(("parallel", "arbitrary")),
        name="peer_experts",
    )(hm, st, stats, u, vt)


def _rope_tables(rows_lat):
    pairs = HEAD_W // 4
    t = jnp.arange(rows_lat)
    inv = ROPE_THETA ** (-jnp.arange(pairs, dtype=F32) / pairs)
    ang = jnp.concatenate([(t // GRID_W).astype(F32)[:, None] * inv,
                           (t % GRID_W).astype(F32)[:, None] * inv], axis=-1)
    cos, sin = jnp.cos(ang), jnp.sin(ang)
    zero = jnp.zeros_like(sin)
    c = jnp.stack([cos, cos], -1).reshape(rows_lat, HEAD_W)
    sa = jnp.stack([-sin, zero], -1).reshape(rows_lat, HEAD_W)
    sb = jnp.stack([zero, sin], -1).reshape(rows_lat, HEAD_W)
    ident = jnp.ones((CTX, HEAD_W), F32)
    none = jnp.zeros((CTX, HEAD_W), F32)
    return (jnp.concatenate([ident, c]), jnp.concatenate([none, sa]), jnp.concatenate([none, sb]))


def _odd_column_map():
    src = -np.ones((OD_W,), np.int64)
    for h in range(4):
        src[OD_Q + h * HEAD_W:OD_Q + h * HEAD_W + GLA_DK] = np.arange(h * GLA_DK, (h + 1) * GLA_DK)
        src[OD_K + h * HEAD_W:OD_K + h * HEAD_W + GLA_DK] = 256 + np.arange(h * GLA_DK, (h + 1) * GLA_DK)
    src[OD_V:OD_V + 512] = 512 + np.arange(512)
    src[OD_G:OD_G + 512] = 1024 + np.arange(512)
    src[OD_Z:OD_Z + 512] = 1568 + np.arange(512)
    src[OD_XS:OD_XS + 512] = 2080 + np.arange(512)
    src[OD_BM:OD_BM + 256] = 2592 + np.arange(256)
    src[OD_CM:OD_CM + 256] = 2848 + np.arange(256)
    src[OD_MISC + MISC_LRF:OD_MISC + MISC_LRF + 16] = 1536 + np.arange(16)
    src[OD_MISC + MISC_LRB:OD_MISC + MISC_LRB + 16] = 1552 + np.arange(16)
    src[OD_MISC + MISC_DTF:OD_MISC + MISC_DTF + 8] = 3104 + np.arange(8)
    src[OD_MISC + MISC_DTB:OD_MISC + MISC_DTB + 8] = 3112 + np.arange(8)
    return src


def _relayout_odd_w(w):
    src = _odd_column_map()
    cols = jnp.take(w, jnp.asarray(np.maximum(src, 0)), axis=1)
    return jnp.where(jnp.asarray(src >= 0)[None, :], cols, 0.0)


def _pad_rows(rows_list, width):
    out = [jnp.broadcast_to(jnp.asarray(r, F32).reshape(1, width), (1, width)) for r in rows_list]
    out += [jnp.zeros((1, width), F32)] * (8 - len(out))
    return jnp.concatenate(out, axis=0)


def _hgrn_params(lb, gain):
    la = jnp.log(jnp.maximum(lb, LB_FLOOR))
    l1 = jnp.log1p(-lb)
    oml = 1.0 - lb
    return _pad_rows([la[0], la[1], l1[0], l1[1], oml[0], oml[1], jnp.tile(gain, 4)], 4 * HEAD_W)


def _gla_params(gate_w, gate_b, gain):
    wg = jnp.zeros((2, HEAD_W, 4 * HEAD_W), F32)
    gb = jnp.zeros((2, 4 * HEAD_W), F32)
    for h in range(4):
        s = slice(h * GLA_DK, (h + 1) * GLA_DK)
        dst = slice(h * HEAD_W, h * HEAD_W + GLA_DK)
        for d, off in enumerate((MISC_LRF, MISC_LRB)):
            wg = wg.at[d, off:off + 16, dst].set(gate_w[d][:, s])
            gb = gb.at[d, dst].set(gate_b[d, s])
    return wg.astype(BF16), _pad_rows([gb[0], gb[1], jnp.tile(gain, 4)], 4 * HEAD_W)


def _ssd_params(dt_bias, a_log, d_skip, norm_g):
    sel = np.zeros((2, 2, HEAD_W, 4 * HEAD_W), np.float32)
    for g in range(2):
        for d, off in enumerate((MISC_DTF, MISC_DTB)):
            for hh in range(4):
                sel[g, d, off + 4 * g + hh, hh * HEAD_W:(hh + 1) * HEAD_W] = 1.0
    par, par2 = [], []
    neg_a = -jnp.exp(a_log.astype(F32))
    for g in range(2):
        hs = slice(4 * g, 4 * g + 4)
        rep = lambda v: jnp.repeat(v[hs], HEAD_W)
        par.append(_pad_rows([rep(dt_bias[0]), rep(dt_bias[1]), rep(neg_a[0]), rep(neg_a[1])], 4 * HEAD_W))
        par2.append(_pad_rows([jnp.repeat(d_skip[hs], SSD_DH), norm_g[g * 256:(g + 1) * 256]], 4 * SSD_DH))
    return jnp.asarray(sel, BF16), jnp.stack(par), jnp.stack(par2)


def kernel(x, c, ctx, c_ctx, mod_w, mod_b, ln_g, ln_b, ev_w_in, ev_w_out, hg_lb_logits, hg_norm_g, at_q_norm_g, at_k_norm_g, od_w_in, od_w_out, gla_gate_w, gla_gate_b, gla_norm_g, ssd_conv_w, ssd_conv_b, ssd_dt_bias, ssd_a_log, ssd_d, ssd_norm_g, peer_wq, peer_subkeys, peer_u, peer_v):
    nb, seq, _ = x.shape
    assert ctx.shape[1] == CTX and seq % TM == 0 and nb + 1 <= MOD_ROWS
    rows = CTX + seq
    n = nb * rows

    cc = jnp.concatenate([c, c_ctx[None, :], jnp.zeros((MOD_ROWS - nb - 1, D), F32)], axis=0)
    mods = _modulation(cc, mod_w, mod_b)
    mods = mods.reshape(DEPTH, MOD_ROWS, N_MOD, D).transpose(0, 2, 1, 3).reshape(DEPTH, N_MOD * MOD_ROWS, 1, D)

    sm = jax.nn.softmax(hg_lb_logits.astype(F32), axis=1)
    hg_lb = jnp.cumsum(sm, axis=1) - sm[:, :1]
    tables = _rope_tables(seq)

    h = jnp.concatenate([ctx, x], axis=1).reshape(n, D)
    pre = None
    for l in range(DEPTH):
        j = l // 2
        modl = mods[l]
        if l % 2 == 0:
            y, h = _inproj(h, ev_w_in[j].astype(BF16), modl, nb, pre)
            y3 = y.reshape(nb, rows, EV_W)
            mix_a = _vscan("hgrn", y3, ((EV_Q, EV_I, EV_ZF), (EV_Q, EV_I, EV_ZB)), EV_G,
                           _hgrn_params(hg_lb[:, j], hg_norm_g[j]))
            mix_b = _attention(y3, tables, at_q_norm_g[j].reshape(1, HEAD_W), at_k_norm_g[j].reshape(1, HEAD_W))
            w_out = ev_w_out[j]
        else:
            y, h = _inproj(h, _relayout_odd_w(od_w_in[j]).astype(BF16), modl, nb, pre)
            y3 = y.reshape(nb, rows, OD_W)
            wg, gpar = _gla_params(gla_gate_w[j], gla_gate_b[j], gla_norm_g[j])
            mix_a = _vscan("gla", y3, ((OD_Q, OD_K, OD_V),) * 2, OD_G, gpar, wg)
            xbc = _ssd_conv(y3, ssd_conv_w[j], ssd_conv_b[j])
            sel, spar, spar2 = _ssd_params(ssd_dt_bias[j], ssd_a_log[j], ssd_d[j], ssd_norm_g[j])
            mix_b = _ssd_scan(xbc, y3, sel, spar, spar2)
            w_out = od_w_out[j]
        lg1, lb1 = ln_g[l, 0].reshape(1, D), ln_b[l, 0].reshape(1, D)
        sk = peer_subkeys[l].reshape(2 * PEER_HEADS, PEER_NKEYS, PEER_NKEYS).astype(BF16)
        h1, hm, st = _outproj(h, mix_a.reshape(n, D // 2), mix_b.reshape(n, D // 2), modl,
                              w_out.astype(BF16), lg1, lb1, peer_wq[l].astype(BF16), sk, nb)
        stats = _peer_stats(st)
        yp = _peer_experts(hm, st, stats, peer_u[l].astype(BF16), peer_v[l].astype(BF16).T)
        pre = (yp, modl, ln_g[l, 1].reshape(1, D), ln_b[l, 1].reshape(1, D))
        h = h1
    out = _final(h, pre[0], pre[1], pre[2], pre[3], nb)
    return out.reshape(nb, seq, D)
```

```python
import functools
import math

import numpy as np
import jax
import jax.numpy as jnp
from jax import lax
from jax.experimental import pallas as pl
from jax.experimental.pallas import tpu as pltpu

F32 = jnp.float32
BF16 = jnp.bfloat16
HIGHEST = lax.Precision.HIGHEST

D = 1024
DEPTH = 4
N_MOD = 6
CTX = 256
GRID_W = 64
DN_ALPHA = (2 * DEPTH) ** 0.25
NORM_EPS = 1e-6
ROPE_THETA = 10000.0
LB_FLOOR = 1e-30
TM = 256
MOD_ROWS = 24
VMEM_LIMIT = 56 * 1024 * 1024

HEAD_W = 128
VEC_CHUNK = 16
SSD_CHUNK = 128
SSD_DH = 64
GLA_DK = 64
GLA_GATE_NORM = 16.0
PEER_HEADS = 8
PEER_NKEYS = 128
PEER_TOPK = 16
PEER_T = 512
PEER_EB = 2048

EV_Q, EV_I, EV_ZF, EV_ZB, EV_G, EV_AQ, EV_AK, EV_AV = 0, 512, 1024, 1536, 2048, 2560, 3072, 3328
EV_W = 3584
OD_Q, OD_K, OD_V, OD_G, OD_Z, OD_XS, OD_BM, OD_CM, OD_MISC = 0, 512, 1024, 1536, 2048, 2560, 3072, 3328, 3584
OD_W = 3712
MISC_LRF, MISC_LRB, MISC_DTF, MISC_DTB = 0, 16, 32, 40

NT_DIMS = (((1,), (1,)), ((), ()))
TN_DIMS = (((0,), (0,)), ((), ()))


def _cparams(sem):
    return pltpu.CompilerParams(dimension_semantics=sem, vmem_limit_bytes=VMEM_LIMIT)


def _layer_norm(y, g, b):
    mu = jnp.mean(y, axis=-1, keepdims=True)
    yc = y - mu
    var = jnp.mean(yc * yc, axis=-1, keepdims=True)
    return yc * lax.rsqrt(var + NORM_EPS) * g + b


def _rms(y, g):
    return y * lax.rsqrt(jnp.mean(y * y, axis=-1, keepdims=True) + NORM_EPS) * g


def _log_sigmoid(z):
    return jnp.minimum(z, 0.0) - jnp.log(1.0 + jnp.exp(-jnp.abs(z)))


def _silu(z):
    return z * jax.nn.sigmoid(z)


def _split_hi_lo(a):
    hi = a.astype(BF16)
    lo = (a - hi.astype(F32)).astype(BF16)
    return hi, lo


def _mod_body(c_ref, w_ref, b_ref, o_ref):
    a = _silu(c_ref[...])
    a_hi, a_lo = _split_hi_lo(a)
    w_hi, w_lo = _split_hi_lo(w_ref[0])
    acc = jnp.dot(a_hi, w_hi, preferred_element_type=F32)
    acc += jnp.dot(a_lo, w_hi, preferred_element_type=F32)
    acc += jnp.dot(a_hi, w_lo, preferred_element_type=F32)
    o_ref[0] = acc + b_ref[0]


def _modulation(cc, mod_w, mod_b):
    tn = 1536
    nw = mod_w.shape[-1]
    return pl.pallas_call(
        _mod_body,
        grid=(DEPTH, nw // tn),
        in_specs=[
            pl.BlockSpec((MOD_ROWS, D), lambda l, j: (0, 0)),
            pl.BlockSpec((1, D, tn), lambda l, j: (l, 0, j)),
            pl.BlockSpec((1, 1, tn), lambda l, j: (l, 0, j)),
        ],
        out_specs=pl.BlockSpec((1, MOD_ROWS, tn), lambda l, j: (l, 0, j)),
        out_shape=jax.ShapeDtypeStruct((DEPTH, MOD_ROWS, nw), F32),
        compiler_params=_cparams(("parallel", "parallel")),
        name="modulation",
    )(cc, mod_w, mod_b.reshape(DEPTH, 1, nw))


def _mod_spec(k, n_batch, tiles_per_batch):
    def index(i):
        row = jnp.where(i % tiles_per_batch == 0, n_batch, i // tiles_per_batch)
        return (k * MOD_ROWS + row, 0, 0)
    return pl.BlockSpec((1, 1, D), index)


def _inproj_body(has_pre, *refs):
    if has_pre:
        x_ref, y_ref, g2_ref, lng_ref, lnb_ref, sh_ref, sc_ref, w_ref, o_ref, h_ref = refs
        h = _layer_norm(DN_ALPHA * x_ref[...] + g2_ref[0] * y_ref[...], lng_ref[...], lnb_ref[...])
        h_ref[...] = h
    else:
        x_ref, sh_ref, sc_ref, w_ref, o_ref = refs
        h = x_ref[...]
    u = h * (1.0 + sc_ref[0]) + sh_ref[0]
    o_ref[...] = jnp.dot(u.astype(BF16), w_ref[...], preferred_element_type=F32)


def _inproj(h, w, modl, n_batch, pre=None):
    n = h.shape[0]
    nw = w.shape[1]
    tpb = n // n_batch // TM
    tok = pl.BlockSpec((TM, D), lambda i: (i, 0))
    vec = pl.BlockSpec((1, D), lambda i: (0, 0))
    in_specs = [tok]
    args = [h]
    if pre is not None:
        y, prev_modl, ln_g, ln_b = pre
        in_specs += [tok, _mod_spec(5, n_batch, tpb), vec, vec]
        args += [y, prev_modl, ln_g, ln_b]
    in_specs += [_mod_spec(0, n_batch, tpb), _mod_spec(1, n_batch, tpb),
                 pl.BlockSpec((D, nw), lambda i: (0, 0))]
    args += [modl, modl, w]
    out_specs = [pl.BlockSpec((TM, nw), lambda i: (i, 0))]
    out_shape = [jax.ShapeDtypeStruct((n, nw), F32)]
    if pre is not None:
        out_specs.append(tok)
        out_shape.append(jax.ShapeDtypeStruct((n, D), F32))
    res = pl.pallas_call(
        functools.partial(_inproj_body, pre is not None),
        grid=(n // TM,),
        in_specs=in_specs,
        out_specs=out_specs,
        out_shape=out_shape,
        compiler_params=_cparams(("parallel",)),
        name="inproj",
    )(*args)
    return (res[0], res[1]) if pre is not None else (res[0], h)


def _final_body(x_ref, y_ref, g2_ref, lng_ref, lnb_ref, o_ref):
    o_ref[...] = _layer_norm(DN_ALPHA * x_ref[...] + g2_ref[0] * y_ref[...], lng_ref[...], lnb_ref[...])


def _final(h, y, modl, ln_g, ln_b, n_batch):
    n = h.shape[0]
    lat = n // n_batch // TM
    tok = pl.BlockSpec((TM, D), lambda b, j: (b * lat + j, 0))
    vec = pl.BlockSpec((1, D), lambda b, j: (0, 0))
    return pl.pallas_call(
        _final_body,
        grid=(n_batch, lat),
        in_specs=[tok, tok, pl.BlockSpec((1, 1, D), lambda b, j: (5 * MOD_ROWS + b, 0, 0)), vec, vec],
        out_specs=tok,
        out_shape=jax.ShapeDtypeStruct((n, D), F32),
        compiler_params=_cparams(("parallel", "parallel")),
        name="final_norm",
    )(h, y, modl, ln_g, ln_b)


def _vscan_body(mode, nblk, *refs):
    C = VEC_CHUNK
    nh = 4
    if mode == "hgrn":
        qf_ref, vf_ref, zf_ref, qb_ref, vb_ref, zb_ref, g_ref, par_ref, o_ref, st_ref, ob_ref = refs
        dir_refs = ((qf_ref, vf_ref, zf_ref), (qb_ref, vb_ref, zb_ref))
    else:
        (qf_ref, kf_ref, vf_ref, mf_ref, qb_ref, kb_ref, vb_ref, mb_ref, g_ref, wg_ref, par_ref,
         o_ref, st_ref, ob_ref) = refs
        dir_refs = ((qf_ref, kf_ref, vf_ref, mf_ref), (qb_ref, kb_ref, vb_ref, mb_ref))
    rows = o_ref.shape[1]
    rb = qf_ref.shape[1]
    nch = rb // C
    i = pl.program_id(1)
    bi = jnp.where(i == 0, 0, nblk - i)

    @pl.when(i == 0)
    def _():
        st_ref[...] = jnp.zeros(st_ref.shape, F32)

    w = nh * HEAD_W
    ri = lax.broadcasted_iota(jnp.int32, (rb, rb), 0)
    ci = lax.broadcasted_iota(jnp.int32, (rb, rb), 1)
    rid = lax.broadcasted_iota(jnp.int32, (rb, w), 0)
    levels = [C << s for s in range((rb // C).bit_length() - 1)]
    neg = -1e30

    def group_row(x, grp, r):
        g = x.reshape(rb // grp, grp, w)
        return jnp.broadcast_to(g[:, r:r + 1, :], g.shape).reshape(rb, w)

    def nt(a, b):
        return lax.dot_general(a, b, NT_DIMS, preferred_element_type=F32)

    def same_group(grp):
        sh = grp.bit_length() - 1
        return (ri >> sh) == (ci >> sh)

    def direction(d, out_ref, row0):
        refs_d = dir_refs[d]
        q = refs_d[0][0]
        v16 = refs_d[-2 if mode == "gla" else 1][0].astype(BF16)
        if mode == "hgrn":
            z = refs_d[2][0]
            qh = _silu(q) * (HEAD_W ** -0.5)
            la, l1, oml = par_ref[d:d + 1, :], par_ref[2 + d:3 + d, :], par_ref[4 + d:5 + d, :]
            bb = l1 + _log_sigmoid(z)
            log_f = jnp.maximum(la, bb) + jnp.log(1.0 + jnp.exp(-jnp.abs(la - bb)))
            k = oml * jax.nn.sigmoid(-z)
        else:
            qh = q * (GLA_DK ** -0.5)
            k = refs_d[1][0]
            pre = jnp.dot(refs_d[3][0].astype(BF16), wg_ref[d], preferred_element_type=F32)
            log_f = _log_sigmoid(pre + par_ref[d:d + 1, :]) * (1.0 / GLA_GATE_NORM)
        tri = (ri >= ci if d == 0 else ri <= ci).astype(BF16)
        hi, lo = _split_hi_lo(log_f)
        cc = jnp.dot(tri, jnp.concatenate([hi, lo], axis=-1), preferred_element_type=F32)
        cum = cc[:, :w] + cc[:, w:]
        first_row, last_row = (0, rb - 1) if d == 0 else (rb - 1, 0)

        x0 = cum - group_row(cum, C, first_row % C)
        qs = [(qh * jnp.exp(x0)).astype(BF16)]
        ks = [(k * jnp.exp(-x0)).astype(BF16)]
        for m in levels:
            is_query = (rid & m) != 0 if d == 0 else (rid & m) == 0
            bnd = group_row(cum, 2 * m, m - 1 if d == 0 else m)
            qs.append((qh * jnp.exp(jnp.where(is_query, cum - bnd, neg))).astype(BF16))
            ks.append((k * jnp.exp(jnp.where(is_query, neg, bnd - cum))).astype(BF16))
        last = cum[last_row:last_row + 1, :]
        qe = (qh * jnp.exp(cum)).astype(BF16)
        kl = (k * jnp.exp(last - cum)).astype(BF16)
        dec = jnp.exp(last)

        outs = []
        for hh in range(nh):
            sl = slice(hh * HEAD_W, (hh + 1) * HEAD_W)
            p0 = nt(qs[0][:, sl], ks[0][:, sl])
            causal = ri >= ci if d == 0 else ri <= ci
            att = jnp.where(same_group(C), jnp.where(causal, p0, 0.0), 0.0)
            for lv, m in enumerate(levels):
                p = nt(qs[lv + 1][:, sl], ks[lv + 1][:, sl])
                att += p if 2 * m == rb else jnp.where(same_group(2 * m), p, 0.0)
            o = jnp.dot(att.astype(BF16), v16[:, sl], preferred_element_type=F32)
            s_t = st_ref[d * nh + hh]
            o += nt(qe[:, sl], s_t.astype(BF16))
            kv_t = lax.dot_general(v16[:, sl], kl[:, sl], TN_DIMS, preferred_element_type=F32)
            st_ref[d * nh + hh] = s_t * dec[:, sl] + kv_t
            outs.append(o)
        out_ref[row0, :] = jnp.concatenate(outs, axis=-1)

    direction(0, o_ref.at[0], pl.ds(pl.multiple_of(i * rb, rb), rb))
    direction(1, ob_ref, pl.ds(pl.multiple_of(bi * rb, rb), rb))

    @pl.when(i == nblk - 1)
    def _():
        def post(j, carry):
            rs = pl.ds(pl.multiple_of(j * rb, rb), rb)
            for hh in range(nh):
                lanes = pl.ds(hh * HEAD_W, HEAD_W)
                o = o_ref[0, rs, lanes] + ob_ref[rs, lanes]
                g = g_ref[0, rs, lanes]
                if mode == "hgrn":
                    res = _rms(o * jax.nn.sigmoid(g), par_ref[6:7, lanes])
                else:
                    res = _rms(o, par_ref[2:3, lanes]) * _silu(g)
                o_ref[0, rs, lanes] = res
            return carry

        lax.fori_loop(0, rows // rb, post, 0)


def _vscan(mode, y3, col_offsets, g_offset, params, wg=None):
    nb, rows, _ = y3.shape
    width = 4 * HEAD_W
    rb = CTX
    nblk = rows // rb
    fwd = lambda w, off: pl.BlockSpec((1, rb, w), lambda b, i: (b, i, off // w))
    bwd = lambda w, off: pl.BlockSpec((1, rb, w), lambda b, i: (b, jnp.where(i == 0, 0, nblk - i), off // w))
    in_specs, args = [], []
    for mk in (fwd, bwd):
        for off in col_offsets[mk is bwd]:
            in_specs.append(mk(width, off))
            args.append(y3)
        if mode == "gla":
            in_specs.append(mk(HEAD_W, OD_MISC))
            args.append(y3)
    in_specs.append(pl.BlockSpec((1, rows, width), lambda b, i: (b, 0, g_offset // width)))
    args.append(y3)
    if mode == "gla":
        in_specs.append(pl.BlockSpec((2, HEAD_W, width), lambda b, i: (0, 0, 0)))
        args.append(wg)
    in_specs.append(pl.BlockSpec((8, width), lambda b, i: (0, 0)))
    args.append(params)
    return pl.pallas_call(
        functools.partial(_vscan_body, mode, nblk),
        grid=(nb, nblk),
        in_specs=in_specs,
        out_specs=pl.BlockSpec((1, rows, width), lambda b, i: (b, 0, 0)),
        out_shape=jax.ShapeDtypeStruct((nb, rows, width), F32),
        scratch_shapes=[pltpu.VMEM((8, HEAD_W, HEAD_W), F32), pltpu.VMEM((rows, width), F32)],
        compiler_params=_cparams(("parallel", "arbitrary")),
        name="vscan_" + mode,
    )(*args)


def _rope(x, c, sa, sb):
    return x * c + pltpu.roll(x, HEAD_W - 1, 1) * sa + pltpu.roll(x, 1, 1) * sb


def _attn_body(q_ref, k_ref, v_ref, ck_ref, sak_ref, sbk_ref, cq_ref, saq_ref, sbq_ref,
               qg_ref, kg_ref, o_ref, kp_ref, vp_ref):
    qi = pl.program_id(2)
    rows = k_ref.shape[1]
    blk = 256

    @pl.when(qi == 0)
    def _():
        def prep(i, carry):
            rs = pl.ds(pl.multiple_of(i * blk, blk), blk)
            kn = _rms(k_ref[0, rs, :], kg_ref[...])
            kp_ref[rs, :] = _rope(kn, ck_ref[rs, :], sak_ref[rs, :], sbk_ref[rs, :]).astype(BF16)
            vp_ref[rs, :] = v_ref[0, rs, :].astype(BF16)
            return carry
        lax.fori_loop(0, rows // blk, prep, 0)

    def attend(nk):
        outs = []
        for r in range(2):
            qn = _rms(q_ref[0, :, r * HEAD_W:(r + 1) * HEAD_W], qg_ref[...])
            qr = (_rope(qn, cq_ref[...], saq_ref[...], sbq_ref[...]) * (HEAD_W ** -0.5)).astype(BF16)
            s = lax.dot_general(qr, kp_ref[0:nk, :], NT_DIMS, preferred_element_type=F32)
            m = jnp.max(s, axis=-1, keepdims=True)
            p = jnp.exp(s - m)
            l = jnp.sum(p, axis=-1, keepdims=True)
            o = jnp.dot(p.astype(BF16), vp_ref[0:nk, :], preferred_element_type=F32)
            outs.append(o / l)
        o_ref[0] = jnp.concatenate(outs, axis=-1)

    @pl.when(qi == 0)
    def _():
        attend(CTX)

    @pl.when(qi > 0)
    def _():
        attend(rows)


def _attention(y3, tables, q_gain, k_gain):
    nb, rows, _ = y3.shape
    c, sa, sb = tables
    full = pl.BlockSpec((rows, HEAD_W), lambda b, g, i: (0, 0))
    tile = pl.BlockSpec((TM, HEAD_W), lambda b, g, i: (i, 0))
    vec = pl.BlockSpec((1, HEAD_W), lambda b, g, i: (0, 0))
    return pl.pallas_call(
        _attn_body,
        grid=(nb, 2, rows // TM),
        in_specs=[
            pl.BlockSpec((1, TM, 2 * HEAD_W), lambda b, g, i: (b, i, EV_AQ // (2 * HEAD_W) + g)),
            pl.BlockSpec((1, rows, HEAD_W), lambda b, g, i: (b, 0, EV_AK // HEAD_W + g)),
            pl.BlockSpec((1, rows, HEAD_W), lambda b, g, i: (b, 0, EV_AV // HEAD_W + g)),
            full, full, full, tile, tile, tile, vec, vec,
        ],
        out_specs=pl.BlockSpec((1, TM, 2 * HEAD_W), lambda b, g, i: (b, i, g)),
        out_shape=jax.ShapeDtypeStruct((nb, rows, 4 * HEAD_W), F32),
        scratch_shapes=[pltpu.VMEM((rows, HEAD_W), BF16), pltpu.VMEM((rows, HEAD_W), BF16)],
        compiler_params=_cparams(("parallel", "parallel", "arbitrary")),
        name="attention",
    )(y3, y3, y3, c, sa, sb, c, sa, sb, q_gain, k_gain)


def _conv_body(x_ref, w_ref, b_ref, o_ref):
    rows = x_ref.shape[1]
    x = x_ref[0]
    row = lax.broadcasted_iota(jnp.int32, x.shape, 0)
    lo = jnp.where(row < CTX, 0, CTX)
    hi = jnp.where(row < CTX, CTX, rows)
    taps = w_ref.shape[0]
    acc = jnp.zeros(x.shape, F32) + b_ref[...]
    for j in range(taps):
        off = j - taps // 2
        xs = x if off == 0 else pltpu.roll(x, (-off) % rows, 0)
        src = row + off
        xs = jnp.where(src >= lo, jnp.where(src < hi, xs, 0.0), 0.0)
        acc += xs * w_ref[j:j + 1, :]
    o_ref[0] = _silu(acc)


def _ssd_conv(y3, conv_w, conv_b):
    nb, rows, _ = y3.shape
    nc = conv_w.shape[1]
    return pl.pallas_call(
        _conv_body,
        grid=(nb, nc // HEAD_W),
        in_specs=[
            pl.BlockSpec((1, rows, HEAD_W), lambda b, j: (b, 0, OD_XS // HEAD_W + j)),
            pl.BlockSpec((conv_w.shape[0], HEAD_W), lambda b, j: (0, j)),
            pl.BlockSpec((1, HEAD_W), lambda b, j: (0, j)),
        ],
        out_specs=pl.BlockSpec((1, rows, HEAD_W), lambda b, j: (b, 0, j)),
        out_shape=jax.ShapeDtypeStruct((nb, rows, nc), F32),
        compiler_params=_cparams(("parallel", "parallel")),
        name="ssd_conv",
    )(y3, conv_w, conv_b.reshape(1, nc))


def _ssd_body(n_ctx_chunks, n_chunks, x_ref, bm_ref, cm_ref, z_ref, misc_ref, sel_ref, par_ref,
              par2_ref, o_ref, st_ref, ob_ref):
    C = SSD_CHUNK
    P = SSD_DH
    rows = o_ref.shape[1]
    w = 4 * HEAD_W
    st_ref[...] = jnp.zeros(st_ref.shape, F32)
    ri = lax.broadcasted_iota(jnp.int32, (C, C), 0)
    ci = lax.broadcasted_iota(jnp.int32, (C, C), 1)
    masks = (ri >= ci, ri <= ci)
    tri = tuple(mk.astype(BF16) for mk in masks)

    def narrow(a):
        return jnp.concatenate([a[:, hh * HEAD_W:hh * HEAD_W + P] for hh in range(4)], axis=-1)

    def chunk(d, r0, out_ref):
        rs = pl.ds(r0, C)
        x = x_ref[0, rs, :]
        bmat = bm_ref[0, rs, :]
        cb16 = cm_ref[0, rs, :].astype(BF16)
        b16 = bmat.astype(BF16)
        bt16 = bmat.T.astype(BF16)
        scores = lax.dot_general(cb16, b16, NT_DIMS, preferred_element_type=F32)
        mh, ml = _split_hi_lo(misc_ref[0, rs, :])
        dd = jnp.dot(jnp.concatenate([mh, ml], axis=0), sel_ref[0, d], preferred_element_type=F32)
        dt = dd[:C] + dd[C:] + par_ref[0, d:d + 1, :]
        dt = jnp.maximum(dt, 0.0) + jnp.log(1.0 + jnp.exp(-jnp.abs(dt)))
        la = dt * par_ref[0, 2 + d:3 + d, :]
        lh, ll = _split_hi_lo(la)
        cc = jnp.dot(tri[d], jnp.concatenate([lh, ll], axis=-1), preferred_element_type=F32)
        cum = cc[:, :w] + cc[:, w:]
        last = cum[C - 1:C, :] if d == 0 else cum[0:1, :]
        xdt = x * narrow(dt)
        s_t = st_ref[d]
        y_in = jnp.dot(cb16, s_t.astype(BF16), preferred_element_type=F32) * narrow(jnp.exp(cum))
        xw = (xdt * narrow(jnp.exp(last - cum))).astype(BF16)
        st_ref[d] = s_t * narrow(jnp.exp(last)) + jnp.dot(bt16, xw, preferred_element_type=F32)
        xdt16 = xdt.astype(BF16)
        ys = []
        for hh in range(4):
            cum_h = cum[:, hh * HEAD_W:(hh + 1) * HEAD_W]
            seg = jnp.where(masks[d], cum_h - cum_h.T, 0.0)
            dec = jnp.where(masks[d], jnp.exp(seg), 0.0)
            ys.append(jnp.dot((scores * dec).astype(BF16), xdt16[:, hh * P:(hh + 1) * P],
                              preferred_element_type=F32))
        out_ref[rs, :] = jnp.concatenate(ys, axis=-1) + y_in

    def step(t, carry):
        cf = t
        cb = jnp.where(t < n_ctx_chunks, n_ctx_chunks - 1 - t, n_chunks + n_ctx_chunks - 1 - t)
        chunk(0, pl.multiple_of(cf * C, C), o_ref.at[0])
        chunk(1, pl.multiple_of(cb * C, C), ob_ref)
        return carry

    lax.fori_loop(0, n_chunks, step, 0)

    blk = 256

    def post(i, carry):
        rs = pl.ds(pl.multiple_of(i * blk, blk), blk)
        y = o_ref[0, rs, :] + ob_ref[rs, :] + par2_ref[0, 0:1, :] * x_ref[0, rs, :]
        o_ref[0, rs, :] = _rms(y * _silu(z_ref[0, rs, :]), par2_ref[0, 1:2, :])
        return carry

    lax.fori_loop(0, rows // blk, post, 0)


def _ssd_scan(xbc, y3, sel, par, par2):
    nb, rows, _ = xbc.shape
    gw = 4 * SSD_DH
    return pl.pallas_call(
        functools.partial(_ssd_body, CTX // SSD_CHUNK, rows // SSD_CHUNK),
        grid=(nb, 2),
        in_specs=[
            pl.BlockSpec((1, rows, gw), lambda b, g: (b, 0, g)),
            pl.BlockSpec((1, rows, HEAD_W), lambda b, g: (b, 0, 2 * gw // HEAD_W + g)),
            pl.BlockSpec((1, rows, HEAD_W), lambda b, g: (b, 0, 2 * gw // HEAD_W + 2 + g)),
            pl.BlockSpec((1, rows, gw), lambda b, g: (b, 0, OD_Z // gw + g)),
            pl.BlockSpec((1, rows, HEAD_W), lambda b, g: (b, 0, OD_MISC // HEAD_W)),
            pl.BlockSpec((1, 2, HEAD_W, 4 * HEAD_W), lambda b, g: (g, 0, 0, 0)),
            pl.BlockSpec((1, 8, 4 * HEAD_W), lambda b, g: (g, 0, 0)),
            pl.BlockSpec((1, 8, gw), lambda b, g: (g, 0, 0)),
        ],
        out_specs=pl.BlockSpec((1, rows, gw), lambda b, g: (b, 0, g)),
        out_shape=jax.ShapeDtypeStruct((nb, rows, 2 * gw), F32),
        scratch_shapes=[pltpu.VMEM((2, HEAD_W, gw), F32), pltpu.VMEM((rows, gw), F32)],
        compiler_params=_cparams(("parallel", "parallel")),
        name="ssd_scan",
    )(xbc, xbc, xbc, y3, y3, sel, par, par2)


def _outproj_body(h_ref, a_ref, b_ref, g1_ref, sh2_ref, sc2_ref, wo_ref, lng_ref, lnb_ref, wq_ref,
                  sk_ref, h1_ref, hm_ref, st_ref):
    mix = jnp.concatenate([a_ref[...], b_ref[...]], axis=-1).astype(BF16)
    o = jnp.dot(mix, wo_ref[...], preferred_element_type=F32)
    h1 = _layer_norm(DN_ALPHA * h_ref[...] + g1_ref[0] * o, lng_ref[...], lnb_ref[...])
    h1_ref[...] = h1
    hm = (h1 * (1.0 + sc2_ref[0]) + sh2_ref[0]).astype(BF16)
    hm_ref[...] = hm
    qry = jnp.dot(hm, wq_ref[...], preferred_element_type=F32).astype(BF16)
    for j in range(2 * PEER_HEADS):
        qj = qry[:, j * PEER_NKEYS:(j + 1) * PEER_NKEYS]
        st_ref[j * PEER_NKEYS:(j + 1) * PEER_NKEYS, :] = lax.dot_general(
            sk_ref[j], qj, NT_DIMS, preferred_element_type=F32)


def _outproj(h, mix_a, mix_b, modl, w_out, ln_g, ln_b, wq, sk, n_batch, latent_only=False):
    n = h.shape[0]
    tpb = n // n_batch // TM
    if latent_only:
        lat = tpb - 1
        n_out = n_batch * lat * TM
        src = lambda j: (j // lat) * tpb + 1 + j % lat
        mod = lambda k: pl.BlockSpec((1, 1, D), lambda j: (k * MOD_ROWS + j // lat, 0, 0))
    else:
        n_out = n
        src = lambda j: j
        mod = lambda k: _mod_spec(k, n_batch, tpb)
    tok_in = pl.BlockSpec((TM, D), lambda j: (src(j), 0))
    half = pl.BlockSpec((TM, D // 2), lambda j: (src(j), 0))
    tok = pl.BlockSpec((TM, D), lambda j: (j, 0))
    vec = pl.BlockSpec((1, D), lambda j: (0, 0))
    nq = wq.shape[1]
    return pl.pallas_call(
        _outproj_body,
        grid=(n_out // TM,),
        in_specs=[tok_in, half, half, mod(2), mod(3), mod(4), pl.BlockSpec((D, D), lambda j: (0, 0)), vec, vec,
                  pl.BlockSpec((D, nq), lambda j: (0, 0)),
                  pl.BlockSpec(sk.shape, lambda j: (0, 0, 0))],
        out_specs=[tok, tok, pl.BlockSpec((nq, TM), lambda j: (0, j))],
        out_shape=[jax.ShapeDtypeStruct((n_out, D), F32), jax.ShapeDtypeStruct((n_out, D), BF16),
                   jax.ShapeDtypeStruct((nq, n_out), F32)],
        compiler_params=_cparams(("parallel",)),
        name="outproj",
    )(h, mix_a, mix_b, modl, modl, modl, w_out, ln_g, ln_b, wq, sk)


def _sort16_network():
    comps = []

    def merge(lo, hi, r):
        step = r * 2
        if step < hi - lo:
            merge(lo, hi, step)
            merge(lo + r, hi, step)
            comps.extend((j, j + r) for j in range(lo + r, hi - r, step))
        else:
            comps.append((lo, lo + r))

    def sort(lo, hi):
        if hi - lo >= 1:
            mid = lo + (hi - lo) // 2
            sort(lo, mid)
            sort(mid + 1, hi)
            merge(lo, hi, 1)

    sort(0, 15)
    return comps


SORT16 = _sort16_network()
BITONIC16 = [(j, j + s) for s in (8, 4, 2, 1) for j in range(16) if not j & s]


def _compare_exchange(v, comps):
    v = list(v)
    for a, b in comps:
        v[a], v[b] = jnp.maximum(v[a], v[b]), jnp.minimum(v[a], v[b])
    return v


def _merge_top16(a, b_rev):
    c = [a[k] if b_rev[k] is None else jnp.maximum(a[k], b_rev[k]) for k in range(16)]
    return _compare_exchange(c, BITONIC16)


def _top16_keys(ref, row0, col):
    v = [ref[row0 + 8 * j:row0 + 8 * j + 8, col] for j in range(16)]
    v = _compare_exchange(v, SORT16)
    for shift in (4, 2, 1):
        rolled = [pltpu.roll(x, shift, 0) for x in v]
        v = _merge_top16(v, rolled[::-1])
    return v


def _stats_body(st_ref, o_ref):
    k = PEER_TOPK
    nk = PEER_NKEYS
    sub = lax.broadcasted_iota(jnp.int32, (8, 128), 0)

    def column(tc, carry):
        col = pl.ds(pl.multiple_of(tc * 128, 128), 128)
        tops = []
        for side in range(2):
            packed = None
            for h in range(PEER_HEADS):
                a = _top16_keys(st_ref, (2 * h + side) * nk, col)
                packed = a if h == 0 else [jnp.where(sub == h, a[r], packed[r]) for r in range(k)]
            tops.append(packed)
        t1, t2 = tops
        best = [t1[0] + t2[r] for r in range(k)]
        for r1 in range(1, k):
            ln = k // (r1 + 1)
            lst = [t1[r1] + t2[r] for r in range(ln)]
            best = _merge_top16(best, [None] * (k - ln) + lst[::-1])
        z = 1.0
        for r in range(1, k):
            z = z + jnp.exp(best[r] - best[0])
        o_ref[0, :, col] = t1[0]
        o_ref[1, :, col] = t2[0] + jnp.log(z)
        o_ref[2, :, col] = best[k - 1]
        for r in range(k):
            o_ref[3 + r, :, col] = t2[r]
        return carry

    lax.fori_loop(0, o_ref.shape[2] // 128, column, 0)


def _peer_stats(st):
    n = st.shape[1]
    ts = 512
    return pl.pallas_call(
        _stats_body,
        grid=(n // ts,),
        in_specs=[pl.BlockSpec((st.shape[0], ts), lambda i: (0, i))],
        out_specs=pl.BlockSpec((3 + PEER_TOPK, PEER_HEADS, ts), lambda i: (0, 0, i)),
        out_shape=jax.ShapeDtypeStruct((3 + PEER_TOPK, PEER_HEADS, n), F32),
        compiler_params=_cparams(("parallel",)),
        name="peer_stats",
    )(st)


def _count_prefix(pred, rows):
    pick = lambda a, b, c: jnp.where(c, b, a)
    c8 = pred(rows[7])
    c4 = pred(pick(rows[3], rows[11], c8))
    c2 = pred(pick(pick(rows[1], rows[5], c4), pick(rows[9], rows[13], c4), c8))
    ev = [rows[2 * j] for j in range(8)]
    c1 = pred(pick(pick(pick(ev[0], ev[1], c2), pick(ev[2], ev[3], c2), c4),
                   pick(pick(ev[4], ev[5], c2), pick(ev[6], ev[7], c2), c4), c8))
    c16 = pred(rows[15])
    val = lambda c, v: jnp.where(c, v, 0.0)
    return val(c8, 8.0) + val(c4, 4.0) + val(c2, 2.0) + val(c1, 1.0) + val(c16, 1.0)


def _expert_body(hm_ref, st_ref, stats_ref, u_ref, vt_ref, y_ref, e1_ref, n1_ref, e2_ref, r2_ref,
                 a_ref, p_ref, acc_ref, hmt_ref):
    e = pl.program_id(1)
    ne = pl.num_programs(1)
    t = hm_ref.shape[0]
    nk = PEER_NKEYS
    k = PEER_TOPK
    m = PEER_EB // nk
    ntc = t // 128

    @pl.when(e == 0)
    def _():
        def prep(tc, carry):
            col = pl.ds(pl.multiple_of(tc * 128, 128), 128)
            for h in range(PEER_HEADS):
                rows = slice(h * nk, (h + 1) * nk)
                s1 = st_ref[2 * h * nk:(2 * h + 1) * nk, col]
                s2 = st_ref[(2 * h + 1) * nk:(2 * h + 2) * nk, col]
                tau = stats_ref[2, h:h + 1, col]
                t2 = [stats_ref[3 + r, h:h + 1, col] for r in range(k)]
                n1 = _count_prefix(lambda thr: s1 + thr >= tau, t2)
                r2 = _count_prefix(lambda thr: thr > s2, t2)
                e1_ref[rows, col] = jnp.exp(s1 - stats_ref[0, h:h + 1, col])
                n1_ref[rows, col] = n1
                e2_ref[rows, col] = jnp.exp(s2 - stats_ref[1, h:h + 1, col]).astype(BF16)
                r2_ref[rows, col] = r2.astype(BF16)
            return carry

        lax.fori_loop(0, ntc, prep, 0)
        acc_ref[...] = jnp.zeros(acc_ref.shape, F32)
        hmt_ref[...] = hm_ref[...].astype(F32).T.astype(BF16)

    a_ref[...] = jnp.dot(u_ref[...], hmt_ref[...], preferred_element_type=F32)

    def build(tc, carry):
        col = pl.ds(pl.multiple_of(tc * 128, 128), 128)
        own = [pl.ds(pl.multiple_of(h * nk + e * m, m), m) for h in range(PEER_HEADS)]
        e1 = [e1_ref[own[h], col].astype(BF16) for h in range(PEER_HEADS)]
        n1 = [n1_ref[own[h], col].astype(BF16) for h in range(PEER_HEADS)]
        zero = jnp.zeros((), BF16)
        group = 4
        for g0 in range(0, m, group):
            gates = [jnp.zeros((nk, 128), BF16) for _ in range(group)]
            for h in range(PEER_HEADS):
                rows2 = slice(h * nk, (h + 1) * nk)
                r2 = r2_ref[rows2, col]
                e2 = e2_ref[rows2, col]
                for j in range(group):
                    n1b = jnp.broadcast_to(n1[h][g0 + j:g0 + j + 1, :], (nk, 128))
                    e1b = jnp.broadcast_to(e1[h][g0 + j:g0 + j + 1, :], (nk, 128))
                    gates[j] = gates[j] + e1b * jnp.where(r2 < n1b, e2, zero)
            for j in range(group):
                rows = slice((g0 + j) * nk, (g0 + j + 1) * nk)
                act = jax.nn.gelu(a_ref[rows, col], approximate=True)
                p_ref[rows, col] = act.astype(BF16) * gates[j]
        return carry

    lax.fori_loop(0, ntc, build, 0)
    acc_ref[...] += jnp.dot(vt_ref[...], p_ref[...], preferred_element_type=F32)

    @pl.when(e == ne - 1)
    def _():
        y_ref[...] = acc_ref[...].T


def _peer_experts(hm, st, stats, u, vt):
    n = hm.shape[0]
    ne = u.shape[0] // PEER_EB
    assert (PEER_EB // PEER_NKEYS) % 8 == 0
    t = PEER_T
    half = (PEER_HEADS * PEER_NKEYS, t)
    return pl.pallas_call(
        _expert_body,
        grid=(n // t, ne),
        in_specs=[
            pl.BlockSpec((t, D), lambda i, e: (i, 0)),
            pl.BlockSpec((st.shape[0], t), lambda i, e: (0, i)),
            pl.BlockSpec((stats.shape[0], PEER_HEADS, t), lambda i, e: (0, 0, i)),
            pl.BlockSpec((PEER_EB, D), lambda i, e: (e, 0)),
            pl.BlockSpec((D, PEER_EB), lambda i, e: (0, e)),
        ],
        out_specs=pl.BlockSpec((t, D), lambda i, e: (i, 0)),
        out_shape=jax.ShapeDtypeStruct((n, D), F32),
        scratch_shapes=[pltpu.VMEM(half, F32), pltpu.VMEM(half, F32), pltpu.VMEM(half, BF16),
                        pltpu.VMEM(half, BF16), pltpu.VMEM((PEER_EB, t), F32),
                        pltpu.VMEM((PEER_EB, t), BF16), pltpu.VMEM((D, t), F32),
                        pltpu.VMEM((D, t), BF16)],
        compiler_params=_cparams(("parallel", "arbitrary")),
        name="peer_experts",
    )(hm, st, stats, u, vt)


def _rope_tables(rows_lat):
    pairs = HEAD_W // 4
    t = jnp.arange(rows_lat)
    inv = ROPE_THETA ** (-jnp.arange(pairs, dtype=F32) / pairs)
    ang = jnp.concatenate([(t // GRID_W).astype(F32)[:, None] * inv,
                           (t % GRID_W).astype(F32)[:, None] * inv], axis=-1)
    cos, sin = jnp.cos(ang), jnp.sin(ang)
    zero = jnp.zeros_like(sin)
    c = jnp.stack([cos, cos], -1).reshape(rows_lat, HEAD_W)
    sa = jnp.stack([-sin, zero], -1).reshape(rows_lat, HEAD_W)
    sb = jnp.stack([zero, sin], -1).reshape(rows_lat, HEAD_W)
    ident = jnp.ones((CTX, HEAD_W), F32)
    none = jnp.zeros((CTX, HEAD_W), F32)
    return (jnp.concatenate([ident, c]), jnp.concatenate([none, sa]), jnp.concatenate([none, sb]))


def _odd_column_map():
    src = -np.ones((OD_W,), np.int64)
    for h in range(4):
        src[OD_Q + h * HEAD_W:OD_Q + h * HEAD_W + GLA_DK] = np.arange(h * GLA_DK, (h + 1) * GLA_DK)
        src[OD_K + h * HEAD_W:OD_K + h * HEAD_W + GLA_DK] = 256 + np.arange(h * GLA_DK, (h + 1) * GLA_DK)
    src[OD_V:OD_V + 512] = 512 + np.arange(512)
    src[OD_G:OD_G + 512] = 1024 + np.arange(512)
    src[OD_Z:OD_Z + 512] = 1568 + np.arange(512)
    src[OD_XS:OD_XS + 512] = 2080 + np.arange(512)
    src[OD_BM:OD_BM + 256] = 2592 + np.arange(256)
    src[OD_CM:OD_CM + 256] = 2848 + np.arange(256)
    src[OD_MISC + MISC_LRF:OD_MISC + MISC_LRF + 16] = 1536 + np.arange(16)
    src[OD_MISC + MISC_LRB:OD_MISC + MISC_LRB + 16] = 1552 + np.arange(16)
    src[OD_MISC + MISC_DTF:OD_MISC + MISC_DTF + 8] = 3104 + np.arange(8)
    src[OD_MISC + MISC_DTB:OD_MISC + MISC_DTB + 8] = 3112 + np.arange(8)
    return src


def _relayout_odd_w(w):
    src = _odd_column_map()
    cols = jnp.take(w, jnp.asarray(np.maximum(src, 0)), axis=1)
    return jnp.where(jnp.asarray(src >= 0)[None, :], cols, 0.0)


def _pad_rows(rows_list, width):
    out = [jnp.broadcast_to(jnp.asarray(r, F32).reshape(1, width), (1, width)) for r in rows_list]
    out += [jnp.zeros((1, width), F32)] * (8 - len(out))
    return jnp.concatenate(out, axis=0)


def _hgrn_params(lb, gain):
    la = jnp.log(jnp.maximum(lb, LB_FLOOR))
    l1 = jnp.log1p(-lb)
    oml = 1.0 - lb
    return _pad_rows([la[0], la[1], l1[0], l1[1], oml[0], oml[1], jnp.tile(gain, 4)], 4 * HEAD_W)


def _gla_params(gate_w, gate_b, gain):
    wg = jnp.zeros((2, HEAD_W, 4 * HEAD_W), F32)
    gb = jnp.zeros((2, 4 * HEAD_W), F32)
    for h in range(4):
        s = slice(h * GLA_DK, (h + 1) * GLA_DK)
        dst = slice(h * HEAD_W, h * HEAD_W + GLA_DK)
        for d, off in enumerate((MISC_LRF, MISC_LRB)):
            wg = wg.at[d, off:off + 16, dst].set(gate_w[d][:, s])
            gb = gb.at[d, dst].set(gate_b[d, s])
    return wg.astype(BF16), _pad_rows([gb[0], gb[1], jnp.tile(gain, 4)], 4 * HEAD_W)


def _ssd_params(dt_bias, a_log, d_skip, norm_g):
    sel = np.zeros((2, 2, HEAD_W, 4 * HEAD_W), np.float32)
    for g in range(2):
        for d, off in enumerate((MISC_DTF, MISC_DTB)):
            for hh in range(4):
                sel[g, d, off + 4 * g + hh, hh * HEAD_W:(hh + 1) * HEAD_W] = 1.0
    par, par2 = [], []
    neg_a = -jnp.exp(a_log.astype(F32))
    for g in range(2):
        hs = slice(4 * g, 4 * g + 4)
        rep = lambda v: jnp.repeat(v[hs], HEAD_W)
        par.append(_pad_rows([rep(dt_bias[0]), rep(dt_bias[1]), rep(neg_a[0]), rep(neg_a[1])], 4 * HEAD_W))
        par2.append(_pad_rows([jnp.repeat(d_skip[hs], SSD_DH), norm_g[g * 256:(g + 1) * 256]], 4 * SSD_DH))
    return jnp.asarray(sel, BF16), jnp.stack(par), jnp.stack(par2)


def kernel(x, c, ctx, c_ctx, mod_w, mod_b, ln_g, ln_b, ev_w_in, ev_w_out, hg_lb_logits, hg_norm_g, at_q_norm_g, at_k_norm_g, od_w_in, od_w_out, gla_gate_w, gla_gate_b, gla_norm_g, ssd_conv_w, ssd_conv_b, ssd_dt_bias, ssd_a_log, ssd_d, ssd_norm_g, peer_wq, peer_subkeys, peer_u, peer_v):
    nb, seq, _ = x.shape
    assert ctx.shape[1] == CTX and seq % TM == 0 and nb + 1 <= MOD_ROWS
    rows = CTX + seq
    n = nb * rows

    cc = jnp.concatenate([c, c_ctx[None, :], jnp.zeros((MOD_ROWS - nb - 1, D), F32)], axis=0)
    mods = _modulation(cc, mod_w, mod_b)
    mods = mods.reshape(DEPTH, MOD_ROWS, N_MOD, D).transpose(0, 2, 1, 3).reshape(DEPTH, N_MOD * MOD_ROWS, 1, D)

    sm = jax.nn.softmax(hg_lb_logits.astype(F32), axis=1)
    hg_lb = jnp.cumsum(sm, axis=1) - sm[:, :1]
    tables = _rope_tables(seq)

    h = jnp.concatenate([ctx, x], axis=1).reshape(n, D)
    pre = None
    for l in range(DEPTH):
        j = l // 2
        modl = mods[l]
        if l % 2 == 0:
            y, h = _inproj(h, ev_w_in[j].astype(BF16), modl, nb, pre)
            y3 = y.reshape(nb, rows, EV_W)
            mix_a = _vscan("hgrn", y3, ((EV_Q, EV_I, EV_ZF), (EV_Q, EV_I, EV_ZB)), EV_G,
                           _hgrn_params(hg_lb[:, j], hg_norm_g[j]))
            mix_b = _attention(y3, tables, at_q_norm_g[j].reshape(1, HEAD_W), at_k_norm_g[j].reshape(1, HEAD_W))
            w_out = ev_w_out[j]
        else:
            y, h = _inproj(h, _relayout_odd_w(od_w_in[j]).astype(BF16), modl, nb, pre)
            y3 = y.reshape(nb, rows, OD_W)
            wg, gpar = _gla_params(gla_gate_w[j], gla_gate_b[j], gla_norm_g[j])
            mix_a = _vscan("gla", y3, ((OD_Q, OD_K, OD_V),) * 2, OD_G, gpar, wg)
            xbc = _ssd_conv(y3, ssd_conv_w[j], ssd_conv_b[j])
            sel, spar, spar2 = _ssd_params(ssd_dt_bias[j], ssd_a_log[j], ssd_d[j], ssd_norm_g[j])
            mix_b = _ssd_scan(xbc, y3, sel, spar, spar2)
            w_out = od_w_out[j]
        lg1, lb1 = ln_g[l, 0].reshape(1, D), ln_b[l, 0].reshape(1, D)
        sk = peer_subkeys[l].reshape(2 * PEER_HEADS, PEER_NKEYS, PEER_NKEYS).astype(BF16)
        h1, hm, st = _outproj(h, mix_a.reshape(n, D // 2), mix_b.reshape(n, D // 2), modl,
                              w_out.astype(BF16), lg1, lb1, peer_wq[l].astype(BF16), sk, nb,
                              latent_only=(l == DEPTH - 1))
        stats = _peer_stats(st)
        yp = _peer_experts(hm, st, stats, peer_u[l].astype(BF16), peer_v[l].astype(BF16).T)
        pre = (yp, modl, ln_g[l, 1].reshape(1, D), ln_b[l, 1].reshape(1, D))
        h = h1
    out = _final(h, pre[0], pre[1], pre[2], pre[3], nb)
    return out.reshape(nb, seq, D)
```

```python
import functools

import numpy as np
import jax
import jax.numpy as jnp
from jax import lax
from jax.experimental import pallas as pl
from jax.experimental.pallas import tpu as pltpu

F32 = jnp.float32
BF16 = jnp.bfloat16

D = 1024
DEPTH = 4
N_MOD = 6
CTX = 256
GRID_W = 64
DN_ALPHA = (2 * DEPTH) ** 0.25
NORM_EPS = 1e-6
ROPE_THETA = 10000.0
LB_FLOOR = 1e-30
TM = 256
MOD_ROWS = 24
VMEM_LIMIT = 56 * 1024 * 1024

HEAD_W = 128
VEC_CHUNK = 16
SSD_CHUNK = 128
SSD_DH = 64
GLA_DK = 64
GLA_GATE_NORM = 16.0
PEER_HEADS = 8
PEER_NKEYS = 128
PEER_TOPK = 16
PEER_T = 512
PEER_EB = 2048

EV_Q, EV_I, EV_ZF, EV_ZB, EV_G, EV_AQ, EV_AK, EV_AV = 0, 512, 1024, 1536, 2048, 2560, 3072, 3328
EV_W = 3584
OD_Q, OD_K, OD_V, OD_G, OD_Z, OD_XS, OD_BM, OD_CM, OD_MISC = 0, 512, 1024, 1536, 2048, 2560, 3072, 3328, 3584
OD_W = 3712
MISC_LRF, MISC_LRB, MISC_DTF, MISC_DTB = 0, 16, 32, 40

NT_DIMS = (((1,), (1,)), ((), ()))
TN_DIMS = (((0,), (0,)), ((), ()))


def _cparams(sem):
    return pltpu.CompilerParams(dimension_semantics=sem, vmem_limit_bytes=VMEM_LIMIT)


def _layer_norm(y, g, b):
    mu = jnp.mean(y, axis=-1, keepdims=True)
    yc = y - mu
    var = jnp.mean(yc * yc, axis=-1, keepdims=True)
    return yc * lax.rsqrt(var + NORM_EPS) * g + b


def _rms(y, g):
    return y * lax.rsqrt(jnp.mean(y * y, axis=-1, keepdims=True) + NORM_EPS) * g


def _log_sigmoid(z):
    return jnp.minimum(z, 0.0) - jnp.log(1.0 + jnp.exp(-jnp.abs(z)))


def _silu(z):
    return z * jax.nn.sigmoid(z)


def _split_hi_lo(a):
    hi = a.astype(BF16)
    lo = (a - hi.astype(F32)).astype(BF16)
    return hi, lo


def _mod_body(c_ref, w_ref, b_ref, o_ref):
    a = _silu(c_ref[...])
    a_hi, a_lo = _split_hi_lo(a)
    w_hi, w_lo = _split_hi_lo(w_ref[0])
    acc = jnp.dot(a_hi, w_hi, preferred_element_type=F32)
    acc += jnp.dot(a_lo, w_hi, preferred_element_type=F32)
    acc += jnp.dot(a_hi, w_lo, preferred_element_type=F32)
    o_ref[0] = acc + b_ref[0]


def _modulation(cc, mod_w, mod_b):
    tn = 1536
    nw = mod_w.shape[-1]
    return pl.pallas_call(
        _mod_body,
        grid=(DEPTH, nw // tn),
        in_specs=[
            pl.BlockSpec((MOD_ROWS, D), lambda l, j: (0, 0)),
            pl.BlockSpec((1, D, tn), lambda l, j: (l, 0, j)),
            pl.BlockSpec((1, 1, tn), lambda l, j: (l, 0, j)),
        ],
        out_specs=pl.BlockSpec((1, MOD_ROWS, tn), lambda l, j: (l, 0, j)),
        out_shape=jax.ShapeDtypeStruct((DEPTH, MOD_ROWS, nw), F32),
        compiler_params=_cparams(("parallel", "parallel")),
        name="modulation",
    )(cc, mod_w, mod_b.reshape(DEPTH, 1, nw))


def _mod_spec(k, n_batch, tiles_per_batch):
    def index(i):
        row = jnp.where(i % tiles_per_batch == 0, n_batch, i // tiles_per_batch)
        return (k * MOD_ROWS + row, 0, 0)
    return pl.BlockSpec((1, 1, D), index)


def _inproj_body(has_pre, *refs):
    if has_pre:
        x_ref, y_ref, g2_ref, lng_ref, lnb_ref, sh_ref, sc_ref, w_ref, o_ref, h_ref = refs
        h = _layer_norm(DN_ALPHA * x_ref[...] + g2_ref[0] * y_ref[...], lng_ref[...], lnb_ref[...])
        h_ref[...] = h
    else:
        x_ref, sh_ref, sc_ref, w_ref, o_ref = refs
        h = x_ref[...]
    u = h * (1.0 + sc_ref[0]) + sh_ref[0]
    o_ref[...] = jnp.dot(u.astype(BF16), w_ref[...], preferred_element_type=F32)


def _inproj(h, w, modl, n_batch, pre=None):
    n = h.shape[0]
    nw = w.shape[1]
    tpb = n // n_batch // TM
    tok = pl.BlockSpec((TM, D), lambda i: (i, 0))
    vec = pl.BlockSpec((1, D), lambda i: (0, 0))
    in_specs = [tok]
    args = [h]
    if pre is not None:
        y, prev_modl, ln_g, ln_b = pre
        in_specs += [tok, _mod_spec(5, n_batch, tpb), vec, vec]
        args += [y, prev_modl, ln_g, ln_b]
    in_specs += [_mod_spec(0, n_batch, tpb), _mod_spec(1, n_batch, tpb),
                 pl.BlockSpec((D, nw), lambda i: (0, 0))]
    args += [modl, modl, w]
    out_specs = [pl.BlockSpec((TM, nw), lambda i: (i, 0))]
    out_shape = [jax.ShapeDtypeStruct((n, nw), F32)]
    if pre is not None:
        out_specs.append(tok)
        out_shape.append(jax.ShapeDtypeStruct((n, D), F32))
    res = pl.pallas_call(
        functools.partial(_inproj_body, pre is not None),
        grid=(n // TM,),
        in_specs=in_specs,
        out_specs=out_specs,
        out_shape=out_shape,
        compiler_params=_cparams(("parallel",)),
        name="inproj",
    )(*args)
    return (res[0], res[1]) if pre is not None else (res[0], h)


def _final_body(x_ref, y_ref, g2_ref, lng_ref, lnb_ref, o_ref):
    o_ref[...] = _layer_norm(DN_ALPHA * x_ref[...] + g2_ref[0] * y_ref[...], lng_ref[...], lnb_ref[...])


def _final(h, y, modl, ln_g, ln_b, n_batch):
    n = h.shape[0]
    lat = n // n_batch // TM
    tok = pl.BlockSpec((TM, D), lambda b, j: (b * lat + j, 0))
    vec = pl.BlockSpec((1, D), lambda b, j: (0, 0))
    return pl.pallas_call(
        _final_body,
        grid=(n_batch, lat),
        in_specs=[tok, tok, pl.BlockSpec((1, 1, D), lambda b, j: (5 * MOD_ROWS + b, 0, 0)), vec, vec],
        out_specs=tok,
        out_shape=jax.ShapeDtypeStruct((n, D), F32),
        compiler_params=_cparams(("parallel", "parallel")),
        name="final_norm",
    )(h, y, modl, ln_g, ln_b)


def _vscan_body(mode, nblk, *refs):
    C = VEC_CHUNK
    nh = 4
    if mode == "hgrn":
        qf_ref, vf_ref, zf_ref, qb_ref, vb_ref, zb_ref, g_ref, par_ref, o_ref, st_ref, ob_ref = refs
        dir_refs = ((qf_ref, vf_ref, zf_ref), (qb_ref, vb_ref, zb_ref))
    else:
        (qf_ref, kf_ref, vf_ref, mf_ref, qb_ref, kb_ref, vb_ref, mb_ref, g_ref, wg_ref, par_ref,
         o_ref, st_ref, ob_ref) = refs
        dir_refs = ((qf_ref, kf_ref, vf_ref, mf_ref), (qb_ref, kb_ref, vb_ref, mb_ref))
    rows = o_ref.shape[1]
    rb = qf_ref.shape[1]
    i = pl.program_id(1)
    bi = jnp.where(i == 0, 0, nblk - i)

    @pl.when(i == 0)
    def _():
        st_ref[...] = jnp.zeros(st_ref.shape, F32)

    w = nh * HEAD_W
    ri = lax.broadcasted_iota(jnp.int32, (rb, rb), 0)
    ci = lax.broadcasted_iota(jnp.int32, (rb, rb), 1)
    rid = lax.broadcasted_iota(jnp.int32, (rb, w), 0)
    levels = [C << s for s in range((rb // C).bit_length() - 1)]
    neg = -1e30

    def group_row(x, grp, r):
        g = x.reshape(rb // grp, grp, w)
        return jnp.broadcast_to(g[:, r:r + 1, :], g.shape).reshape(rb, w)

    def nt(a, b):
        return lax.dot_general(a, b, NT_DIMS, preferred_element_type=F32)

    def same_group(grp):
        sh = grp.bit_length() - 1
        return (ri >> sh) == (ci >> sh)

    def direction(d, out_ref, row0):
        refs_d = dir_refs[d]
        q = refs_d[0][0]
        v16 = refs_d[-2 if mode == "gla" else 1][0].astype(BF16)
        if mode == "hgrn":
            z = refs_d[2][0]
            qh = _silu(q) * (HEAD_W ** -0.5)
            la, l1, oml = par_ref[d:d + 1, :], par_ref[2 + d:3 + d, :], par_ref[4 + d:5 + d, :]
            bb = l1 + _log_sigmoid(z)
            log_f = jnp.maximum(la, bb) + jnp.log(1.0 + jnp.exp(-jnp.abs(la - bb)))
            k = oml * jax.nn.sigmoid(-z)
        else:
            qh = q * (GLA_DK ** -0.5)
            k = refs_d[1][0]
            pre = jnp.dot(refs_d[3][0].astype(BF16), wg_ref[d], preferred_element_type=F32)
            log_f = _log_sigmoid(pre + par_ref[d:d + 1, :]) * (1.0 / GLA_GATE_NORM)
        tri = (ri >= ci if d == 0 else ri <= ci).astype(BF16)
        hi, lo = _split_hi_lo(log_f)
        cc = jnp.dot(tri, jnp.concatenate([hi, lo], axis=-1), preferred_element_type=F32)
        cum = cc[:, :w] + cc[:, w:]
        first_row, last_row = (0, rb - 1) if d == 0 else (rb - 1, 0)

        x0 = cum - group_row(cum, C, first_row % C)
        qs = [(qh * jnp.exp(x0)).astype(BF16)]
        ks = [(k * jnp.exp(-x0)).astype(BF16)]
        for m in levels:
            is_query = (rid & m) != 0 if d == 0 else (rid & m) == 0
            bnd = group_row(cum, 2 * m, m - 1 if d == 0 else m)
            qs.append((qh * jnp.exp(jnp.where(is_query, cum - bnd, neg))).astype(BF16))
            ks.append((k * jnp.exp(jnp.where(is_query, neg, bnd - cum))).astype(BF16))
        last = cum[last_row:last_row + 1, :]
        qe = (qh * jnp.exp(cum)).astype(BF16)
        kl = (k * jnp.exp(last - cum)).astype(BF16)
        dec = jnp.exp(last)

        outs = []
        for hh in range(nh):
            sl = slice(hh * HEAD_W, (hh + 1) * HEAD_W)
            p0 = nt(qs[0][:, sl], ks[0][:, sl])
            causal = ri >= ci if d == 0 else ri <= ci
            att = jnp.where(same_group(C), jnp.where(causal, p0, 0.0), 0.0)
            for lv, m in enumerate(levels):
                p = nt(qs[lv + 1][:, sl], ks[lv + 1][:, sl])
                att += p if 2 * m == rb else jnp.where(same_group(2 * m), p, 0.0)
            o = jnp.dot(att.astype(BF16), v16[:, sl], preferred_element_type=F32)
            s_t = st_ref[d * nh + hh]
            o += nt(qe[:, sl], s_t.astype(BF16))
            kv_t = lax.dot_general(v16[:, sl], kl[:, sl], TN_DIMS, preferred_element_type=F32)
            st_ref[d * nh + hh] = s_t * dec[:, sl] + kv_t
            outs.append(o)
        out_ref[row0, :] = jnp.concatenate(outs, axis=-1)

    direction(0, o_ref.at[0], pl.ds(pl.multiple_of(i * rb, rb), rb))
    direction(1, ob_ref, pl.ds(pl.multiple_of(bi * rb, rb), rb))

    @pl.when(i == nblk - 1)
    def _():
        def post(j, carry):
            rs = pl.ds(pl.multiple_of(j * rb, rb), rb)
            for hh in range(nh):
                lanes = pl.ds(hh * HEAD_W, HEAD_W)
                o = o_ref[0, rs, lanes] + ob_ref[rs, lanes]
                g = g_ref[0, rs, lanes]
                if mode == "hgrn":
                    res = _rms(o * jax.nn.sigmoid(g), par_ref[6:7, lanes])
                else:
                    res = _rms(o, par_ref[2:3, lanes]) * _silu(g)
                o_ref[0, rs, lanes] = res
            return carry

        lax.fori_loop(0, rows // rb, post, 0)


def _vscan(mode, y3, col_offsets, g_offset, params, wg=None):
    nb, rows, _ = y3.shape
    width = 4 * HEAD_W
    rb = CTX
    nblk = rows // rb
    fwd = lambda w, off: pl.BlockSpec((1, rb, w), lambda b, i: (b, i, off // w))
    bwd = lambda w, off: pl.BlockSpec((1, rb, w), lambda b, i: (b, jnp.where(i == 0, 0, nblk - i), off // w))
    in_specs, args = [], []
    for mk in (fwd, bwd):
        for off in col_offsets[mk is bwd]:
            in_specs.append(mk(width, off))
            args.append(y3)
        if mode == "gla":
            in_specs.append(mk(HEAD_W, OD_MISC))
            args.append(y3)
    in_specs.append(pl.BlockSpec((1, rows, width), lambda b, i: (b, 0, g_offset // width)))
    args.append(y3)
    if mode == "gla":
        in_specs.append(pl.BlockSpec((2, HEAD_W, width), lambda b, i: (0, 0, 0)))
        args.append(wg)
    in_specs.append(pl.BlockSpec((8, width), lambda b, i: (0, 0)))
    args.append(params)
    return pl.pallas_call(
        functools.partial(_vscan_body, mode, nblk),
        grid=(nb, nblk),
        in_specs=in_specs,
        out_specs=pl.BlockSpec((1, rows, width), lambda b, i: (b, 0, 0)),
        out_shape=jax.ShapeDtypeStruct((nb, rows, width), F32),
        scratch_shapes=[pltpu.VMEM((8, HEAD_W, HEAD_W), F32), pltpu.VMEM((rows, width), F32)],
        compiler_params=_cparams(("parallel", "arbitrary")),
        name="vscan_" + mode,
    )(*args)


def _rope(x, c, sa, sb):
    return x * c + pltpu.roll(x, HEAD_W - 1, 1) * sa + pltpu.roll(x, 1, 1) * sb


def _attn_body(q_ref, k_ref, v_ref, ck_ref, sak_ref, sbk_ref, cq_ref, saq_ref, sbq_ref,
               qg_ref, kg_ref, o_ref, kp_ref, vp_ref):
    qi = pl.program_id(2)
    rows = k_ref.shape[1]
    blk = 256

    @pl.when(qi == 0)
    def _():
        def prep(i, carry):
            rs = pl.ds(pl.multiple_of(i * blk, blk), blk)
            kn = _rms(k_ref[0, rs, :], kg_ref[...])
            kp_ref[rs, :] = _rope(kn, ck_ref[rs, :], sak_ref[rs, :], sbk_ref[rs, :]).astype(BF16)
            vp_ref[rs, :] = v_ref[0, rs, :].astype(BF16)
            return carry
        lax.fori_loop(0, rows // blk, prep, 0)

    def attend(nk):
        outs = []
        for r in range(2):
            qn = _rms(q_ref[0, :, r * HEAD_W:(r + 1) * HEAD_W], qg_ref[...])
            qr = (_rope(qn, cq_ref[...], saq_ref[...], sbq_ref[...]) * (HEAD_W ** -0.5)).astype(BF16)
            s = lax.dot_general(qr, kp_ref[0:nk, :], NT_DIMS, preferred_element_type=F32)
            m = jnp.max(s, axis=-1, keepdims=True)
            p = jnp.exp(s - m)
            l = jnp.sum(p, axis=-1, keepdims=True)
            o = jnp.dot(p.astype(BF16), vp_ref[0:nk, :], preferred_element_type=F32)
            outs.append(o / l)
        o_ref[0] = jnp.concatenate(outs, axis=-1)

    @pl.when(qi == 0)
    def _():
        attend(CTX)

    @pl.when(qi > 0)
    def _():
        attend(rows)


def _attention(y3, tables, q_gain, k_gain):
    nb, rows, _ = y3.shape
    c, sa, sb = tables
    full = pl.BlockSpec((rows, HEAD_W), lambda b, g, i: (0, 0))
    tile = pl.BlockSpec((TM, HEAD_W), lambda b, g, i: (i, 0))
    vec = pl.BlockSpec((1, HEAD_W), lambda b, g, i: (0, 0))
    return pl.pallas_call(
        _attn_body,
        grid=(nb, 2, rows // TM),
        in_specs=[
            pl.BlockSpec((1, TM, 2 * HEAD_W), lambda b, g, i: (b, i, EV_AQ // (2 * HEAD_W) + g)),
            pl.BlockSpec((1, rows, HEAD_W), lambda b, g, i: (b, 0, EV_AK // HEAD_W + g)),
            pl.BlockSpec((1, rows, HEAD_W), lambda b, g, i: (b, 0, EV_AV // HEAD_W + g)),
            full, full, full, tile, tile, tile, vec, vec,
        ],
        out_specs=pl.BlockSpec((1, TM, 2 * HEAD_W), lambda b, g, i: (b, i, g)),
        out_shape=jax.ShapeDtypeStruct((nb, rows, 4 * HEAD_W), F32),
        scratch_shapes=[pltpu.VMEM((rows, HEAD_W), BF16), pltpu.VMEM((rows, HEAD_W), BF16)],
        compiler_params=_cparams(("parallel", "parallel", "arbitrary")),
        name="attention",
    )(y3, y3, y3, c, sa, sb, c, sa, sb, q_gain, k_gain)


def _conv_body(x_ref, w_ref, b_ref, o_ref):
    rows = x_ref.shape[1]
    x = x_ref[0]
    row = lax.broadcasted_iota(jnp.int32, x.shape, 0)
    lo = jnp.where(row < CTX, 0, CTX)
    hi = jnp.where(row < CTX, CTX, rows)
    taps = w_ref.shape[0]
    acc = jnp.zeros(x.shape, F32) + b_ref[...]
    for j in range(taps):
        off = j - taps // 2
        xs = x if off == 0 else pltpu.roll(x, (-off) % rows, 0)
        src = row + off
        xs = jnp.where(src >= lo, jnp.where(src < hi, xs, 0.0), 0.0)
        acc += xs * w_ref[j:j + 1, :]
    o_ref[0] = _silu(acc)


def _ssd_conv(y3, conv_w, conv_b):
    nb, rows, _ = y3.shape
    nc = conv_w.shape[1]
    return pl.pallas_call(
        _conv_body,
        grid=(nb, nc // HEAD_W),
        in_specs=[
            pl.BlockSpec((1, rows, HEAD_W), lambda b, j: (b, 0, OD_XS // HEAD_W + j)),
            pl.BlockSpec((conv_w.shape[0], HEAD_W), lambda b, j: (0, j)),
            pl.BlockSpec((1, HEAD_W), lambda b, j: (0, j)),
        ],
        out_specs=pl.BlockSpec((1, rows, HEAD_W), lambda b, j: (b, 0, j)),
        out_shape=jax.ShapeDtypeStruct((nb, rows, nc), F32),
        compiler_params=_cparams(("parallel", "parallel")),
        name="ssd_conv",
    )(y3, conv_w, conv_b.reshape(1, nc))


def _ssd_body(n_ctx_chunks, n_chunks, x_ref, bm_ref, cm_ref, z_ref, misc_ref, sel_ref, par_ref,
              par2_ref, o_ref, st_ref, ob_ref):
    C = SSD_CHUNK
    P = SSD_DH
    rows = o_ref.shape[1]
    w = 4 * HEAD_W
    st_ref[...] = jnp.zeros(st_ref.shape, F32)
    ri = lax.broadcasted_iota(jnp.int32, (C, C), 0)
    ci = lax.broadcasted_iota(jnp.int32, (C, C), 1)
    masks = (ri >= ci, ri <= ci)
    tri = tuple(mk.astype(BF16) for mk in masks)

    def narrow(a):
        return jnp.concatenate([a[:, hh * HEAD_W:hh * HEAD_W + P] for hh in range(4)], axis=-1)

    def chunk(d, r0, out_ref):
        rs = pl.ds(r0, C)
        x = x_ref[0, rs, :]
        bmat = bm_ref[0, rs, :]
        cb16 = cm_ref[0, rs, :].astype(BF16)
        b16 = bmat.astype(BF16)
        bt16 = bmat.T.astype(BF16)
        scores = lax.dot_general(cb16, b16, NT_DIMS, preferred_element_type=F32)
        mh, ml = _split_hi_lo(misc_ref[0, rs, :])
        dd = jnp.dot(jnp.concatenate([mh, ml], axis=0), sel_ref[0, d], preferred_element_type=F32)
        dt = dd[:C] + dd[C:] + par_ref[0, d:d + 1, :]
        dt = jnp.maximum(dt, 0.0) + jnp.log(1.0 + jnp.exp(-jnp.abs(dt)))
        la = dt * par_ref[0, 2 + d:3 + d, :]
        lh, ll = _split_hi_lo(la)
        cc = jnp.dot(tri[d], jnp.concatenate([lh, ll], axis=-1), preferred_element_type=F32)
        cum = cc[:, :w] + cc[:, w:]
        last = cum[C - 1:C, :] if d == 0 else cum[0:1, :]
        xdt = x * narrow(dt)
        s_t = st_ref[d]
        y_in = jnp.dot(cb16, s_t.astype(BF16), preferred_element_type=F32) * narrow(jnp.exp(cum))
        xw = (xdt * narrow(jnp.exp(last - cum))).astype(BF16)
        st_ref[d] = s_t * narrow(jnp.exp(last)) + jnp.dot(bt16, xw, preferred_element_type=F32)
        xdt16 = xdt.astype(BF16)
        ys = []
        for hh in range(4):
            cum_h = cum[:, hh * HEAD_W:(hh + 1) * HEAD_W]
            seg = jnp.where(masks[d], cum_h - cum_h.T, 0.0)
            dec = jnp.where(masks[d], jnp.exp(seg), 0.0)
            ys.append(jnp.dot((scores * dec).astype(BF16), xdt16[:, hh * P:(hh + 1) * P],
                              preferred_element_type=F32))
        out_ref[rs, :] = jnp.concatenate(ys, axis=-1) + y_in

    def step(t, carry):
        cf = t
        cb = jnp.where(t < n_ctx_chunks, n_ctx_chunks - 1 - t, n_chunks + n_ctx_chunks - 1 - t)
        chunk(0, pl.multiple_of(cf * C, C), o_ref.at[0])
        chunk(1, pl.multiple_of(cb * C, C), ob_ref)
        return carry

    lax.fori_loop(0, n_chunks, step, 0)

    blk = 256

    def post(i, carry):
        rs = pl.ds(pl.multiple_of(i * blk, blk), blk)
        y = o_ref[0, rs, :] + ob_ref[rs, :] + par2_ref[0, 0:1, :] * x_ref[0, rs, :]
        o_ref[0, rs, :] = _rms(y * _silu(z_ref[0, rs, :]), par2_ref[0, 1:2, :])
        return carry

    lax.fori_loop(0, rows // blk, post, 0)


def _ssd_scan(xbc, y3, sel, par, par2):
    nb, rows, _ = xbc.shape
    gw = 4 * SSD_DH
    return pl.pallas_call(
        functools.partial(_ssd_body, CTX // SSD_CHUNK, rows // SSD_CHUNK),
        grid=(nb, 2),
        in_specs=[
            pl.BlockSpec((1, rows, gw), lambda b, g: (b, 0, g)),
            pl.BlockSpec((1, rows, HEAD_W), lambda b, g: (b, 0, 2 * gw // HEAD_W + g)),
            pl.BlockSpec((1, rows, HEAD_W), lambda b, g: (b, 0, 2 * gw // HEAD_W + 2 + g)),
            pl.BlockSpec((1, rows, gw), lambda b, g: (b, 0, OD_Z // gw + g)),
            pl.BlockSpec((1, rows, HEAD_W), lambda b, g: (b, 0, OD_MISC // HEAD_W)),
            pl.BlockSpec((1, 2, HEAD_W, 4 * HEAD_W), lambda b, g: (g, 0, 0, 0)),
            pl.BlockSpec((1, 8, 4 * HEAD_W), lambda b, g: (g, 0, 0)),
            pl.BlockSpec((1, 8, gw), lambda b, g: (g, 0, 0)),
        ],
        out_specs=pl.BlockSpec((1, rows, gw), lambda b, g: (b, 0, g)),
        out_shape=jax.ShapeDtypeStruct((nb, rows, 2 * gw), F32),
        scratch_shapes=[pltpu.VMEM((2, HEAD_W, gw), F32), pltpu.VMEM((rows, gw), F32)],
        compiler_params=_cparams(("parallel", "parallel")),
        name="ssd_scan",
    )(xbc, xbc, xbc, y3, y3, sel, par, par2)


def _outproj_body(h_ref, a_ref, b_ref, g1_ref, sh2_ref, sc2_ref, wo_ref, lng_ref, lnb_ref, wq_ref,
                  sk_ref, h1_ref, hm_ref, st_ref):
    mix = jnp.concatenate([a_ref[...], b_ref[...]], axis=-1).astype(BF16)
    o = jnp.dot(mix, wo_ref[...], preferred_element_type=F32)
    h1 = _layer_norm(DN_ALPHA * h_ref[...] + g1_ref[0] * o, lng_ref[...], lnb_ref[...])
    h1_ref[...] = h1
    hm = (h1 * (1.0 + sc2_ref[0]) + sh2_ref[0]).astype(BF16)
    hm_ref[...] = hm
    qry = jnp.dot(hm, wq_ref[...], preferred_element_type=F32).astype(BF16)
    for j in range(2 * PEER_HEADS):
        qj = qry[:, j * PEER_NKEYS:(j + 1) * PEER_NKEYS]
        st_ref[j * PEER_NKEYS:(j + 1) * PEER_NKEYS, :] = lax.dot_general(
            sk_ref[j], qj, NT_DIMS, preferred_element_type=F32)


def _outproj(h, mix_a, mix_b, modl, w_out, ln_g, ln_b, wq, sk, n_batch, latent_only=False):
    n = h.shape[0]
    tpb = n // n_batch // TM
    if latent_only:
        lat = tpb - 1
        n_out = n_batch * lat * TM
        src = lambda j: (j // lat) * tpb + 1 + j % lat
        mod = lambda k: pl.BlockSpec((1, 1, D), lambda j: (k * MOD_ROWS + j // lat, 0, 0))
    else:
        n_out = n
        src = lambda j: j
        mod = lambda k: _mod_spec(k, n_batch, tpb)
    tok_in = pl.BlockSpec((TM, D), lambda j: (src(j), 0))
    half = pl.BlockSpec((TM, D // 2), lambda j: (src(j), 0))
    tok = pl.BlockSpec((TM, D), lambda j: (j, 0))
    vec = pl.BlockSpec((1, D), lambda j: (0, 0))
    nq = wq.shape[1]
    return pl.pallas_call(
        _outproj_body,
        grid=(n_out // TM,),
        in_specs=[tok_in, half, half, mod(2), mod(3), mod(4), pl.BlockSpec((D, D), lambda j: (0, 0)), vec, vec,
                  pl.BlockSpec((D, nq), lambda j: (0, 0)),
                  pl.BlockSpec(sk.shape, lambda j: (0, 0, 0))],
        out_specs=[tok, tok, pl.BlockSpec((nq, TM), lambda j: (0, j))],
        out_shape=[jax.ShapeDtypeStruct((n_out, D), F32), jax.ShapeDtypeStruct((n_out, D), BF16),
                   jax.ShapeDtypeStruct((nq, n_out), F32)],
        compiler_params=_cparams(("parallel",)),
        name="outproj",
    )(h, mix_a, mix_b, modl, modl, modl, w_out, ln_g, ln_b, wq, sk)


def _sort16_network():
    comps = []

    def merge(lo, hi, r):
        step = r * 2
        if step < hi - lo:
            merge(lo, hi, step)
            merge(lo + r, hi, step)
            comps.extend((j, j + r) for j in range(lo + r, hi - r, step))
        else:
            comps.append((lo, lo + r))

    def sort(lo, hi):
        if hi - lo >= 1:
            mid = lo + (hi - lo) // 2
            sort(lo, mid)
            sort(mid + 1, hi)
            merge(lo, hi, 1)

    sort(0, 15)
    return comps


SORT16 = _sort16_network()
BITONIC16 = [(j, j + s) for s in (8, 4, 2, 1) for j in range(16) if not j & s]


def _compare_exchange(v, comps):
    v = list(v)
    for a, b in comps:
        v[a], v[b] = jnp.maximum(v[a], v[b]), jnp.minimum(v[a], v[b])
    return v


def _merge_top16(a, b_rev):
    c = [a[k] if b_rev[k] is None else jnp.maximum(a[k], b_rev[k]) for k in range(16)]
    return _compare_exchange(c, BITONIC16)


def _top16_keys(ref, row0, col):
    v = [ref[row0 + 8 * j:row0 + 8 * j + 8, col] for j in range(16)]
    v = _compare_exchange(v, SORT16)
    for shift in (4, 2, 1):
        rolled = [pltpu.roll(x, shift, 0) for x in v]
        v = _merge_top16(v, rolled[::-1])
    return v


def _stats_body(st_ref, o_ref):
    k = PEER_TOPK
    nk = PEER_NKEYS
    sub = lax.broadcasted_iota(jnp.int32, (8, 128), 0)

    def column(tc, carry):
        col = pl.ds(pl.multiple_of(tc * 128, 128), 128)
        tops = []
        for side in range(2):
            packed = None
            for h in range(PEER_HEADS):
                a = _top16_keys(st_ref, (2 * h + side) * nk, col)
                packed = a if h == 0 else [jnp.where(sub == h, a[r], packed[r]) for r in range(k)]
            tops.append(packed)
        t1, t2 = tops
        best = [t1[0] + t2[r] for r in range(k)]
        for r1 in range(1, k):
            ln = k // (r1 + 1)
            lst = [t1[r1] + t2[r] for r in range(ln)]
            best = _merge_top16(best, [None] * (k - ln) + lst[::-1])
        z = 1.0
        for r in range(1, k):
            z = z + jnp.exp(best[r] - best[0])
        o_ref[0, :, col] = t1[0]
        o_ref[1, :, col] = t2[0] + jnp.log(z)
        o_ref[2, :, col] = best[k - 1]
        for r in range(k):
            o_ref[3 + r, :, col] = t2[r]
        return carry

    lax.fori_loop(0, o_ref.shape[2] // 128, column, 0)


def _peer_stats(st):
    n = st.shape[1]
    ts = 512
    return pl.pallas_call(
        _stats_body,
        grid=(n // ts,),
        in_specs=[pl.BlockSpec((st.shape[0], ts), lambda i: (0, i))],
        out_specs=pl.BlockSpec((3 + PEER_TOPK, PEER_HEADS, ts), lambda i: (0, 0, i)),
        out_shape=jax.ShapeDtypeStruct((3 + PEER_TOPK, PEER_HEADS, n), F32),
        compiler_params=_cparams(("parallel",)),
        name="peer_stats",
    )(st)


def _count_prefix(pred, rows):
    pick = lambda a, b, c: jnp.where(c, b, a)
    c8 = pred(rows[7])
    c4 = pred(pick(rows[3], rows[11], c8))
    c2 = pred(pick(pick(rows[1], rows[5], c4), pick(rows[9], rows[13], c4), c8))
    ev = [rows[2 * j] for j in range(8)]
    c1 = pred(pick(pick(pick(ev[0], ev[1], c2), pick(ev[2], ev[3], c2), c4),
                   pick(pick(ev[4], ev[5], c2), pick(ev[6], ev[7], c2), c4), c8))
    c16 = pred(rows[15])
    val = lambda c, v: jnp.where(c, v, 0.0)
    return val(c8, 8.0) + val(c4, 4.0) + val(c2, 2.0) + val(c1, 1.0) + val(c16, 1.0)


def _expert_body(hm_ref, st_ref, stats_ref, u_ref, vt_ref, y_ref, e1_ref, n1_ref, e2_ref, r2_ref,
                 a_ref, p_ref, acc_ref, hmt_ref):
    e = pl.program_id(1)
    ne = pl.num_programs(1)
    t = hm_ref.shape[0]
    nk = PEER_NKEYS
    k = PEER_TOPK
    m = PEER_EB // nk
    ntc = t // 128

    @pl.when(e == 0)
    def _():
        def prep(tc, carry):
            col = pl.ds(pl.multiple_of(tc * 128, 128), 128)
            for h in range(PEER_HEADS):
                rows = slice(h * nk, (h + 1) * nk)
                s1 = st_ref[2 * h * nk:(2 * h + 1) * nk, col]
                s2 = st_ref[(2 * h + 1) * nk:(2 * h + 2) * nk, col]
                tau = stats_ref[2, h:h + 1, col]
                t2 = [stats_ref[3 + r, h:h + 1, col] for r in range(k)]
                n1 = _count_prefix(lambda thr: s1 + thr >= tau, t2)
                r2 = _count_prefix(lambda thr: thr > s2, t2)
                e1_ref[rows, col] = jnp.exp(s1 - stats_ref[0, h:h + 1, col])
                n1_ref[rows, col] = n1
                e2_ref[rows, col] = jnp.exp(s2 - stats_ref[1, h:h + 1, col]).astype(BF16)
                r2_ref[rows, col] = r2.astype(BF16)
            return carry

        lax.fori_loop(0, ntc, prep, 0)
        acc_ref[...] = jnp.zeros(acc_ref.shape, F32)
        hmt_ref[...] = hm_ref[...].astype(F32).T.astype(BF16)

    a_ref[...] = jnp.dot(u_ref[...], hmt_ref[...], preferred_element_type=F32)

    def build(tc, carry):
        col = pl.ds(pl.multiple_of(tc * 128, 128), 128)
        own = [pl.ds(pl.multiple_of(h * nk + e * m, m), m) for h in range(PEER_HEADS)]
        e1 = [e1_ref[own[h], col].astype(BF16) for h in range(PEER_HEADS)]
        n1 = [n1_ref[own[h], col].astype(BF16) for h in range(PEER_HEADS)]
        zero = jnp.zeros((), BF16)
        group = 4
        for g0 in range(0, m, group):
            gates = [jnp.zeros((nk, 128), BF16) for _ in range(group)]
            for h in range(PEER_HEADS):
                rows2 = slice(h * nk, (h + 1) * nk)
                r2 = r2_ref[rows2, col]
                e2 = e2_ref[rows2, col]
                for j in range(group):
                    n1b = jnp.broadcast_to(n1[h][g0 + j:g0 + j + 1, :], (nk, 128))
                    e1b = jnp.broadcast_to(e1[h][g0 + j:g0 + j + 1, :], (nk, 128))
                    gates[j] = gates[j] + e1b * jnp.where(r2 < n1b, e2, zero)
            for j in range(group):
                rows = slice((g0 + j) * nk, (g0 + j + 1) * nk)
                act = jax.nn.gelu(a_ref[rows, col], approximate=True)
                p_ref[rows, col] = act.astype(BF16) * gates[j]
        return carry

    lax.fori_loop(0, ntc, build, 0)
    acc_ref[...] += jnp.dot(vt_ref[...], p_ref[...], preferred_element_type=F32)

    @pl.when(e == ne - 1)
    def _():
        y_ref[...] = acc_ref[...].T


def _peer_experts(hm, st, stats, u, vt):
    n = hm.shape[0]
    ne = u.shape[0] // PEER_EB
    assert (PEER_EB // PEER_NKEYS) % 8 == 0
    t = PEER_T
    half = (PEER_HEADS * PEER_NKEYS, t)
    return pl.pallas_call(
        _expert_body,
        grid=(n // t, ne),
        in_specs=[
            pl.BlockSpec((t, D), lambda i, e: (i, 0)),
            pl.BlockSpec((st.shape[0], t), lambda i, e: (0, i)),
            pl.BlockSpec((stats.shape[0], PEER_HEADS, t), lambda i, e: (0, 0, i)),
            pl.BlockSpec((PEER_EB, D), lambda i, e: (e, 0)),
            pl.BlockSpec((D, PEER_EB), lambda i, e: (0, e)),
        ],
        out_specs=pl.BlockSpec((t, D), lambda i, e: (i, 0)),
        out_shape=jax.ShapeDtypeStruct((n, D), F32),
        scratch_shapes=[pltpu.VMEM(half, F32), pltpu.VMEM(half, F32), pltpu.VMEM(half, BF16),
                        pltpu.VMEM(half, BF16), pltpu.VMEM((PEER_EB, t), F32),
                        pltpu.VMEM((PEER_EB, t), BF16), pltpu.VMEM((D, t), F32),
                        pltpu.VMEM((D, t), BF16)],
        compiler_params=_cparams(("parallel", "arbitrary")),
        name="peer_experts",
    )(hm, st, stats, u, vt)


def _rope_tables(rows_lat):
    pairs = HEAD_W // 4
    t = jnp.arange(rows_lat)
    inv = ROPE_THETA ** (-jnp.arange(pairs, dtype=F32) / pairs)
    ang = jnp.concatenate([(t // GRID_W).astype(F32)[:, None] * inv,
                           (t % GRID_W).astype(F32)[:, None] * inv], axis=-1)
    cos, sin = jnp.cos(ang), jnp.sin(ang)
    zero = jnp.zeros_like(sin)
    c = jnp.stack([cos, cos], -1).reshape(rows_lat, HEAD_W)
    sa = jnp.stack([-sin, zero], -1).reshape(rows_lat, HEAD_W)
    sb = jnp.stack([zero, sin], -1).reshape(rows_lat, HEAD_W)
    ident = jnp.ones((CTX, HEAD_W), F32)
    none = jnp.zeros((CTX, HEAD_W), F32)
    return (jnp.concatenate([ident, c]), jnp.concatenate([none, sa]), jnp.concatenate([none, sb]))


def _odd_column_map():
    src = -np.ones((OD_W,), np.int64)
    for h in range(4):
        src[OD_Q + h * HEAD_W:OD_Q + h * HEAD_W + GLA_DK] = np.arange(h * GLA_DK, (h + 1) * GLA_DK)
        src[OD_K + h * HEAD_W:OD_K + h * HEAD_W + GLA_DK] = 256 + np.arange(h * GLA_DK, (h + 1) * GLA_DK)
    src[OD_V:OD_V + 512] = 512 + np.arange(512)
    src[OD_G:OD_G + 512] = 1024 + np.arange(512)
    src[OD_Z:OD_Z + 512] = 1568 + np.arange(512)
    src[OD_XS:OD_XS + 512] = 2080 + np.arange(512)
    src[OD_BM:OD_BM + 256] = 2592 + np.arange(256)
    src[OD_CM:OD_CM + 256] = 2848 + np.arange(256)
    src[OD_MISC + MISC_LRF:OD_MISC + MISC_LRF + 16] = 1536 + np.arange(16)
    src[OD_MISC + MISC_LRB:OD_MISC + MISC_LRB + 16] = 1552 + np.arange(16)
    src[OD_MISC + MISC_DTF:OD_MISC + MISC_DTF + 8] = 3104 + np.arange(8)
    src[OD_MISC + MISC_DTB:OD_MISC + MISC_DTB + 8] = 3112 + np.arange(8)
    return src


def _relayout_odd_w(w):
    src = _odd_column_map()
    cols = jnp.take(w, jnp.asarray(np.maximum(src, 0)), axis=1)
    return jnp.where(jnp.asarray(src >= 0)[None, :], cols, 0.0)


def _pad_rows(rows_list, width):
    out = [jnp.broadcast_to(jnp.asarray(r, F32).reshape(1, width), (1, width)) for r in rows_list]
    out += [jnp.zeros((1, width), F32)] * (8 - len(out))
    return jnp.concatenate(out, axis=0)


def _hgrn_params(lb, gain):
    la = jnp.log(jnp.maximum(lb, LB_FLOOR))
    l1 = jnp.log1p(-lb)
    oml = 1.0 - lb
    return _pad_rows([la[0], la[1], l1[0], l1[1], oml[0], oml[1], jnp.tile(gain, 4)], 4 * HEAD_W)


def _gla_params(gate_w, gate_b, gain):
    wg = jnp.zeros((2, HEAD_W, 4 * HEAD_W), F32)
    gb = jnp.zeros((2, 4 * HEAD_W), F32)
    for h in range(4):
        s = slice(h * GLA_DK, (h + 1) * GLA_DK)
        dst = slice(h * HEAD_W, h * HEAD_W + GLA_DK)
        for d, off in enumerate((MISC_LRF, MISC_LRB)):
            wg = wg.at[d, off:off + 16, dst].set(gate_w[d][:, s])
            gb = gb.at[d, dst].set(gate_b[d, s])
    return wg.astype(BF16), _pad_rows([gb[0], gb[1], jnp.tile(gain, 4)], 4 * HEAD_W)


def _ssd_params(dt_bias, a_log, d_skip, norm_g):
    sel = np.zeros((2, 2, HEAD_W, 4 * HEAD_W), np.float32)
    for g in range(2):
        for d, off in enumerate((MISC_DTF, MISC_DTB)):
            for hh in range(4):
                sel[g, d, off + 4 * g + hh, hh * HEAD_W:(hh + 1) * HEAD_W] = 1.0
    par, par2 = [], []
    neg_a = -jnp.exp(a_log.astype(F32))
    for g in range(2):
        hs = slice(4 * g, 4 * g + 4)
        rep = lambda v: jnp.repeat(v[hs], HEAD_W)
        par.append(_pad_rows([rep(dt_bias[0]), rep(dt_bias[1]), rep(neg_a[0]), rep(neg_a[1])], 4 * HEAD_W))
        par2.append(_pad_rows([jnp.repeat(d_skip[hs], SSD_DH), norm_g[g * 256:(g + 1) * 256]], 4 * SSD_DH))
    return jnp.asarray(sel, BF16), jnp.stack(par), jnp.stack(par2)


def kernel(x, c, ctx, c_ctx, mod_w, mod_b, ln_g, ln_b, ev_w_in, ev_w_out, hg_lb_logits, hg_norm_g, at_q_norm_g, at_k_norm_g, od_w_in, od_w_out, gla_gate_w, gla_gate_b, gla_norm_g, ssd_conv_w, ssd_conv_b, ssd_dt_bias, ssd_a_log, ssd_d, ssd_norm_g, peer_wq, peer_subkeys, peer_u, peer_v):
    nb, seq, _ = x.shape
    assert ctx.shape[1] == CTX and seq % TM == 0 and nb + 1 <= MOD_ROWS
    rows = CTX + seq
    n = nb * rows

    cc = jnp.concatenate([c, c_ctx[None, :], jnp.zeros((MOD_ROWS - nb - 1, D), F32)], axis=0)
    mods = _modulation(cc, mod_w, mod_b)
    mods = mods.reshape(DEPTH, MOD_ROWS, N_MOD, D).transpose(0, 2, 1, 3).reshape(DEPTH, N_MOD * MOD_ROWS, 1, D)

    sm = jax.nn.softmax(hg_lb_logits.astype(F32), axis=1)
    hg_lb = jnp.cumsum(sm, axis=1) - sm[:, :1]
    tables = _rope_tables(seq)

    h = jnp.concatenate([ctx, x], axis=1).reshape(n, D)
    pre = None
    for l in range(DEPTH):
        j = l // 2
        modl = mods[l]
        if l % 2 == 0:
            y, h = _inproj(h, ev_w_in[j].astype(BF16), modl, nb, pre)
            y3 = y.reshape(nb, rows, EV_W)
            mix_a = _vscan("hgrn", y3, ((EV_Q, EV_I, EV_ZF), (EV_Q, EV_I, EV_ZB)), EV_G,
                           _hgrn_params(hg_lb[:, j], hg_norm_g[j]))
            mix_b = _attention(y3, tables, at_q_norm_g[j].reshape(1, HEAD_W), at_k_norm_g[j].reshape(1, HEAD_W))
            w_out = ev_w_out[j]
        else:
            y, h = _inproj(h, _relayout_odd_w(od_w_in[j]).astype(BF16), modl, nb, pre)
            y3 = y.reshape(nb, rows, OD_W)
            wg, gpar = _gla_params(gla_gate_w[j], gla_gate_b[j], gla_norm_g[j])
            mix_a = _vscan("gla", y3, ((OD_Q, OD_K, OD_V),) * 2, OD_G, gpar, wg)
            xbc = _ssd_conv(y3, ssd_conv_w[j], ssd_conv_b[j])
            sel, spar, spar2 = _ssd_params(ssd_dt_bias[j], ssd_a_log[j], ssd_d[j], ssd_norm_g[j])
            mix_b = _ssd_scan(xbc, y3, sel, spar, spar2)
            w_out = od_w_out[j]
        lg1, lb1 = ln_g[l, 0].reshape(1, D), ln_b[l, 0].reshape(1, D)
        sk = peer_subkeys[l].reshape(2 * PEER_HEADS, PEER_NKEYS, PEER_NKEYS).astype(BF16)
        h1, hm, st = _outproj(h, mix_a.reshape(n, D // 2), mix_b.reshape(n, D // 2), modl,
                              w_out.astype(BF16), lg1, lb1, peer_wq[l].astype(BF16), sk, nb,
                              latent_only=(l == DEPTH - 1))
        stats = _peer_stats(st)
        yp = _peer_experts(hm, st, stats, peer_u[l].astype(BF16), peer_v[l].astype(BF16).T)
        pre = (yp, modl, ln_g[l, 1].reshape(1, D), ln_b[l, 1].reshape(1, D))
        h = h1
    out = _final(h, pre[0], pre[1], pre[2], pre[3], nb)
    return out.reshape(nb, seq, D)
```

```python
import functools

import numpy as np
import jax
import jax.numpy as jnp
from jax import lax
from jax.experimental import pallas as pl
from jax.experimental.pallas import tpu as pltpu

F32 = jnp.float32
BF16 = jnp.bfloat16

D = 1024
DEPTH = 4
N_MOD = 6
CTX = 256
GRID_W = 64
DN_ALPHA = (2 * DEPTH) ** 0.25
NORM_EPS = 1e-6
ROPE_THETA = 10000.0
LB_FLOOR = 1e-30
TM = 256
MOD_ROWS = 24
VMEM_LIMIT = 56 * 1024 * 1024

HEAD_W = 128
VEC_CHUNK = 16
SSD_CHUNK = 128
SSD_DH = 64
GLA_DK = 64
GLA_GATE_NORM = 16.0
PEER_HEADS = 8
PEER_NKEYS = 128
PEER_TOPK = 16
PEER_T = 512
PEER_EB = 2048

EV_Q, EV_I, EV_ZF, EV_ZB, EV_G, EV_AQ, EV_AK, EV_AV = 0, 512, 1024, 1536, 2048, 2560, 3072, 3328
EV_W = 3584
OD_Q, OD_K, OD_V, OD_G, OD_Z, OD_XS, OD_BM, OD_CM, OD_MISC = 0, 512, 1024, 1536, 2048, 2560, 3072, 3328, 3584
OD_W = 3712
MISC_LRF, MISC_LRB, MISC_DTF, MISC_DTB = 0, 16, 32, 40

NT_DIMS = (((1,), (1,)), ((), ()))
TN_DIMS = (((0,), (0,)), ((), ()))


def _cparams(sem):
    return pltpu.CompilerParams(dimension_semantics=sem, vmem_limit_bytes=VMEM_LIMIT)


def _layer_norm(y, g, b):
    mu = jnp.mean(y, axis=-1, keepdims=True)
    yc = y - mu
    var = jnp.mean(yc * yc, axis=-1, keepdims=True)
    return yc * lax.rsqrt(var + NORM_EPS) * g + b


def _rms(y, g):
    return y * lax.rsqrt(jnp.mean(y * y, axis=-1, keepdims=True) + NORM_EPS) * g


def _log_sigmoid(z):
    return jnp.minimum(z, 0.0) - jnp.log(1.0 + jnp.exp(-jnp.abs(z)))


def _silu(z):
    return z * jax.nn.sigmoid(z)


def _split_hi_lo(a):
    hi = a.astype(BF16)
    lo = (a - hi.astype(F32)).astype(BF16)
    return hi, lo


def _mod_body(c_ref, w_ref, b_ref, o_ref):
    a = _silu(c_ref[...])
    a_hi, a_lo = _split_hi_lo(a)
    w_hi, w_lo = _split_hi_lo(w_ref[0])
    acc = jnp.dot(a_hi, w_hi, preferred_element_type=F32)
    acc += jnp.dot(a_lo, w_hi, preferred_element_type=F32)
    acc += jnp.dot(a_hi, w_lo, preferred_element_type=F32)
    o_ref[0] = acc + b_ref[0]


def _modulation(cc, mod_w, mod_b):
    tn = 1536
    nw = mod_w.shape[-1]
    return pl.pallas_call(
        _mod_body,
        grid=(DEPTH, nw // tn),
        in_specs=[
            pl.BlockSpec((MOD_ROWS, D), lambda l, j: (0, 0)),
            pl.BlockSpec((1, D, tn), lambda l, j: (l, 0, j)),
            pl.BlockSpec((1, 1, tn), lambda l, j: (l, 0, j)),
        ],
        out_specs=pl.BlockSpec((1, MOD_ROWS, tn), lambda l, j: (l, 0, j)),
        out_shape=jax.ShapeDtypeStruct((DEPTH, MOD_ROWS, nw), F32),
        compiler_params=_cparams(("parallel", "parallel")),
        name="modulation",
    )(cc, mod_w, mod_b.reshape(DEPTH, 1, nw))


def _mod_spec(k, n_batch, tiles_per_batch):
    def index(i):
        row = jnp.where(i % tiles_per_batch == 0, n_batch, i // tiles_per_batch)
        return (k * MOD_ROWS + row, 0, 0)
    return pl.BlockSpec((1, 1, D), index)


def _inproj_body(has_pre, *refs):
    if has_pre:
        x_ref, y_ref, g2_ref, lng_ref, lnb_ref, sh_ref, sc_ref, w_ref, o_ref, h_ref = refs
        h = _layer_norm(DN_ALPHA * x_ref[...] + g2_ref[0] * y_ref[...], lng_ref[...], lnb_ref[...])
        h_ref[...] = h
    else:
        x_ref, sh_ref, sc_ref, w_ref, o_ref = refs
        h = x_ref[...]
    u = h * (1.0 + sc_ref[0]) + sh_ref[0]
    o_ref[...] = jnp.dot(u.astype(BF16), w_ref[...], preferred_element_type=F32)


def _inproj(h, w, modl, n_batch, pre=None):
    n = h.shape[0]
    nw = w.shape[1]
    tpb = n // n_batch // TM
    tok = pl.BlockSpec((TM, D), lambda i: (i, 0))
    vec = pl.BlockSpec((1, D), lambda i: (0, 0))
    in_specs = [tok]
    args = [h]
    if pre is not None:
        y, prev_modl, ln_g, ln_b = pre
        in_specs += [tok, _mod_spec(5, n_batch, tpb), vec, vec]
        args += [y, prev_modl, ln_g, ln_b]
    in_specs += [_mod_spec(0, n_batch, tpb), _mod_spec(1, n_batch, tpb),
                 pl.BlockSpec((D, nw), lambda i: (0, 0))]
    args += [modl, modl, w]
    out_specs = [pl.BlockSpec((TM, nw), lambda i: (i, 0))]
    out_shape = [jax.ShapeDtypeStruct((n, nw), F32)]
    if pre is not None:
        out_specs.append(tok)
        out_shape.append(jax.ShapeDtypeStruct((n, D), F32))
    res = pl.pallas_call(
        functools.partial(_inproj_body, pre is not None),
        grid=(n // TM,),
        in_specs=in_specs,
        out_specs=out_specs,
        out_shape=out_shape,
        compiler_params=_cparams(("parallel",)),
        name="inproj",
    )(*args)
    return (res[0], res[1]) if pre is not None else (res[0], h)


def _final_body(x_ref, y_ref, g2_ref, lng_ref, lnb_ref, o_ref):
    o_ref[...] = _layer_norm(DN_ALPHA * x_ref[...] + g2_ref[0] * y_ref[...], lng_ref[...], lnb_ref[...])


def _final(h, y, modl, ln_g, ln_b, n_batch):
    n = h.shape[0]
    lat = n // n_batch // TM
    tok = pl.BlockSpec((TM, D), lambda b, j: (b * lat + j, 0))
    vec = pl.BlockSpec((1, D), lambda b, j: (0, 0))
    return pl.pallas_call(
        _final_body,
        grid=(n_batch, lat),
        in_specs=[tok, tok, pl.BlockSpec((1, 1, D), lambda b, j: (5 * MOD_ROWS + b, 0, 0)), vec, vec],
        out_specs=tok,
        out_shape=jax.ShapeDtypeStruct((n, D), F32),
        compiler_params=_cparams(("parallel", "parallel")),
        name="final_norm",
    )(h, y, modl, ln_g, ln_b)


def _vscan_body(mode, nblk, *refs):
    C = VEC_CHUNK
    nh = 4
    if mode == "hgrn":
        qf_ref, vf_ref, zf_ref, qb_ref, vb_ref, zb_ref, g_ref, par_ref, o_ref, st_ref, ob_ref = refs
        dir_refs = ((qf_ref, vf_ref, zf_ref), (qb_ref, vb_ref, zb_ref))
    else:
        (qf_ref, kf_ref, vf_ref, mf_ref, qb_ref, kb_ref, vb_ref, mb_ref, g_ref, wg_ref, par_ref,
         o_ref, st_ref, ob_ref) = refs
        dir_refs = ((qf_ref, kf_ref, vf_ref, mf_ref), (qb_ref, kb_ref, vb_ref, mb_ref))
    rows = o_ref.shape[1]
    rb = qf_ref.shape[1]
    i = pl.program_id(1)
    bi = jnp.where(i == 0, 0, nblk - i)

    @pl.when(i == 0)
    def _():
        st_ref[...] = jnp.zeros(st_ref.shape, F32)

    w = nh * HEAD_W
    ri = lax.broadcasted_iota(jnp.int32, (rb, rb), 0)
    ci = lax.broadcasted_iota(jnp.int32, (rb, rb), 1)
    rid = lax.broadcasted_iota(jnp.int32, (rb, w), 0)
    levels = [C << s for s in range((rb // C).bit_length() - 1)]
    neg = -1e30

    def group_row(x, grp, r):
        g = x.reshape(rb // grp, grp, w)
        return jnp.broadcast_to(g[:, r:r + 1, :], g.shape).reshape(rb, w)

    def nt(a, b):
        return lax.dot_general(a, b, NT_DIMS, preferred_element_type=F32)

    def same_group(grp):
        sh = grp.bit_length() - 1
        return (ri >> sh) == (ci >> sh)

    def direction(d, out_ref, row0):
        refs_d = dir_refs[d]
        q = refs_d[0][0]
        v16 = refs_d[-2 if mode == "gla" else 1][0].astype(BF16)
        if mode == "hgrn":
            z = refs_d[2][0]
            qh = _silu(q) * (HEAD_W ** -0.5)
            la, l1, oml = par_ref[d:d + 1, :], par_ref[2 + d:3 + d, :], par_ref[4 + d:5 + d, :]
            bb = l1 + _log_sigmoid(z)
            log_f = jnp.maximum(la, bb) + jnp.log(1.0 + jnp.exp(-jnp.abs(la - bb)))
            k = oml * jax.nn.sigmoid(-z)
        else:
            qh = q * (GLA_DK ** -0.5)
            k = refs_d[1][0]
            pre = jnp.dot(refs_d[3][0].astype(BF16), wg_ref[d], preferred_element_type=F32)
            log_f = _log_sigmoid(pre + par_ref[d:d + 1, :]) * (1.0 / GLA_GATE_NORM)
        tri = (ri >= ci if d == 0 else ri <= ci).astype(BF16)
        hi, lo = _split_hi_lo(log_f)
        cc = jnp.dot(tri, jnp.concatenate([hi, lo], axis=-1), preferred_element_type=F32)
        cum = cc[:, :w] + cc[:, w:]
        first_row, last_row = (0, rb - 1) if d == 0 else (rb - 1, 0)

        x0 = cum - group_row(cum, C, first_row % C)
        qs = [(qh * jnp.exp(x0)).astype(BF16)]
        ks = [(k * jnp.exp(-x0)).astype(BF16)]
        for m in levels:
            is_query = (rid & m) != 0 if d == 0 else (rid & m) == 0
            bnd = group_row(cum, 2 * m, m - 1 if d == 0 else m)
            qs.append((qh * jnp.exp(jnp.where(is_query, cum - bnd, neg))).astype(BF16))
            ks.append((k * jnp.exp(jnp.where(is_query, neg, bnd - cum))).astype(BF16))
        last = cum[last_row:last_row + 1, :]
        qe = (qh * jnp.exp(cum)).astype(BF16)
        kl = (k * jnp.exp(last - cum)).astype(BF16)
        dec = jnp.exp(last)

        outs = []
        for hh in range(nh):
            sl = slice(hh * HEAD_W, (hh + 1) * HEAD_W)
            p0 = nt(qs[0][:, sl], ks[0][:, sl])
            causal = ri >= ci if d == 0 else ri <= ci
            att = jnp.where(same_group(C), jnp.where(causal, p0, 0.0), 0.0)
            for lv, m in enumerate(levels):
                p = nt(qs[lv + 1][:, sl], ks[lv + 1][:, sl])
                att += p if 2 * m == rb else jnp.where(same_group(2 * m), p, 0.0)
            o = jnp.dot(att.astype(BF16), v16[:, sl], preferred_element_type=F32)
            s_t = st_ref[d * nh + hh]
            o += nt(qe[:, sl], s_t.astype(BF16))
            kv_t = lax.dot_general(v16[:, sl], kl[:, sl], TN_DIMS, preferred_element_type=F32)
            st_ref[d * nh + hh] = s_t * dec[:, sl] + kv_t
            outs.append(o)
        out_ref[row0, :] = jnp.concatenate(outs, axis=-1)

    direction(0, o_ref.at[0], pl.ds(pl.multiple_of(i * rb, rb), rb))
    direction(1, ob_ref, pl.ds(pl.multiple_of(bi * rb, rb), rb))

    @pl.when(i == nblk - 1)
    def _():
        def post(j, carry):
            rs = pl.ds(pl.multiple_of(j * rb, rb), rb)
            for hh in range(nh):
                lanes = pl.ds(hh * HEAD_W, HEAD_W)
                o = o_ref[0, rs, lanes] + ob_ref[rs, lanes]
                g = g_ref[0, rs, lanes]
                if mode == "hgrn":
                    res = _rms(o * jax.nn.sigmoid(g), par_ref[6:7, lanes])
                else:
                    res = _rms(o, par_ref[2:3, lanes]) * _silu(g)
                o_ref[0, rs, lanes] = res
            return carry

        lax.fori_loop(0, rows // rb, post, 0)


def _vscan(mode, y3, col_offsets, g_offset, params, wg=None):
    nb, rows, _ = y3.shape
    width = 4 * HEAD_W
    rb = CTX
    nblk = rows // rb
    fwd = lambda w, off: pl.BlockSpec((1, rb, w), lambda b, i: (b, i, off // w))
    bwd = lambda w, off: pl.BlockSpec((1, rb, w), lambda b, i: (b, jnp.where(i == 0, 0, nblk - i), off // w))
    in_specs, args = [], []
    for mk in (fwd, bwd):
        for off in col_offsets[mk is bwd]:
            in_specs.append(mk(width, off))
            args.append(y3)
        if mode == "gla":
            in_specs.append(mk(HEAD_W, OD_MISC))
            args.append(y3)
    in_specs.append(pl.BlockSpec((1, rows, width), lambda b, i: (b, 0, g_offset // width)))
    args.append(y3)
    if mode == "gla":
        in_specs.append(pl.BlockSpec((2, HEAD_W, width), lambda b, i: (0, 0, 0)))
        args.append(wg)
    in_specs.append(pl.BlockSpec((8, width), lambda b, i: (0, 0)))
    args.append(params)
    return pl.pallas_call(
        functools.partial(_vscan_body, mode, nblk),
        grid=(nb, nblk),
        in_specs=in_specs,
        out_specs=pl.BlockSpec((1, rows, width), lambda b, i: (b, 0, 0)),
        out_shape=jax.ShapeDtypeStruct((nb, rows, width), F32),
        scratch_shapes=[pltpu.VMEM((8, HEAD_W, HEAD_W), F32), pltpu.VMEM((rows, width), F32)],
        compiler_params=_cparams(("parallel", "arbitrary")),
        name="vscan_" + mode,
    )(*args)


def _rope(x, c, sa, sb):
    return x * c + pltpu.roll(x, HEAD_W - 1, 1) * sa + pltpu.roll(x, 1, 1) * sb


def _attn_body(q_ref, k_ref, v_ref, ck_ref, sak_ref, sbk_ref, cq_ref, saq_ref, sbq_ref,
               qg_ref, kg_ref, o_ref, kp_ref, vp_ref):
    qi = pl.program_id(2)
    rows = k_ref.shape[1]
    blk = 256

    @pl.when(qi == 0)
    def _():
        def prep(i, carry):
            rs = pl.ds(pl.multiple_of(i * blk, blk), blk)
            kn = _rms(k_ref[0, rs, :], kg_ref[...])
            kp_ref[rs, :] = _rope(kn, ck_ref[rs, :], sak_ref[rs, :], sbk_ref[rs, :]).astype(BF16)
            vp_ref[rs, :] = v_ref[0, rs, :].astype(BF16)
            return carry
        lax.fori_loop(0, rows // blk, prep, 0)

    def attend(nk):
        outs = []
        for r in range(2):
            qn = _rms(q_ref[0, :, r * HEAD_W:(r + 1) * HEAD_W], qg_ref[...])
            qr = (_rope(qn, cq_ref[...], saq_ref[...], sbq_ref[...]) * (HEAD_W ** -0.5)).astype(BF16)
            s = lax.dot_general(qr, kp_ref[0:nk, :], NT_DIMS, preferred_element_type=F32)
            m = jnp.max(s, axis=-1, keepdims=True)
            p = jnp.exp(s - m)
            l = jnp.sum(p, axis=-1, keepdims=True)
            o = jnp.dot(p.astype(BF16), vp_ref[0:nk, :], preferred_element_type=F32)
            outs.append(o / l)
        o_ref[0] = jnp.concatenate(outs, axis=-1)

    @pl.when(qi == 0)
    def _():
        attend(CTX)

    @pl.when(qi > 0)
    def _():
        attend(rows)


def _attention(y3, tables, q_gain, k_gain):
    nb, rows, _ = y3.shape
    c, sa, sb = tables
    full = pl.BlockSpec((rows, HEAD_W), lambda b, g, i: (0, 0))
    tile = pl.BlockSpec((TM, HEAD_W), lambda b, g, i: (i, 0))
    vec = pl.BlockSpec((1, HEAD_W), lambda b, g, i: (0, 0))
    return pl.pallas_call(
        _attn_body,
        grid=(nb, 2, rows // TM),
        in_specs=[
            pl.BlockSpec((1, TM, 2 * HEAD_W), lambda b, g, i: (b, i, EV_AQ // (2 * HEAD_W) + g)),
            pl.BlockSpec((1, rows, HEAD_W), lambda b, g, i: (b, 0, EV_AK // HEAD_W + g)),
            pl.BlockSpec((1, rows, HEAD_W), lambda b, g, i: (b, 0, EV_AV // HEAD_W + g)),
            full, full, full, tile, tile, tile, vec, vec,
        ],
        out_specs=pl.BlockSpec((1, TM, 2 * HEAD_W), lambda b, g, i: (b, i, g)),
        out_shape=jax.ShapeDtypeStruct((nb, rows, 4 * HEAD_W), F32),
        scratch_shapes=[pltpu.VMEM((rows, HEAD_W), BF16), pltpu.VMEM((rows, HEAD_W), BF16)],
        compiler_params=_cparams(("parallel", "parallel", "arbitrary")),
        name="attention",
    )(y3, y3, y3, c, sa, sb, c, sa, sb, q_gain, k_gain)


def _conv_body(x_ref, w_ref, b_ref, o_ref):
    rows = x_ref.shape[1]
    x = x_ref[0]
    row = lax.broadcasted_iota(jnp.int32, x.shape, 0)
    lo = jnp.where(row < CTX, 0, CTX)
    hi = jnp.where(row < CTX, CTX, rows)
    taps = w_ref.shape[0]
    acc = jnp.zeros(x.shape, F32) + b_ref[...]
    for j in range(taps):
        off = j - taps // 2
        xs = x if off == 0 else pltpu.roll(x, (-off) % rows, 0)
        src = row + off
        xs = jnp.where(src >= lo, jnp.where(src < hi, xs, 0.0), 0.0)
        acc += xs * w_ref[j:j + 1, :]
    o_ref[0] = _silu(acc)


def _ssd_conv(y3, conv_w, conv_b):
    nb, rows, _ = y3.shape
    nc = conv_w.shape[1]
    return pl.pallas_call(
        _conv_body,
        grid=(nb, nc // HEAD_W),
        in_specs=[
            pl.BlockSpec((1, rows, HEAD_W), lambda b, j: (b, 0, OD_XS // HEAD_W + j)),
            pl.BlockSpec((conv_w.shape[0], HEAD_W), lambda b, j: (0, j)),
            pl.BlockSpec((1, HEAD_W), lambda b, j: (0, j)),
        ],
        out_specs=pl.BlockSpec((1, rows, HEAD_W), lambda b, j: (b, 0, j)),
        out_shape=jax.ShapeDtypeStruct((nb, rows, nc), F32),
        compiler_params=_cparams(("parallel", "parallel")),
        name="ssd_conv",
    )(y3, conv_w, conv_b.reshape(1, nc))


def _ssd_body(n_ctx_chunks, n_chunks, x_ref, bm_ref, cm_ref, z_ref, misc_ref, sel_ref, par_ref,
              par2_ref, o_ref, st_ref, ob_ref):
    C = SSD_CHUNK
    P = SSD_DH
    rows = o_ref.shape[1]
    w = 4 * HEAD_W
    gw = 4 * P
    st_ref[...] = jnp.zeros(st_ref.shape, F32)
    ri = lax.broadcasted_iota(jnp.int32, (C, C), 0)
    ci = lax.broadcasted_iota(jnp.int32, (C, C), 1)
    masks = (ri >= ci, ri <= ci)
    tri = tuple(mk.astype(BF16) for mk in masks)

    def narrow(a):
        return jnp.concatenate([a[:, hh * HEAD_W:hh * HEAD_W + P] for hh in range(4)], axis=-1)

    def chunk(d, g, r0, misc2, out_ref):
        rs = pl.ds(r0, C)
        x = x_ref[0, rs, g * gw:(g + 1) * gw]
        bmat = bm_ref[0, rs, g * HEAD_W:(g + 1) * HEAD_W]
        cb16 = cm_ref[0, rs, g * HEAD_W:(g + 1) * HEAD_W].astype(BF16)
        b16 = bmat.astype(BF16)
        bt16 = bmat.T.astype(BF16)
        scores = lax.dot_general(cb16, b16, NT_DIMS, preferred_element_type=F32)
        dd = jnp.dot(misc2, sel_ref[g, d], preferred_element_type=F32)
        dt = dd[:C] + dd[C:] + par_ref[g, d:d + 1, :]
        dt = jnp.maximum(dt, 0.0) + jnp.log(1.0 + jnp.exp(-jnp.abs(dt)))
        la = dt * par_ref[g, 2 + d:3 + d, :]
        lh, ll = _split_hi_lo(la)
        cc = jnp.dot(tri[d], jnp.concatenate([lh, ll], axis=-1), preferred_element_type=F32)
        cum = cc[:, :w] + cc[:, w:]
        last = cum[C - 1:C, :] if d == 0 else cum[0:1, :]
        xdt = x * narrow(dt)
        s_t = st_ref[2 * g + d]
        y_in = jnp.dot(cb16, s_t.astype(BF16), preferred_element_type=F32) * narrow(jnp.exp(cum))
        xw = (xdt * narrow(jnp.exp(last - cum))).astype(BF16)
        st_ref[2 * g + d] = s_t * narrow(jnp.exp(last)) + jnp.dot(bt16, xw, preferred_element_type=F32)
        xdt16 = xdt.astype(BF16)
        ys = []
        for hh in range(4):
            cum_h = cum[:, hh * HEAD_W:(hh + 1) * HEAD_W]
            seg = jnp.where(masks[d], cum_h - cum_h.T, 0.0)
            dec = jnp.where(masks[d], jnp.exp(seg), 0.0)
            ys.append(jnp.dot((scores * dec).astype(BF16), xdt16[:, hh * P:(hh + 1) * P],
                              preferred_element_type=F32))
        out_ref[rs, g * gw:(g + 1) * gw] = jnp.concatenate(ys, axis=-1) + y_in

    def step(t, carry):
        cf = t
        cb = jnp.where(t < n_ctx_chunks, n_ctx_chunks - 1 - t, n_chunks + n_ctx_chunks - 1 - t)
        for d, c, out_ref in ((0, cf, o_ref.at[0]), (1, cb, ob_ref)):
            r0 = pl.multiple_of(c * C, C)
            mh, ml = _split_hi_lo(misc_ref[0, pl.ds(r0, C), :])
            misc2 = jnp.concatenate([mh, ml], axis=0)
            for g in range(2):
                chunk(d, g, r0, misc2, out_ref)
        return carry

    lax.fori_loop(0, n_chunks, step, 0)

    blk = 256

    def post(i, carry):
        rs = pl.ds(pl.multiple_of(i * blk, blk), blk)
        for g in range(2):
            cols = slice(g * gw, (g + 1) * gw)
            y = o_ref[0, rs, cols] + ob_ref[rs, cols] + par2_ref[g, 0:1, :] * x_ref[0, rs, cols]
            o_ref[0, rs, cols] = _rms(y * _silu(z_ref[0, rs, cols]), par2_ref[g, 1:2, :])
        return carry

    lax.fori_loop(0, rows // blk, post, 0)


def _ssd_scan(xbc, y3, sel, par, par2):
    nb, rows, _ = xbc.shape
    gw = 4 * SSD_DH
    full = lambda a: pl.BlockSpec(a.shape, lambda b: (0,) * a.ndim)
    return pl.pallas_call(
        functools.partial(_ssd_body, CTX // SSD_CHUNK, rows // SSD_CHUNK),
        grid=(nb,),
        in_specs=[
            pl.BlockSpec((1, rows, 2 * gw), lambda b: (b, 0, 0)),
            pl.BlockSpec((1, rows, 2 * HEAD_W), lambda b: (b, 0, gw // HEAD_W)),
            pl.BlockSpec((1, rows, 2 * HEAD_W), lambda b: (b, 0, gw // HEAD_W + 1)),
            pl.BlockSpec((1, rows, 2 * gw), lambda b: (b, 0, OD_Z // (2 * gw))),
            pl.BlockSpec((1, rows, HEAD_W), lambda b: (b, 0, OD_MISC // HEAD_W)),
            full(sel), full(par), full(par2),
        ],
        out_specs=pl.BlockSpec((1, rows, 2 * gw), lambda b: (b, 0, 0)),
        out_shape=jax.ShapeDtypeStruct((nb, rows, 2 * gw), F32),
        scratch_shapes=[pltpu.VMEM((4, HEAD_W, gw), F32), pltpu.VMEM((rows, 2 * gw), F32)],
        compiler_params=_cparams(("parallel",)),
        name="ssd_scan",
    )(xbc, xbc, xbc, y3, y3, sel, par, par2)


def _outproj_body(h_ref, a_ref, b_ref, g1_ref, sh2_ref, sc2_ref, wo_ref, lng_ref, lnb_ref, wq_ref,
                  sk_ref, h1_ref, hm_ref, st_ref):
    mix = jnp.concatenate([a_ref[...], b_ref[...]], axis=-1).astype(BF16)
    o = jnp.dot(mix, wo_ref[...], preferred_element_type=F32)
    h1 = _layer_norm(DN_ALPHA * h_ref[...] + g1_ref[0] * o, lng_ref[...], lnb_ref[...])
    h1_ref[...] = h1
    hm = (h1 * (1.0 + sc2_ref[0]) + sh2_ref[0]).astype(BF16)
    hm_ref[...] = hm
    qry = jnp.dot(hm, wq_ref[...], preferred_element_type=F32).astype(BF16)
    for j in range(2 * PEER_HEADS):
        qj = qry[:, j * PEER_NKEYS:(j + 1) * PEER_NKEYS]
        st_ref[j * PEER_NKEYS:(j + 1) * PEER_NKEYS, :] = lax.dot_general(
            sk_ref[j], qj, NT_DIMS, preferred_element_type=F32)


def _outproj(h, mix_a, mix_b, modl, w_out, ln_g, ln_b, wq, sk, n_batch, latent_only=False):
    n = h.shape[0]
    tpb = n // n_batch // TM
    if latent_only:
        lat = tpb - 1
        n_out = n_batch * lat * TM
        src = lambda j: (j // lat) * tpb + 1 + j % lat
        mod = lambda k: pl.BlockSpec((1, 1, D), lambda j: (k * MOD_ROWS + j // lat, 0, 0))
    else:
        n_out = n
        src = lambda j: j
        mod = lambda k: _mod_spec(k, n_batch, tpb)
    tok_in = pl.BlockSpec((TM, D), lambda j: (src(j), 0))
    half = pl.BlockSpec((TM, D // 2), lambda j: (src(j), 0))
    tok = pl.BlockSpec((TM, D), lambda j: (j, 0))
    vec = pl.BlockSpec((1, D), lambda j: (0, 0))
    nq = wq.shape[1]
    return pl.pallas_call(
        _outproj_body,
        grid=(n_out // TM,),
        in_specs=[tok_in, half, half, mod(2), mod(3), mod(4), pl.BlockSpec((D, D), lambda j: (0, 0)), vec, vec,
                  pl.BlockSpec((D, nq), lambda j: (0, 0)),
                  pl.BlockSpec(sk.shape, lambda j: (0, 0, 0))],
        out_specs=[tok, tok, pl.BlockSpec((nq, TM), lambda j: (0, j))],
        out_shape=[jax.ShapeDtypeStruct((n_out, D), F32), jax.ShapeDtypeStruct((n_out, D), BF16),
                   jax.ShapeDtypeStruct((nq, n_out), F32)],
        compiler_params=_cparams(("parallel",)),
        name="outproj",
    )(h, mix_a, mix_b, modl, modl, modl, w_out, ln_g, ln_b, wq, sk)


def _sort16_network():
    comps = []

    def merge(lo, hi, r):
        step = r * 2
        if step < hi - lo:
            merge(lo, hi, step)
            merge(lo + r, hi, step)
            comps.extend((j, j + r) for j in range(lo + r, hi - r, step))
        else:
            comps.append((lo, lo + r))

    def sort(lo, hi):
        if hi - lo >= 1:
            mid = lo + (hi - lo) // 2
            sort(lo, mid)
            sort(mid + 1, hi)
            merge(lo, hi, 1)

    sort(0, 15)
    return comps


SORT16 = _sort16_network()
BITONIC16 = [(j, j + s) for s in (8, 4, 2, 1) for j in range(16) if not j & s]


def _compare_exchange(v, comps):
    v = list(v)
    for a, b in comps:
        v[a], v[b] = jnp.maximum(v[a], v[b]), jnp.minimum(v[a], v[b])
    return v


def _merge_top16(a, b_rev):
    c = [a[k] if b_rev[k] is None else jnp.maximum(a[k], b_rev[k]) for k in range(16)]
    return _compare_exchange(c, BITONIC16)


def _top16_keys(ref, row0, col):
    v = [ref[row0 + 8 * j:row0 + 8 * j + 8, col] for j in range(16)]
    v = _compare_exchange(v, SORT16)
    for shift in (4, 2, 1):
        rolled = [pltpu.roll(x, shift, 0) for x in v]
        v = _merge_top16(v, rolled[::-1])
    return v


def _stats_body(st_ref, o_ref):
    k = PEER_TOPK
    nk = PEER_NKEYS
    sub = lax.broadcasted_iota(jnp.int32, (8, 128), 0)

    def column(tc, carry):
        col = pl.ds(pl.multiple_of(tc * 128, 128), 128)
        tops = []
        for side in range(2):
            packed = None
            for h in range(PEER_HEADS):
                a = _top16_keys(st_ref, (2 * h + side) * nk, col)
                packed = a if h == 0 else [jnp.where(sub == h, a[r], packed[r]) for r in range(k)]
            tops.append(packed)
        t1, t2 = tops
        best = [t1[0] + t2[r] for r in range(k)]
        for r1 in range(1, k):
            ln = k // (r1 + 1)
            lst = [t1[r1] + t2[r] for r in range(ln)]
            best = _merge_top16(best, [None] * (k - ln) + lst[::-1])
        z = 1.0
        for r in range(1, k):
            z = z + jnp.exp(best[r] - best[0])
        o_ref[0, :, col] = t1[0]
        o_ref[1, :, col] = t2[0] + jnp.log(z)
        o_ref[2, :, col] = best[k - 1]
        for r in range(k):
            o_ref[3 + r, :, col] = t2[r]
        return carry

    lax.fori_loop(0, o_ref.shape[2] // 128, column, 0)


def _peer_stats(st):
    n = st.shape[1]
    ts = 512
    return pl.pallas_call(
        _stats_body,
        grid=(n // ts,),
        in_specs=[pl.BlockSpec((st.shape[0], ts), lambda i: (0, i))],
        out_specs=pl.BlockSpec((3 + PEER_TOPK, PEER_HEADS, ts), lambda i: (0, 0, i)),
        out_shape=jax.ShapeDtypeStruct((3 + PEER_TOPK, PEER_HEADS, n), F32),
        compiler_params=_cparams(("parallel",)),
        name="peer_stats",
    )(st)


def _count_prefix(pred, rows):
    pick = lambda a, b, c: jnp.where(c, b, a)
    c8 = pred(rows[7])
    c4 = pred(pick(rows[3], rows[11], c8))
    c2 = pred(pick(pick(rows[1], rows[5], c4), pick(rows[9], rows[13], c4), c8))
    ev = [rows[2 * j] for j in range(8)]
    c1 = pred(pick(pick(pick(ev[0], ev[1], c2), pick(ev[2], ev[3], c2), c4),
                   pick(pick(ev[4], ev[5], c2), pick(ev[6], ev[7], c2), c4), c8))
    c16 = pred(rows[15])
    val = lambda c, v: jnp.where(c, v, 0.0)
    return val(c8, 8.0) + val(c4, 4.0) + val(c2, 2.0) + val(c1, 1.0) + val(c16, 1.0)


def _expert_body(hm_ref, st_ref, stats_ref, u_ref, vt_ref, y_ref, e1_ref, n1_ref, e2_ref, r2_ref,
                 a_ref, p_ref, acc_ref, hmt_ref):
    e = pl.program_id(1)
    ne = pl.num_programs(1)
    t = hm_ref.shape[0]
    nk = PEER_NKEYS
    k = PEER_TOPK
    m = PEER_EB // nk
    ntc = t // 128

    @pl.when(e == 0)
    def _():
        def prep(tc, carry):
            col = pl.ds(pl.multiple_of(tc * 128, 128), 128)
            for h in range(PEER_HEADS):
                rows = slice(h * nk, (h + 1) * nk)
                s1 = st_ref[2 * h * nk:(2 * h + 1) * nk, col]
                s2 = st_ref[(2 * h + 1) * nk:(2 * h + 2) * nk, col]
                tau = stats_ref[2, h:h + 1, col]
                t2 = [stats_ref[3 + r, h:h + 1, col] for r in range(k)]
                n1 = _count_prefix(lambda thr: s1 + thr >= tau, t2)
                r2 = _count_prefix(lambda thr: thr > s2, t2)
                e1_ref[rows, col] = jnp.exp(s1 - stats_ref[0, h:h + 1, col])
                n1_ref[rows, col] = n1
                e2_ref[rows, col] = jnp.exp(s2 - stats_ref[1, h:h + 1, col]).astype(BF16)
                r2_ref[rows, col] = r2.astype(BF16)
            return carry

        lax.fori_loop(0, ntc, prep, 0)
        acc_ref[...] = jnp.zeros(acc_ref.shape, F32)
        hmt_ref[...] = hm_ref[...].astype(F32).T.astype(BF16)

    a_ref[...] = jnp.dot(u_ref[...], hmt_ref[...], preferred_element_type=F32)

    def build(tc, carry):
        col = pl.ds(pl.multiple_of(tc * 128, 128), 128)
        own = [pl.ds(pl.multiple_of(h * nk + e * m, m), m) for h in range(PEER_HEADS)]
        e1 = [e1_ref[own[h], col].astype(BF16) for h in range(PEER_HEADS)]
        n1 = [n1_ref[own[h], col].astype(BF16) for h in range(PEER_HEADS)]
        zero = jnp.zeros((), BF16)
        group = 4
        for g0 in range(0, m, group):
            gates = [jnp.zeros((nk, 128), BF16) for _ in range(group)]
            for h in range(PEER_HEADS):
                rows2 = slice(h * nk, (h + 1) * nk)
                r2 = r2_ref[rows2, col]
                e2 = e2_ref[rows2, col]
                for j in range(group):
                    n1b = jnp.broadcast_to(n1[h][g0 + j:g0 + j + 1, :], (nk, 128))
                    e1b = jnp.broadcast_to(e1[h][g0 + j:g0 + j + 1, :], (nk, 128))
                    gates[j] = gates[j] + e1b * jnp.where(r2 < n1b, e2, zero)
            for j in range(group):
                rows = slice((g0 + j) * nk, (g0 + j + 1) * nk)
                act = jax.nn.gelu(a_ref[rows, col], approximate=True)
                p_ref[rows, col] = act.astype(BF16) * gates[j]
        return carry

    lax.fori_loop(0, ntc, build, 0)
    acc_ref[...] += jnp.dot(vt_ref[...], p_ref[...], preferred_element_type=F32)

    @pl.when(e == ne - 1)
    def _():
        y_ref[...] = acc_ref[...].T


def _peer_experts(hm, st, stats, u, vt):
    n = hm.shape[0]
    ne = u.shape[0] // PEER_EB
    assert (PEER_EB // PEER_NKEYS) % 8 == 0
    t = PEER_T
    half = (PEER_HEADS * PEER_NKEYS, t)
    return pl.pallas_call(
        _expert_body,
        grid=(n // t, ne),
        in_specs=[
            pl.BlockSpec((t, D), lambda i, e: (i, 0)),
            pl.BlockSpec((st.shape[0], t), lambda i, e: (0, i)),
            pl.BlockSpec((stats.shape[0], PEER_HEADS, t), lambda i, e: (0, 0, i)),
            pl.BlockSpec((PEER_EB, D), lambda i, e: (e, 0)),
            pl.BlockSpec((D, PEER_EB), lambda i, e: (0, e)),
        ],
        out_specs=pl.BlockSpec((t, D), lambda i, e: (i, 0)),
        out_shape=jax.ShapeDtypeStruct((n, D), F32),
        scratch_shapes=[pltpu.VMEM(half, F32), pltpu.VMEM(half, F32), pltpu.VMEM(half, BF16),
                        pltpu.VMEM(half, BF16), pltpu.VMEM((PEER_EB, t), F32),
                        pltpu.VMEM((PEER_EB, t), BF16), pltpu.VMEM((D, t), F32),
                        pltpu.VMEM((D, t), BF16)],
        compiler_params=_cparams(("parallel", "arbitrary")),
        name="peer_experts",
    )(hm, st, stats, u, vt)


def _rope_tables(rows_lat):
    pairs = HEAD_W // 4
    t = jnp.arange(rows_lat)
    inv = ROPE_THETA ** (-jnp.arange(pairs, dtype=F32) / pairs)
    ang = jnp.concatenate([(t // GRID_W).astype(F32)[:, None] * inv,
                           (t % GRID_W).astype(F32)[:, None] * inv], axis=-1)
    cos, sin = jnp.cos(ang), jnp.sin(ang)
    zero = jnp.zeros_like(sin)
    c = jnp.stack([cos, cos], -1).reshape(rows_lat, HEAD_W)
    sa = jnp.stack([-sin, zero], -1).reshape(rows_lat, HEAD_W)
    sb = jnp.stack([zero, sin], -1).reshape(rows_lat, HEAD_W)
    ident = jnp.ones((CTX, HEAD_W), F32)
    none = jnp.zeros((CTX, HEAD_W), F32)
    return (jnp.concatenate([ident, c]), jnp.concatenate([none, sa]), jnp.concatenate([none, sb]))


def _odd_column_map():
    src = -np.ones((OD_W,), np.int64)
    for h in range(4):
        src[OD_Q + h * HEAD_W:OD_Q + h * HEAD_W + GLA_DK] = np.arange(h * GLA_DK, (h + 1) * GLA_DK)
        src[OD_K + h * HEAD_W:OD_K + h * HEAD_W + GLA_DK] = 256 + np.arange(h * GLA_DK, (h + 1) * GLA_DK)
    src[OD_V:OD_V + 512] = 512 + np.arange(512)
    src[OD_G:OD_G + 512] = 1024 + np.arange(512)
    src[OD_Z:OD_Z + 512] = 1568 + np.arange(512)
    src[OD_XS:OD_XS + 512] = 2080 + np.arange(512)
    src[OD_BM:OD_BM + 256] = 2592 + np.arange(256)
    src[OD_CM:OD_CM + 256] = 2848 + np.arange(256)
    src[OD_MISC + MISC_LRF:OD_MISC + MISC_LRF + 16] = 1536 + np.arange(16)
    src[OD_MISC + MISC_LRB:OD_MISC + MISC_LRB + 16] = 1552 + np.arange(16)
    src[OD_MISC + MISC_DTF:OD_MISC + MISC_DTF + 8] = 3104 + np.arange(8)
    src[OD_MISC + MISC_DTB:OD_MISC + MISC_DTB + 8] = 3112 + np.arange(8)
    return src


def _relayout_odd_w(w):
    src = _odd_column_map()
    cols = jnp.take(w, jnp.asarray(np.maximum(src, 0)), axis=1)
    return jnp.where(jnp.asarray(src >= 0)[None, :], cols, 0.0)


def _pad_rows(rows_list, width):
    out = [jnp.broadcast_to(jnp.asarray(r, F32).reshape(1, width), (1, width)) for r in rows_list]
    out += [jnp.zeros((1, width), F32)] * (8 - len(out))
    return jnp.concatenate(out, axis=0)


def _hgrn_params(lb, gain):
    la = jnp.log(jnp.maximum(lb, LB_FLOOR))
    l1 = jnp.log1p(-lb)
    oml = 1.0 - lb
    return _pad_rows([la[0], la[1], l1[0], l1[1], oml[0], oml[1], jnp.tile(gain, 4)], 4 * HEAD_W)


def _gla_params(gate_w, gate_b, gain):
    wg = jnp.zeros((2, HEAD_W, 4 * HEAD_W), F32)
    gb = jnp.zeros((2, 4 * HEAD_W), F32)
    for h in range(4):
        s = slice(h * GLA_DK, (h + 1) * GLA_DK)
        dst = slice(h * HEAD_W, h * HEAD_W + GLA_DK)
        for d, off in enumerate((MISC_LRF, MISC_LRB)):
            wg = wg.at[d, off:off + 16, dst].set(gate_w[d][:, s])
            gb = gb.at[d, dst].set(gate_b[d, s])
    return wg.astype(BF16), _pad_rows([gb[0], gb[1], jnp.tile(gain, 4)], 4 * HEAD_W)


def _ssd_params(dt_bias, a_log, d_skip, norm_g):
    sel = np.zeros((2, 2, HEAD_W, 4 * HEAD_W), np.float32)
    for g in range(2):
        for d, off in enumerate((MISC_DTF, MISC_DTB)):
            for hh in range(4):
                sel[g, d, off + 4 * g + hh, hh * HEAD_W:(hh + 1) * HEAD_W] = 1.0
    par, par2 = [], []
    neg_a = -jnp.exp(a_log.astype(F32))
    for g in range(2):
        hs = slice(4 * g, 4 * g + 4)
        rep = lambda v: jnp.repeat(v[hs], HEAD_W)
        par.append(_pad_rows([rep(dt_bias[0]), rep(dt_bias[1]), rep(neg_a[0]), rep(neg_a[1])], 4 * HEAD_W))
        par2.append(_pad_rows([jnp.repeat(d_skip[hs], SSD_DH), norm_g[g * 256:(g + 1) * 256]], 4 * SSD_DH))
    return jnp.asarray(sel, BF16), jnp.stack(par), jnp.stack(par2)


def kernel(x, c, ctx, c_ctx, mod_w, mod_b, ln_g, ln_b, ev_w_in, ev_w_out, hg_lb_logits, hg_norm_g, at_q_norm_g, at_k_norm_g, od_w_in, od_w_out, gla_gate_w, gla_gate_b, gla_norm_g, ssd_conv_w, ssd_conv_b, ssd_dt_bias, ssd_a_log, ssd_d, ssd_norm_g, peer_wq, peer_subkeys, peer_u, peer_v):
    nb, seq, _ = x.shape
    assert ctx.shape[1] == CTX and seq % TM == 0 and nb + 1 <= MOD_ROWS
    rows = CTX + seq
    n = nb * rows

    cc = jnp.concatenate([c, c_ctx[None, :], jnp.zeros((MOD_ROWS - nb - 1, D), F32)], axis=0)
    mods = _modulation(cc, mod_w, mod_b)
    mods = mods.reshape(DEPTH, MOD_ROWS, N_MOD, D).transpose(0, 2, 1, 3).reshape(DEPTH, N_MOD * MOD_ROWS, 1, D)

    sm = jax.nn.softmax(hg_lb_logits.astype(F32), axis=1)
    hg_lb = jnp.cumsum(sm, axis=1) - sm[:, :1]
    tables = _rope_tables(seq)

    h = jnp.concatenate([ctx, x], axis=1).reshape(n, D)
    pre = None
    for l in range(DEPTH):
        j = l // 2
        modl = mods[l]
        if l % 2 == 0:
            y, h = _inproj(h, ev_w_in[j].astype(BF16), modl, nb, pre)
            y3 = y.reshape(nb, rows, EV_W)
            mix_a = _vscan("hgrn", y3, ((EV_Q, EV_I, EV_ZF), (EV_Q, EV_I, EV_ZB)), EV_G,
                           _hgrn_params(hg_lb[:, j], hg_norm_g[j]))
            mix_b = _attention(y3, tables, at_q_norm_g[j].reshape(1, HEAD_W), at_k_norm_g[j].reshape(1, HEAD_W))
            w_out = ev_w_out[j]
        else:
            y, h = _inproj(h, _relayout_odd_w(od_w_in[j]).astype(BF16), modl, nb, pre)
            y3 = y.reshape(nb, rows, OD_W)
            wg, gpar = _gla_params(gla_gate_w[j], gla_gate_b[j], gla_norm_g[j])
            mix_a = _vscan("gla", y3, ((OD_Q, OD_K, OD_V),) * 2, OD_G, gpar, wg)
            xbc = _ssd_conv(y3, ssd_conv_w[j], ssd_conv_b[j])
            sel, spar, spar2 = _ssd_params(ssd_dt_bias[j], ssd_a_log[j], ssd_d[j], ssd_norm_g[j])
            mix_b = _ssd_scan(xbc, y3, sel, spar, spar2)
            w_out = od_w_out[j]
        lg1, lb1 = ln_g[l, 0].reshape(1, D), ln_b[l, 0].reshape(1, D)
        sk = peer_subkeys[l].reshape(2 * PEER_HEADS, PEER_NKEYS, PEER_NKEYS).astype(BF16)
        h1, hm, st = _outproj(h, mix_a.reshape(n, D // 2), mix_b.reshape(n, D // 2), modl,
                              w_out.astype(BF16), lg1, lb1, peer_wq[l].astype(BF16), sk, nb,
                              latent_only=(l == DEPTH - 1))
        stats = _peer_stats(st)
        yp = _peer_experts(hm, st, stats, peer_u[l].astype(BF16), peer_v[l].astype(BF16).T)
        pre = (yp, modl, ln_g[l, 1].reshape(1, D), ln_b[l, 1].reshape(1, D))
        h = h1
    out = _final(h, pre[0], pre[1], pre[2], pre[3], nb)
    return out.reshape(nb, seq, D)
```

```python
import functools

import numpy as np
import jax
import jax.numpy as jnp
from jax import lax
from jax.experimental import pallas as pl
from jax.experimental.pallas import tpu as pltpu

F32 = jnp.float32
BF16 = jnp.bfloat16

D = 1024
DEPTH = 4
N_MOD = 6
CTX = 256
GRID_W = 64
DN_ALPHA = (2 * DEPTH) ** 0.25
NORM_EPS = 1e-6
ROPE_THETA = 10000.0
LB_FLOOR = 1e-30
TM = 256
MOD_ROWS = 24
VMEM_LIMIT = 56 * 1024 * 1024

HEAD_W = 128
VEC_CHUNK = 16
SSD_CHUNK = 128
SSD_DH = 64
GLA_DK = 64
GLA_GATE_NORM = 16.0
PEER_HEADS = 8
PEER_NKEYS = 128
PEER_TOPK = 16
PEER_T = 512
PEER_EB = 2048
PEER_CW = 512

EV_Q, EV_I, EV_ZF, EV_ZB, EV_G, EV_AQ, EV_AK, EV_AV = 0, 512, 1024, 1536, 2048, 2560, 3072, 3328
EV_W = 3584
OD_Q, OD_K, OD_V, OD_G, OD_Z, OD_XS, OD_BM, OD_CM, OD_MISC = 0, 512, 1024, 1536, 2048, 2560, 3072, 3328, 3584
OD_W = 3712
MISC_LRF, MISC_LRB, MISC_DTF, MISC_DTB = 0, 16, 32, 40

NT_DIMS = (((1,), (1,)), ((), ()))
TN_DIMS = (((0,), (0,)), ((), ()))


def _cparams(sem):
    return pltpu.CompilerParams(dimension_semantics=sem, vmem_limit_bytes=VMEM_LIMIT)


def _layer_norm(y, g, b):
    mu = jnp.mean(y, axis=-1, keepdims=True)
    yc = y - mu
    var = jnp.mean(yc * yc, axis=-1, keepdims=True)
    return yc * lax.rsqrt(var + NORM_EPS) * g + b


def _rms(y, g):
    return y * lax.rsqrt(jnp.mean(y * y, axis=-1, keepdims=True) + NORM_EPS) * g


def _log_sigmoid(z):
    return jnp.minimum(z, 0.0) - jnp.log(1.0 + jnp.exp(-jnp.abs(z)))


def _silu(z):
    return z * jax.nn.sigmoid(z)


def _split_hi_lo(a):
    hi = a.astype(BF16)
    lo = (a - hi.astype(F32)).astype(BF16)
    return hi, lo


def _mod_body(c_ref, w_ref, b_ref, o_ref):
    a = _silu(c_ref[...])
    a_hi, a_lo = _split_hi_lo(a)
    w_hi, w_lo = _split_hi_lo(w_ref[0])
    acc = jnp.dot(a_hi, w_hi, preferred_element_type=F32)
    acc += jnp.dot(a_lo, w_hi, preferred_element_type=F32)
    acc += jnp.dot(a_hi, w_lo, preferred_element_type=F32)
    o_ref[0] = acc + b_ref[0]


def _modulation(cc, mod_w, mod_b):
    tn = 1536
    nw = mod_w.shape[-1]
    return pl.pallas_call(
        _mod_body,
        grid=(DEPTH, nw // tn),
        in_specs=[
            pl.BlockSpec((MOD_ROWS, D), lambda l, j: (0, 0)),
            pl.BlockSpec((1, D, tn), lambda l, j: (l, 0, j)),
            pl.BlockSpec((1, 1, tn), lambda l, j: (l, 0, j)),
        ],
        out_specs=pl.BlockSpec((1, MOD_ROWS, tn), lambda l, j: (l, 0, j)),
        out_shape=jax.ShapeDtypeStruct((DEPTH, MOD_ROWS, nw), F32),
        compiler_params=_cparams(("parallel", "parallel")),
        name="modulation",
    )(cc, mod_w, mod_b.reshape(DEPTH, 1, nw))


def _mod_spec(k, n_batch, tiles_per_batch):
    def index(i):
        row = jnp.where(i % tiles_per_batch == 0, n_batch, i // tiles_per_batch)
        return (k * MOD_ROWS + row, 0, 0)
    return pl.BlockSpec((1, 1, D), index)


def _inproj_body(has_pre, *refs):
    if has_pre:
        x_ref, y_ref, g2_ref, lng_ref, lnb_ref, sh_ref, sc_ref, w_ref, o_ref, h_ref = refs
        h = _layer_norm(DN_ALPHA * x_ref[...] + g2_ref[0] * y_ref[...], lng_ref[...], lnb_ref[...])
        h_ref[...] = h
    else:
        x_ref, sh_ref, sc_ref, w_ref, o_ref = refs
        h = x_ref[...]
    u = h * (1.0 + sc_ref[0]) + sh_ref[0]
    o_ref[...] = jnp.dot(u.astype(BF16), w_ref[...], preferred_element_type=F32)


def _inproj(h, w, modl, n_batch, pre=None):
    n = h.shape[0]
    nw = w.shape[1]
    tpb = n // n_batch // TM
    tok = pl.BlockSpec((TM, D), lambda i: (i, 0))
    vec = pl.BlockSpec((1, D), lambda i: (0, 0))
    in_specs = [tok]
    args = [h]
    if pre is not None:
        y, prev_modl, ln_g, ln_b = pre
        in_specs += [tok, _mod_spec(5, n_batch, tpb), vec, vec]
        args += [y, prev_modl, ln_g, ln_b]
    in_specs += [_mod_spec(0, n_batch, tpb), _mod_spec(1, n_batch, tpb),
                 pl.BlockSpec((D, nw), lambda i: (0, 0))]
    args += [modl, modl, w]
    out_specs = [pl.BlockSpec((TM, nw), lambda i: (i, 0))]
    out_shape = [jax.ShapeDtypeStruct((n, nw), F32)]
    if pre is not None:
        out_specs.append(tok)
        out_shape.append(jax.ShapeDtypeStruct((n, D), F32))
    res = pl.pallas_call(
        functools.partial(_inproj_body, pre is not None),
        grid=(n // TM,),
        in_specs=in_specs,
        out_specs=out_specs,
        out_shape=out_shape,
        compiler_params=_cparams(("parallel",)),
        name="inproj",
    )(*args)
    return (res[0], res[1]) if pre is not None else (res[0], h)


def _final_body(x_ref, y_ref, g2_ref, lng_ref, lnb_ref, o_ref):
    o_ref[...] = _layer_norm(DN_ALPHA * x_ref[...] + g2_ref[0] * y_ref[...], lng_ref[...], lnb_ref[...])


def _final(h, y, modl, ln_g, ln_b, n_batch):
    n = h.shape[0]
    lat = n // n_batch // TM
    tok = pl.BlockSpec((TM, D), lambda b, j: (b * lat + j, 0))
    vec = pl.BlockSpec((1, D), lambda b, j: (0, 0))
    return pl.pallas_call(
        _final_body,
        grid=(n_batch, lat),
        in_specs=[tok, tok, pl.BlockSpec((1, 1, D), lambda b, j: (5 * MOD_ROWS + b, 0, 0)), vec, vec],
        out_specs=tok,
        out_shape=jax.ShapeDtypeStruct((n, D), F32),
        compiler_params=_cparams(("parallel", "parallel")),
        name="final_norm",
    )(h, y, modl, ln_g, ln_b)


def _vscan_body(mode, nblk, *refs):
    C = VEC_CHUNK
    nh = 4
    if mode == "hgrn":
        qf_ref, vf_ref, zf_ref, qb_ref, vb_ref, zb_ref, g_ref, par_ref, o_ref, st_ref, ob_ref = refs
        dir_refs = ((qf_ref, vf_ref, zf_ref), (qb_ref, vb_ref, zb_ref))
    else:
        (qf_ref, kf_ref, vf_ref, mf_ref, qb_ref, kb_ref, vb_ref, mb_ref, g_ref, wg_ref, par_ref,
         o_ref, st_ref, ob_ref) = refs
        dir_refs = ((qf_ref, kf_ref, vf_ref, mf_ref), (qb_ref, kb_ref, vb_ref, mb_ref))
    rows = o_ref.shape[1]
    rb = qf_ref.shape[1]
    i = pl.program_id(1)
    bi = jnp.where(i == 0, 0, nblk - i)

    @pl.when(i == 0)
    def _():
        st_ref[...] = jnp.zeros(st_ref.shape, F32)

    w = nh * HEAD_W
    ri = lax.broadcasted_iota(jnp.int32, (rb, rb), 0)
    ci = lax.broadcasted_iota(jnp.int32, (rb, rb), 1)
    rid = lax.broadcasted_iota(jnp.int32, (rb, w), 0)
    levels = [C << s for s in range((rb // C).bit_length() - 1)]
    neg = -1e30

    def group_row(x, grp, r):
        g = x.reshape(rb // grp, grp, w)
        return jnp.broadcast_to(g[:, r:r + 1, :], g.shape).reshape(rb, w)

    def nt(a, b):
        return lax.dot_general(a, b, NT_DIMS, preferred_element_type=F32)

    def same_group(grp):
        sh = grp.bit_length() - 1
        return (ri >> sh) == (ci >> sh)

    def direction(d, out_ref, row0):
        refs_d = dir_refs[d]
        q = refs_d[0][0]
        v16 = refs_d[-2 if mode == "gla" else 1][0].astype(BF16)
        if mode == "hgrn":
            z = refs_d[2][0]
            qh = _silu(q) * (HEAD_W ** -0.5)
            la, l1, oml = par_ref[d:d + 1, :], par_ref[2 + d:3 + d, :], par_ref[4 + d:5 + d, :]
            bb = l1 + _log_sigmoid(z)
            log_f = jnp.maximum(la, bb) + jnp.log(1.0 + jnp.exp(-jnp.abs(la - bb)))
            k = oml * jax.nn.sigmoid(-z)
        else:
            qh = q * (GLA_DK ** -0.5)
            k = refs_d[1][0]
            pre = jnp.dot(refs_d[3][0].astype(BF16), wg_ref[d], preferred_element_type=F32)
            log_f = _log_sigmoid(pre + par_ref[d:d + 1, :]) * (1.0 / GLA_GATE_NORM)
        tri = (ri >= ci if d == 0 else ri <= ci).astype(BF16)
        hi, lo = _split_hi_lo(log_f)
        cc = jnp.dot(tri, jnp.concatenate([hi, lo], axis=-1), preferred_element_type=F32)
        cum = cc[:, :w] + cc[:, w:]
        first_row, last_row = (0, rb - 1) if d == 0 else (rb - 1, 0)

        x0 = cum - group_row(cum, C, first_row % C)
        qs = [(qh * jnp.exp(x0)).astype(BF16)]
        ks = [(k * jnp.exp(-x0)).astype(BF16)]
        for m in levels:
            is_query = (rid & m) != 0 if d == 0 else (rid & m) == 0
            bnd = group_row(cum, 2 * m, m - 1 if d == 0 else m)
            qs.append((qh * jnp.exp(jnp.where(is_query, cum - bnd, neg))).astype(BF16))
            ks.append((k * jnp.exp(jnp.where(is_query, neg, bnd - cum))).astype(BF16))
        last = cum[last_row:last_row + 1, :]
        qe = (qh * jnp.exp(cum)).astype(BF16)
        kl = (k * jnp.exp(last - cum)).astype(BF16)
        dec = jnp.exp(last)

        outs = []
        for hh in range(nh):
            sl = slice(hh * HEAD_W, (hh + 1) * HEAD_W)
            p0 = nt(qs[0][:, sl], ks[0][:, sl])
            causal = ri >= ci if d == 0 else ri <= ci
            att = jnp.where(same_group(C), jnp.where(causal, p0, 0.0), 0.0)
            for lv, m in enumerate(levels):
                p = nt(qs[lv + 1][:, sl], ks[lv + 1][:, sl])
                att += p if 2 * m == rb else jnp.where(same_group(2 * m), p, 0.0)
            o = jnp.dot(att.astype(BF16), v16[:, sl], preferred_element_type=F32)
            s_t = st_ref[d * nh + hh]
            o += nt(qe[:, sl], s_t.astype(BF16))
            kv_t = lax.dot_general(v16[:, sl], kl[:, sl], TN_DIMS, preferred_element_type=F32)
            st_ref[d * nh + hh] = s_t * dec[:, sl] + kv_t
            outs.append(o)
        out_ref[row0, :] = jnp.concatenate(outs, axis=-1)

    direction(0, o_ref.at[0], pl.ds(pl.multiple_of(i * rb, rb), rb))
    direction(1, ob_ref, pl.ds(pl.multiple_of(bi * rb, rb), rb))

    @pl.when(i == nblk - 1)
    def _():
        def post(j, carry):
            rs = pl.ds(pl.multiple_of(j * rb, rb), rb)
            for hh in range(nh):
                lanes = pl.ds(hh * HEAD_W, HEAD_W)
                o = o_ref[0, rs, lanes] + ob_ref[rs, lanes]
                g = g_ref[0, rs, lanes]
                if mode == "hgrn":
                    res = _rms(o * jax.nn.sigmoid(g), par_ref[6:7, lanes])
                else:
                    res = _rms(o, par_ref[2:3, lanes]) * _silu(g)
                o_ref[0, rs, lanes] = res
            return carry

        lax.fori_loop(0, rows // rb, post, 0)


def _vscan(mode, y3, col_offsets, g_offset, params, wg=None):
    nb, rows, _ = y3.shape
    width = 4 * HEAD_W
    rb = CTX
    nblk = rows // rb
    fwd = lambda w, off: pl.BlockSpec((1, rb, w), lambda b, i: (b, i, off // w))
    bwd = lambda w, off: pl.BlockSpec((1, rb, w), lambda b, i: (b, jnp.where(i == 0, 0, nblk - i), off // w))
    in_specs, args = [], []
    for mk in (fwd, bwd):
        for off in col_offsets[mk is bwd]:
            in_specs.append(mk(width, off))
            args.append(y3)
        if mode == "gla":
            in_specs.append(mk(HEAD_W, OD_MISC))
            args.append(y3)
    in_specs.append(pl.BlockSpec((1, rows, width), lambda b, i: (b, 0, g_offset // width)))
    args.append(y3)
    if mode == "gla":
        in_specs.append(pl.BlockSpec((2, HEAD_W, width), lambda b, i: (0, 0, 0)))
        args.append(wg)
    in_specs.append(pl.BlockSpec((8, width), lambda b, i: (0, 0)))
    args.append(params)
    return pl.pallas_call(
        functools.partial(_vscan_body, mode, nblk),
        grid=(nb, nblk),
        in_specs=in_specs,
        out_specs=pl.BlockSpec((1, rows, width), lambda b, i: (b, 0, 0)),
        out_shape=jax.ShapeDtypeStruct((nb, rows, width), F32),
        scratch_shapes=[pltpu.VMEM((8, HEAD_W, HEAD_W), F32), pltpu.VMEM((rows, width), F32)],
        compiler_params=_cparams(("parallel", "arbitrary")),
        name="vscan_" + mode,
    )(*args)


def _rope(x, c, sa, sb):
    return x * c + pltpu.roll(x, HEAD_W - 1, 1) * sa + pltpu.roll(x, 1, 1) * sb


def _attn_body(q_ref, k_ref, v_ref, ck_ref, sak_ref, sbk_ref, cq_ref, saq_ref, sbq_ref,
               qg_ref, kg_ref, o_ref, kp_ref, vp_ref):
    qi = pl.program_id(2)
    rows = k_ref.shape[1]
    blk = 256

    @pl.when(qi == 0)
    def _():
        def prep(i, carry):
            rs = pl.ds(pl.multiple_of(i * blk, blk), blk)
            kn = _rms(k_ref[0, rs, :], kg_ref[...])
            kp_ref[rs, :] = _rope(kn, ck_ref[rs, :], sak_ref[rs, :], sbk_ref[rs, :]).astype(BF16)
            vp_ref[rs, :] = v_ref[0, rs, :].astype(BF16)
            return carry
        lax.fori_loop(0, rows // blk, prep, 0)

    def attend(nk):
        outs = []
        for r in range(2):
            qn = _rms(q_ref[0, :, r * HEAD_W:(r + 1) * HEAD_W], qg_ref[...])
            qr = (_rope(qn, cq_ref[...], saq_ref[...], sbq_ref[...]) * (HEAD_W ** -0.5)).astype(BF16)
            s = lax.dot_general(qr, kp_ref[0:nk, :], NT_DIMS, preferred_element_type=F32)
            m = jnp.max(s, axis=-1, keepdims=True)
            p = jnp.exp(s - m)
            l = jnp.sum(p, axis=-1, keepdims=True)
            o = jnp.dot(p.astype(BF16), vp_ref[0:nk, :], preferred_element_type=F32)
            outs.append(o / l)
        o_ref[0] = jnp.concatenate(outs, axis=-1)

    @pl.when(qi == 0)
    def _():
        attend(CTX)

    @pl.when(qi > 0)
    def _():
        attend(rows)


def _attention(y3, tables, q_gain, k_gain):
    nb, rows, _ = y3.shape
    c, sa, sb = tables
    full = pl.BlockSpec((rows, HEAD_W), lambda b, g, i: (0, 0))
    tile = pl.BlockSpec((TM, HEAD_W), lambda b, g, i: (i, 0))
    vec = pl.BlockSpec((1, HEAD_W), lambda b, g, i: (0, 0))
    return pl.pallas_call(
        _attn_body,
        grid=(nb, 2, rows // TM),
        in_specs=[
            pl.BlockSpec((1, TM, 2 * HEAD_W), lambda b, g, i: (b, i, EV_AQ // (2 * HEAD_W) + g)),
            pl.BlockSpec((1, rows, HEAD_W), lambda b, g, i: (b, 0, EV_AK // HEAD_W + g)),
            pl.BlockSpec((1, rows, HEAD_W), lambda b, g, i: (b, 0, EV_AV // HEAD_W + g)),
            full, full, full, tile, tile, tile, vec, vec,
        ],
        out_specs=pl.BlockSpec((1, TM, 2 * HEAD_W), lambda b, g, i: (b, i, g)),
        out_shape=jax.ShapeDtypeStruct((nb, rows, 4 * HEAD_W), F32),
        scratch_shapes=[pltpu.VMEM((rows, HEAD_W), BF16), pltpu.VMEM((rows, HEAD_W), BF16)],
        compiler_params=_cparams(("parallel", "parallel", "arbitrary")),
        name="attention",
    )(y3, y3, y3, c, sa, sb, c, sa, sb, q_gain, k_gain)


def _conv_body(x_ref, w_ref, b_ref, o_ref):
    rows = x_ref.shape[1]
    x = x_ref[0]
    row = lax.broadcasted_iota(jnp.int32, x.shape, 0)
    lo = jnp.where(row < CTX, 0, CTX)
    hi = jnp.where(row < CTX, CTX, rows)
    taps = w_ref.shape[0]
    acc = jnp.zeros(x.shape, F32) + b_ref[...]
    for j in range(taps):
        off = j - taps // 2
        xs = x if off == 0 else pltpu.roll(x, (-off) % rows, 0)
        src = row + off
        xs = jnp.where(src >= lo, jnp.where(src < hi, xs, 0.0), 0.0)
        acc += xs * w_ref[j:j + 1, :]
    o_ref[0] = _silu(acc)


def _ssd_conv(y3, conv_w, conv_b):
    nb, rows, _ = y3.shape
    nc = conv_w.shape[1]
    return pl.pallas_call(
        _conv_body,
        grid=(nb, nc // HEAD_W),
        in_specs=[
            pl.BlockSpec((1, rows, HEAD_W), lambda b, j: (b, 0, OD_XS // HEAD_W + j)),
            pl.BlockSpec((conv_w.shape[0], HEAD_W), lambda b, j: (0, j)),
            pl.BlockSpec((1, HEAD_W), lambda b, j: (0, j)),
        ],
        out_specs=pl.BlockSpec((1, rows, HEAD_W), lambda b, j: (b, 0, j)),
        out_shape=jax.ShapeDtypeStruct((nb, rows, nc), F32),
        compiler_params=_cparams(("parallel", "parallel")),
        name="ssd_conv",
    )(y3, conv_w, conv_b.reshape(1, nc))


def _ssd_body(n_ctx_chunks, n_chunks, x_ref, bm_ref, cm_ref, z_ref, misc_ref, sel_ref, par_ref,
              par2_ref, o_ref, st_ref, ob_ref):
    C = SSD_CHUNK
    P = SSD_DH
    rows = o_ref.shape[1]
    w = 4 * HEAD_W
    gw = 4 * P
    st_ref[...] = jnp.zeros(st_ref.shape, F32)
    ri = lax.broadcasted_iota(jnp.int32, (C, C), 0)
    ci = lax.broadcasted_iota(jnp.int32, (C, C), 1)
    masks = (ri >= ci, ri <= ci)
    tri = tuple(mk.astype(BF16) for mk in masks)

    def narrow(a):
        return jnp.concatenate([a[:, hh * HEAD_W:hh * HEAD_W + P] for hh in range(4)], axis=-1)

    def chunk(d, g, r0, misc2, out_ref):
        rs = pl.ds(r0, C)
        x = x_ref[0, rs, g * gw:(g + 1) * gw]
        bmat = bm_ref[0, rs, g * HEAD_W:(g + 1) * HEAD_W]
        cb16 = cm_ref[0, rs, g * HEAD_W:(g + 1) * HEAD_W].astype(BF16)
        b16 = bmat.astype(BF16)
        bt16 = bmat.T.astype(BF16)
        scores = lax.dot_general(cb16, b16, NT_DIMS, preferred_element_type=F32)
        dd = jnp.dot(misc2, sel_ref[g, d], preferred_element_type=F32)
        dt = dd[:C] + dd[C:] + par_ref[g, d:d + 1, :]
        dt = jnp.maximum(dt, 0.0) + jnp.log(1.0 + jnp.exp(-jnp.abs(dt)))
        la = dt * par_ref[g, 2 + d:3 + d, :]
        lh, ll = _split_hi_lo(la)
        cc = jnp.dot(tri[d], jnp.concatenate([lh, ll], axis=-1), preferred_element_type=F32)
        cum = cc[:, :w] + cc[:, w:]
        last = cum[C - 1:C, :] if d == 0 else cum[0:1, :]
        xdt = x * narrow(dt)
        s_t = st_ref[2 * g + d]
        y_in = jnp.dot(cb16, s_t.astype(BF16), preferred_element_type=F32) * narrow(jnp.exp(cum))
        xw = (xdt * narrow(jnp.exp(last - cum))).astype(BF16)
        st_ref[2 * g + d] = s_t * narrow(jnp.exp(last)) + jnp.dot(bt16, xw, preferred_element_type=F32)
        xdt16 = xdt.astype(BF16)
        ys = []
        for hh in range(4):
            cum_h = cum[:, hh * HEAD_W:(hh + 1) * HEAD_W]
            seg = jnp.where(masks[d], cum_h - cum_h.T, 0.0)
            dec = jnp.where(masks[d], jnp.exp(seg), 0.0)
            ys.append(jnp.dot((scores * dec).astype(BF16), xdt16[:, hh * P:(hh + 1) * P],
                              preferred_element_type=F32))
        out_ref[rs, g * gw:(g + 1) * gw] = jnp.concatenate(ys, axis=-1) + y_in

    def step(t, carry):
        cf = t
        cb = jnp.where(t < n_ctx_chunks, n_ctx_chunks - 1 - t, n_chunks + n_ctx_chunks - 1 - t)
        for d, c, out_ref in ((0, cf, o_ref.at[0]), (1, cb, ob_ref)):
            r0 = pl.multiple_of(c * C, C)
            mh, ml = _split_hi_lo(misc_ref[0, pl.ds(r0, C), :])
            misc2 = jnp.concatenate([mh, ml], axis=0)
            for g in range(2):
                chunk(d, g, r0, misc2, out_ref)
        return carry

    lax.fori_loop(0, n_chunks, step, 0)

    blk = 256

    def post(i, carry):
        rs = pl.ds(pl.multiple_of(i * blk, blk), blk)
        for g in range(2):
            cols = slice(g * gw, (g + 1) * gw)
            y = o_ref[0, rs, cols] + ob_ref[rs, cols] + par2_ref[g, 0:1, :] * x_ref[0, rs, cols]
            o_ref[0, rs, cols] = _rms(y * _silu(z_ref[0, rs, cols]), par2_ref[g, 1:2, :])
        return carry

    lax.fori_loop(0, rows // blk, post, 0)


def _ssd_scan(xbc, y3, sel, par, par2):
    nb, rows, _ = xbc.shape
    gw = 4 * SSD_DH
    full = lambda a: pl.BlockSpec(a.shape, lambda b: (0,) * a.ndim)
    return pl.pallas_call(
        functools.partial(_ssd_body, CTX // SSD_CHUNK, rows // SSD_CHUNK),
        grid=(nb,),
        in_specs=[
            pl.BlockSpec((1, rows, 2 * gw), lambda b: (b, 0, 0)),
            pl.BlockSpec((1, rows, 2 * HEAD_W), lambda b: (b, 0, gw // HEAD_W)),
            pl.BlockSpec((1, rows, 2 * HEAD_W), lambda b: (b, 0, gw // HEAD_W + 1)),
            pl.BlockSpec((1, rows, 2 * gw), lambda b: (b, 0, OD_Z // (2 * gw))),
            pl.BlockSpec((1, rows, HEAD_W), lambda b: (b, 0, OD_MISC // HEAD_W)),
            full(sel), full(par), full(par2),
        ],
        out_specs=pl.BlockSpec((1, rows, 2 * gw), lambda b: (b, 0, 0)),
        out_shape=jax.ShapeDtypeStruct((nb, rows, 2 * gw), F32),
        scratch_shapes=[pltpu.VMEM((4, HEAD_W, gw), F32), pltpu.VMEM((rows, 2 * gw), F32)],
        compiler_params=_cparams(("parallel",)),
        name="ssd_scan",
    )(xbc, xbc, xbc, y3, y3, sel, par, par2)


def _outproj_body(h_ref, a_ref, b_ref, g1_ref, sh2_ref, sc2_ref, wo_ref, lng_ref, lnb_ref, wq_ref,
                  sk_ref, h1_ref, hm_ref, st_ref):
    mix = jnp.concatenate([a_ref[...], b_ref[...]], axis=-1).astype(BF16)
    o = jnp.dot(mix, wo_ref[...], preferred_element_type=F32)
    h1 = _layer_norm(DN_ALPHA * h_ref[...] + g1_ref[0] * o, lng_ref[...], lnb_ref[...])
    h1_ref[...] = h1
    hm = (h1 * (1.0 + sc2_ref[0]) + sh2_ref[0]).astype(BF16)
    hm_ref[...] = hm
    qry = jnp.dot(hm, wq_ref[...], preferred_element_type=F32).astype(BF16)
    for j in range(2 * PEER_HEADS):
        qj = qry[:, j * PEER_NKEYS:(j + 1) * PEER_NKEYS]
        st_ref[j * PEER_NKEYS:(j + 1) * PEER_NKEYS, :] = lax.dot_general(
            sk_ref[j], qj, NT_DIMS, preferred_element_type=F32)


def _outproj(h, mix_a, mix_b, modl, w_out, ln_g, ln_b, wq, sk, n_batch, latent_only=False):
    n = h.shape[0]
    tpb = n // n_batch // TM
    if latent_only:
        lat = tpb - 1
        n_out = n_batch * lat * TM
        src = lambda j: (j // lat) * tpb + 1 + j % lat
        mod = lambda k: pl.BlockSpec((1, 1, D), lambda j: (k * MOD_ROWS + j // lat, 0, 0))
    else:
        n_out = n
        src = lambda j: j
        mod = lambda k: _mod_spec(k, n_batch, tpb)
    tok_in = pl.BlockSpec((TM, D), lambda j: (src(j), 0))
    half = pl.BlockSpec((TM, D // 2), lambda j: (src(j), 0))
    tok = pl.BlockSpec((TM, D), lambda j: (j, 0))
    vec = pl.BlockSpec((1, D), lambda j: (0, 0))
    nq = wq.shape[1]
    return pl.pallas_call(
        _outproj_body,
        grid=(n_out // TM,),
        in_specs=[tok_in, half, half, mod(2), mod(3), mod(4), pl.BlockSpec((D, D), lambda j: (0, 0)), vec, vec,
                  pl.BlockSpec((D, nq), lambda j: (0, 0)),
                  pl.BlockSpec(sk.shape, lambda j: (0, 0, 0))],
        out_specs=[tok, tok, pl.BlockSpec((nq, TM), lambda j: (0, j))],
        out_shape=[jax.ShapeDtypeStruct((n_out, D), F32), jax.ShapeDtypeStruct((n_out, D), BF16),
                   jax.ShapeDtypeStruct((nq, n_out), F32)],
        compiler_params=_cparams(("parallel",)),
        name="outproj",
    )(h, mix_a, mix_b, modl, modl, modl, w_out, ln_g, ln_b, wq, sk)


def _sort16_network():
    comps = []

    def merge(lo, hi, r):
        step = r * 2
        if step < hi - lo:
            merge(lo, hi, step)
            merge(lo + r, hi, step)
            comps.extend((j, j + r) for j in range(lo + r, hi - r, step))
        else:
            comps.append((lo, lo + r))

    def sort(lo, hi):
        if hi - lo >= 1:
            mid = lo + (hi - lo) // 2
            sort(lo, mid)
            sort(mid + 1, hi)
            merge(lo, hi, 1)

    sort(0, 15)
    return comps


SORT16 = _sort16_network()
BITONIC16 = [(j, j + s) for s in (8, 4, 2, 1) for j in range(16) if not j & s]


def _compare_exchange(v, comps):
    v = list(v)
    for a, b in comps:
        v[a], v[b] = jnp.maximum(v[a], v[b]), jnp.minimum(v[a], v[b])
    return v


def _merge_top16(a, b_rev):
    c = [a[k] if b_rev[k] is None else jnp.maximum(a[k], b_rev[k]) for k in range(16)]
    return _compare_exchange(c, BITONIC16)


def _top16_keys(ref, row0, col):
    v = [ref[row0 + 8 * j:row0 + 8 * j + 8, col] for j in range(16)]
    v = _compare_exchange(v, SORT16)
    for shift in (4, 2, 1):
        rolled = [pltpu.roll(x, shift, 0) for x in v]
        v = _merge_top16(v, rolled[::-1])
    return v


def _stats_body(st_ref, o_ref):
    k = PEER_TOPK
    nk = PEER_NKEYS
    sub = lax.broadcasted_iota(jnp.int32, (8, 128), 0)

    def column(tc, carry):
        col = pl.ds(pl.multiple_of(tc * 128, 128), 128)
        tops = []
        for side in range(2):
            packed = None
            for h in range(PEER_HEADS):
                a = _top16_keys(st_ref, (2 * h + side) * nk, col)
                packed = a if h == 0 else [jnp.where(sub == h, a[r], packed[r]) for r in range(k)]
            tops.append(packed)
        t1, t2 = tops
        best = [t1[0] + t2[r] for r in range(k)]
        for r1 in range(1, k):
            ln = k // (r1 + 1)
            lst = [t1[r1] + t2[r] for r in range(ln)]
            best = _merge_top16(best, [None] * (k - ln) + lst[::-1])
        z = 1.0
        for r in range(1, k):
            z = z + jnp.exp(best[r] - best[0])
        o_ref[0, :, col] = t1[0]
        o_ref[1, :, col] = t2[0] + jnp.log(z)
        o_ref[2, :, col] = best[k - 1]
        for r in range(k):
            o_ref[3 + r, :, col] = t2[r]
        return carry

    lax.fori_loop(0, o_ref.shape[2] // 128, column, 0)


def _peer_stats(st):
    n = st.shape[1]
    ts = 512
    return pl.pallas_call(
        _stats_body,
        grid=(n // ts,),
        in_specs=[pl.BlockSpec((st.shape[0], ts), lambda i: (0, i))],
        out_specs=pl.BlockSpec((3 + PEER_TOPK, PEER_HEADS, ts), lambda i: (0, 0, i)),
        out_shape=jax.ShapeDtypeStruct((3 + PEER_TOPK, PEER_HEADS, n), F32),
        compiler_params=_cparams(("parallel",)),
        name="peer_stats",
    )(st)


def _count_prefix(pred, rows):
    pick = lambda a, b, c: jnp.where(c, b, a)
    c8 = pred(rows[7])
    c4 = pred(pick(rows[3], rows[11], c8))
    c2 = pred(pick(pick(rows[1], rows[5], c4), pick(rows[9], rows[13], c4), c8))
    ev = [rows[2 * j] for j in range(8)]
    c1 = pred(pick(pick(pick(ev[0], ev[1], c2), pick(ev[2], ev[3], c2), c4),
                   pick(pick(ev[4], ev[5], c2), pick(ev[6], ev[7], c2), c4), c8))
    c16 = pred(rows[15])
    val = lambda c, v: jnp.where(c, v, 0.0)
    return val(c8, 8.0) + val(c4, 4.0) + val(c2, 2.0) + val(c1, 1.0) + val(c16, 1.0)


def _expert_body(hm_ref, st_ref, stats_ref, u_ref, vt_ref, y_ref, e1_ref, n1_ref, e2_ref, r2_ref,
                 a_ref, p_ref, acc_ref, hmt_ref):
    e = pl.program_id(1)
    ne = pl.num_programs(1)
    t = hm_ref.shape[0]
    nk = PEER_NKEYS
    k = PEER_TOPK
    m = PEER_EB // nk
    ntc = t // 128

    @pl.when(e == 0)
    def _():
        def prep(tc, carry):
            col = pl.ds(pl.multiple_of(tc * 128, 128), 128)
            for h in range(PEER_HEADS):
                rows = slice(h * nk, (h + 1) * nk)
                s1 = st_ref[2 * h * nk:(2 * h + 1) * nk, col]
                s2 = st_ref[(2 * h + 1) * nk:(2 * h + 2) * nk, col]
                tau = stats_ref[2, h:h + 1, col]
                t2 = [stats_ref[3 + r, h:h + 1, col] for r in range(k)]
                n1 = _count_prefix(lambda thr: s1 + thr >= tau, t2)
                r2 = _count_prefix(lambda thr: thr > s2, t2)
                e1_ref[rows, col] = jnp.exp(s1 - stats_ref[0, h:h + 1, col])
                n1_ref[rows, col] = n1
                e2_ref[rows, col] = jnp.exp(s2 - stats_ref[1, h:h + 1, col]).astype(BF16)
                r2_ref[rows, col] = r2.astype(BF16)
            return carry

        lax.fori_loop(0, ntc, prep, 0)
        acc_ref[...] = jnp.zeros(acc_ref.shape, F32)
        hmt_ref[...] = hm_ref[...].astype(F32).T.astype(BF16)

    a_ref[...] = jnp.dot(u_ref[...], hmt_ref[...], preferred_element_type=F32)

    cw = PEER_CW

    def build(tc, carry):
        col = pl.ds(pl.multiple_of(tc * cw, cw), cw)
        own = [pl.ds(pl.multiple_of(h * nk + e * m, m), m) for h in range(PEER_HEADS)]
        e1 = [e1_ref[own[h], col].astype(BF16) for h in range(PEER_HEADS)]
        n1 = [n1_ref[own[h], col].astype(BF16) for h in range(PEER_HEADS)]
        zero = jnp.zeros((), BF16)
        group = 2
        for g0 in range(0, m, group):
            gates = [jnp.zeros((nk, cw), BF16) for _ in range(group)]
            for h in range(PEER_HEADS):
                rows2 = slice(h * nk, (h + 1) * nk)
                r2 = r2_ref[rows2, col]
                e2 = e2_ref[rows2, col]
                for j in range(group):
                    n1b = jnp.broadcast_to(n1[h][g0 + j:g0 + j + 1, :], (nk, cw))
                    e1b = jnp.broadcast_to(e1[h][g0 + j:g0 + j + 1, :], (nk, cw))
                    gates[j] = gates[j] + e1b * jnp.where(r2 < n1b, e2, zero)
            for j in range(group):
                rows = slice((g0 + j) * nk, (g0 + j + 1) * nk)
                act = jax.nn.gelu(a_ref[rows, col], approximate=True)
                p_ref[rows, col] = act.astype(BF16) * gates[j]
        return carry

    lax.fori_loop(0, t // cw, build, 0)
    acc_ref[...] += jnp.dot(vt_ref[...], p_ref[...], preferred_element_type=F32)

    @pl.when(e == ne - 1)
    def _():
        y_ref[...] = acc_ref[...].T


def _peer_experts(hm, st, stats, u, vt):
    n = hm.shape[0]
    ne = u.shape[0] // PEER_EB
    assert (PEER_EB // PEER_NKEYS) % 8 == 0
    t = PEER_T
    half = (PEER_HEADS * PEER_NKEYS, t)
    return pl.pallas_call(
        _expert_body,
        grid=(n // t, ne),
        in_specs=[
            pl.BlockSpec((t, D), lambda i, e: (i, 0)),
            pl.BlockSpec((st.shape[0], t), lambda i, e: (0, i)),
            pl.BlockSpec((stats.shape[0], PEER_HEADS, t), lambda i, e: (0, 0, i)),
            pl.BlockSpec((PEER_EB, D), lambda i, e: (e, 0)),
            pl.BlockSpec((D, PEER_EB), lambda i, e: (0, e)),
        ],
        out_specs=pl.BlockSpec((t, D), lambda i, e: (i, 0)),
        out_shape=jax.ShapeDtypeStruct((n, D), F32),
        scratch_shapes=[pltpu.VMEM(half, F32), pltpu.VMEM(half, F32), pltpu.VMEM(half, BF16),
                        pltpu.VMEM(half, BF16), pltpu.VMEM((PEER_EB, t), F32),
                        pltpu.VMEM((PEER_EB, t), BF16), pltpu.VMEM((D, t), F32),
                        pltpu.VMEM((D, t), BF16)],
        compiler_params=_cparams(("parallel", "arbitrary")),
        name="peer_experts",
    )(hm, st, stats, u, vt)


def _rope_tables(rows_lat):
    pairs = HEAD_W // 4
    t = jnp.arange(rows_lat)
    inv = ROPE_THETA ** (-jnp.arange(pairs, dtype=F32) / pairs)
    ang = jnp.concatenate([(t // GRID_W).astype(F32)[:, None] * inv,
                           (t % GRID_W).astype(F32)[:, None] * inv], axis=-1)
    cos, sin = jnp.cos(ang), jnp.sin(ang)
    zero = jnp.zeros_like(sin)
    c = jnp.stack([cos, cos], -1).reshape(rows_lat, HEAD_W)
    sa = jnp.stack([-sin, zero], -1).reshape(rows_lat, HEAD_W)
    sb = jnp.stack([zero, sin], -1).reshape(rows_lat, HEAD_W)
    ident = jnp.ones((CTX, HEAD_W), F32)
    none = jnp.zeros((CTX, HEAD_W), F32)
    return (jnp.concatenate([ident, c]), jnp.concatenate([none, sa]), jnp.concatenate([none, sb]))


def _odd_column_map():
    src = -np.ones((OD_W,), np.int64)
    for h in range(4):
        src[OD_Q + h * HEAD_W:OD_Q + h * HEAD_W + GLA_DK] = np.arange(h * GLA_DK, (h + 1) * GLA_DK)
        src[OD_K + h * HEAD_W:OD_K + h * HEAD_W + GLA_DK] = 256 + np.arange(h * GLA_DK, (h + 1) * GLA_DK)
    src[OD_V:OD_V + 512] = 512 + np.arange(512)
    src[OD_G:OD_G + 512] = 1024 + np.arange(512)
    src[OD_Z:OD_Z + 512] = 1568 + np.arange(512)
    src[OD_XS:OD_XS + 512] = 2080 + np.arange(512)
    src[OD_BM:OD_BM + 256] = 2592 + np.arange(256)
    src[OD_CM:OD_CM + 256] = 2848 + np.arange(256)
    src[OD_MISC + MISC_LRF:OD_MISC + MISC_LRF + 16] = 1536 + np.arange(16)
    src[OD_MISC + MISC_LRB:OD_MISC + MISC_LRB + 16] = 1552 + np.arange(16)
    src[OD_MISC + MISC_DTF:OD_MISC + MISC_DTF + 8] = 3104 + np.arange(8)
    src[OD_MISC + MISC_DTB:OD_MISC + MISC_DTB + 8] = 3112 + np.arange(8)
    return src


def _relayout_odd_w(w):
    src = _odd_column_map()
    cols = jnp.take(w, jnp.asarray(np.maximum(src, 0)), axis=1)
    return jnp.where(jnp.asarray(src >= 0)[None, :], cols, 0.0)


def _pad_rows(rows_list, width):
    out = [jnp.broadcast_to(jnp.asarray(r, F32).reshape(1, width), (1, width)) for r in rows_list]
    out += [jnp.zeros((1, width), F32)] * (8 - len(out))
    return jnp.concatenate(out, axis=0)


def _hgrn_params(lb, gain):
    la = jnp.log(jnp.maximum(lb, LB_FLOOR))
    l1 = jnp.log1p(-lb)
    oml = 1.0 - lb
    return _pad_rows([la[0], la[1], l1[0], l1[1], oml[0], oml[1], jnp.tile(gain, 4)], 4 * HEAD_W)


def _gla_params(gate_w, gate_b, gain):
    wg = jnp.zeros((2, HEAD_W, 4 * HEAD_W), F32)
    gb = jnp.zeros((2, 4 * HEAD_W), F32)
    for h in range(4):
        s = slice(h * GLA_DK, (h + 1) * GLA_DK)
        dst = slice(h * HEAD_W, h * HEAD_W + GLA_DK)
        for d, off in enumerate((MISC_LRF, MISC_LRB)):
            wg = wg.at[d, off:off + 16, dst].set(gate_w[d][:, s])
            gb = gb.at[d, dst].set(gate_b[d, s])
    return wg.astype(BF16), _pad_rows([gb[0], gb[1], jnp.tile(gain, 4)], 4 * HEAD_W)


def _ssd_params(dt_bias, a_log, d_skip, norm_g):
    sel = np.zeros((2, 2, HEAD_W, 4 * HEAD_W), np.float32)
    for g in range(2):
        for d, off in enumerate((MISC_DTF, MISC_DTB)):
            for hh in range(4):
                sel[g, d, off + 4 * g + hh, hh * HEAD_W:(hh + 1) * HEAD_W] = 1.0
    par, par2 = [], []
    neg_a = -jnp.exp(a_log.astype(F32))
    for g in range(2):
        hs = slice(4 * g, 4 * g + 4)
        rep = lambda v: jnp.repeat(v[hs], HEAD_W)
        par.append(_pad_rows([rep(dt_bias[0]), rep(dt_bias[1]), rep(neg_a[0]), rep(neg_a[1])], 4 * HEAD_W))
        par2.append(_pad_rows([jnp.repeat(d_skip[hs], SSD_DH), norm_g[g * 256:(g + 1) * 256]], 4 * SSD_DH))
    return jnp.asarray(sel, BF16), jnp.stack(par), jnp.stack(par2)


def kernel(x, c, ctx, c_ctx, mod_w, mod_b, ln_g, ln_b, ev_w_in, ev_w_out, hg_lb_logits, hg_norm_g, at_q_norm_g, at_k_norm_g, od_w_in, od_w_out, gla_gate_w, gla_gate_b, gla_norm_g, ssd_conv_w, ssd_conv_b, ssd_dt_bias, ssd_a_log, ssd_d, ssd_norm_g, peer_wq, peer_subkeys, peer_u, peer_v):
    nb, seq, _ = x.shape
    assert ctx.shape[1] == CTX and seq % TM == 0 and nb + 1 <= MOD_ROWS
    rows = CTX + seq
    n = nb * rows

    cc = jnp.concatenate([c, c_ctx[None, :], jnp.zeros((MOD_ROWS - nb - 1, D), F32)], axis=0)
    mods = _modulation(cc, mod_w, mod_b)
    mods = mods.reshape(DEPTH, MOD_ROWS, N_MOD, D).transpose(0, 2, 1, 3).reshape(DEPTH, N_MOD * MOD_ROWS, 1, D)

    sm = jax.nn.softmax(hg_lb_logits.astype(F32), axis=1)
    hg_lb = jnp.cumsum(sm, axis=1) - sm[:, :1]
    tables = _rope_tables(seq)

    h = jnp.concatenate([ctx, x], axis=1).reshape(n, D)
    pre = None
    for l in range(DEPTH):
        j = l // 2
        modl = mods[l]
        if l % 2 == 0:
            y, h = _inproj(h, ev_w_in[j].astype(BF16), modl, nb, pre)
            y3 = y.reshape(nb, rows, EV_W)
            mix_a = _vscan("hgrn", y3, ((EV_Q, EV_I, EV_ZF), (EV_Q, EV_I, EV_ZB)), EV_G,
                           _hgrn_params(hg_lb[:, j], hg_norm_g[j]))
            mix_b = _attention(y3, tables, at_q_norm_g[j].reshape(1, HEAD_W), at_k_norm_g[j].reshape(1, HEAD_W))
            w_out = ev_w_out[j]
        else:
            y, h = _inproj(h, _relayout_odd_w(od_w_in[j]).astype(BF16), modl, nb, pre)
            y3 = y.reshape(nb, rows, OD_W)
            wg, gpar = _gla_params(gla_gate_w[j], gla_gate_b[j], gla_norm_g[j])
            mix_a = _vscan("gla", y3, ((OD_Q, OD_K, OD_V),) * 2, OD_G, gpar, wg)
            xbc = _ssd_conv(y3, ssd_conv_w[j], ssd_conv_b[j])
            sel, spar, spar2 = _ssd_params(ssd_dt_bias[j], ssd_a_log[j], ssd_d[j], ssd_norm_g[j])
            mix_b = _ssd_scan(xbc, y3, sel, spar, spar2)
            w_out = od_w_out[j]
        lg1, lb1 = ln_g[l, 0].reshape(1, D), ln_b[l, 0].reshape(1, D)
        sk = peer_subkeys[l].reshape(2 * PEER_HEADS, PEER_NKEYS, PEER_NKEYS).astype(BF16)
        h1, hm, st = _outproj(h, mix_a.reshape(n, D // 2), mix_b.reshape(n, D // 2), modl,
                              w_out.astype(BF16), lg1, lb1, peer_wq[l].astype(BF16), sk, nb,
                              latent_only=(l == DEPTH - 1))
        stats = _peer_stats(st)
        yp = _peer_experts(hm, st, stats, peer_u[l].astype(BF16), peer_v[l].astype(BF16).T)
        pre = (yp, modl, ln_g[l, 1].reshape(1, D), ln_b[l, 1].reshape(1, D))
        h = h1
    out = _final(h, pre[0], pre[1], pre[2], pre[3], nb)
    return out.reshape(nb, seq, D)
```

```python
import functools

import numpy as np
import jax
import jax.numpy as jnp
from jax import lax
from jax.experimental import pallas as pl
from jax.experimental.pallas import tpu as pltpu

F32 = jnp.float32
BF16 = jnp.bfloat16

D = 1024
DEPTH = 4
N_MOD = 6
CTX = 256
GRID_W = 64
DN_ALPHA = (2 * DEPTH) ** 0.25
NORM_EPS = 1e-6
ROPE_THETA = 10000.0
LB_FLOOR = 1e-30
TM = 256
MOD_ROWS = 24
VMEM_LIMIT = 56 * 1024 * 1024

HEAD_W = 128
VEC_CHUNK = 16
SSD_CHUNK = 128
SSD_DH = 64
GLA_DK = 64
GLA_GATE_NORM = 16.0
PEER_HEADS = 8
PEER_NKEYS = 128
PEER_TOPK = 16
PEER_T = 512
PEER_EB = 2048
PEER_CW = 512

EV_Q, EV_I, EV_ZF, EV_ZB, EV_G, EV_AQ, EV_AK, EV_AV = 0, 512, 1024, 1536, 2048, 2560, 3072, 3328
EV_W = 3584
OD_Q, OD_K, OD_V, OD_G, OD_Z, OD_XS, OD_BM, OD_CM, OD_MISC = 0, 512, 1024, 1536, 2048, 2560, 3072, 3328, 3584
OD_W = 3712
MISC_LRF, MISC_LRB, MISC_DTF, MISC_DTB = 0, 16, 32, 40

NT_DIMS = (((1,), (1,)), ((), ()))
TN_DIMS = (((0,), (0,)), ((), ()))


def _cparams(sem):
    return pltpu.CompilerParams(dimension_semantics=sem, vmem_limit_bytes=VMEM_LIMIT)


def _layer_norm(y, g, b):
    mu = jnp.mean(y, axis=-1, keepdims=True)
    yc = y - mu
    var = jnp.mean(yc * yc, axis=-1, keepdims=True)
    return yc * lax.rsqrt(var + NORM_EPS) * g + b


def _rms(y, g):
    return y * lax.rsqrt(jnp.mean(y * y, axis=-1, keepdims=True) + NORM_EPS) * g


def _log_sigmoid(z):
    return jnp.minimum(z, 0.0) - jnp.log(1.0 + jnp.exp(-jnp.abs(z)))


def _silu(z):
    return z * jax.nn.sigmoid(z)


def _split_hi_lo(a):
    hi = a.astype(BF16)
    lo = (a - hi.astype(F32)).astype(BF16)
    return hi, lo


def _mod_body(c_ref, w_ref, b_ref, o_ref):
    a = _silu(c_ref[...])
    a_hi, a_lo = _split_hi_lo(a)
    w_hi, w_lo = _split_hi_lo(w_ref[0])
    acc = jnp.dot(a_hi, w_hi, preferred_element_type=F32)
    acc += jnp.dot(a_lo, w_hi, preferred_element_type=F32)
    acc += jnp.dot(a_hi, w_lo, preferred_element_type=F32)
    o_ref[0] = acc + b_ref[0]


def _modulation(cc, mod_w, mod_b):
    tn = 1536
    nw = mod_w.shape[-1]
    return pl.pallas_call(
        _mod_body,
        grid=(DEPTH, nw // tn),
        in_specs=[
            pl.BlockSpec((MOD_ROWS, D), lambda l, j: (0, 0)),
            pl.BlockSpec((1, D, tn), lambda l, j: (l, 0, j)),
            pl.BlockSpec((1, 1, tn), lambda l, j: (l, 0, j)),
        ],
        out_specs=pl.BlockSpec((1, MOD_ROWS, tn), lambda l, j: (l, 0, j)),
        out_shape=jax.ShapeDtypeStruct((DEPTH, MOD_ROWS, nw), F32),
        compiler_params=_cparams(("parallel", "parallel")),
        name="modulation",
    )(cc, mod_w, mod_b.reshape(DEPTH, 1, nw))


def _mod_spec(k, n_batch, tiles_per_batch):
    def index(i):
        row = jnp.where(i % tiles_per_batch == 0, n_batch, i // tiles_per_batch)
        return (k * MOD_ROWS + row, 0, 0)
    return pl.BlockSpec((1, 1, D), index)


def _inproj_body(has_pre, *refs):
    if has_pre:
        x_ref, y_ref, g2_ref, lng_ref, lnb_ref, sh_ref, sc_ref, w_ref, o_ref, h_ref = refs
        h = _layer_norm(DN_ALPHA * x_ref[...] + g2_ref[0] * y_ref[...], lng_ref[...], lnb_ref[...])
        h_ref[...] = h
    else:
        x_ref, sh_ref, sc_ref, w_ref, o_ref = refs
        h = x_ref[...]
    u = h * (1.0 + sc_ref[0]) + sh_ref[0]
    o_ref[...] = jnp.dot(u.astype(BF16), w_ref[...], preferred_element_type=F32)


def _inproj(h, w, modl, n_batch, pre=None):
    n = h.shape[0]
    nw = w.shape[1]
    tpb = n // n_batch // TM
    tok = pl.BlockSpec((TM, D), lambda i: (i, 0))
    vec = pl.BlockSpec((1, D), lambda i: (0, 0))
    in_specs = [tok]
    args = [h]
    if pre is not None:
        y, prev_modl, ln_g, ln_b = pre
        in_specs += [tok, _mod_spec(5, n_batch, tpb), vec, vec]
        args += [y, prev_modl, ln_g, ln_b]
    in_specs += [_mod_spec(0, n_batch, tpb), _mod_spec(1, n_batch, tpb),
                 pl.BlockSpec((D, nw), lambda i: (0, 0))]
    args += [modl, modl, w]
    out_specs = [pl.BlockSpec((TM, nw), lambda i: (i, 0))]
    out_shape = [jax.ShapeDtypeStruct((n, nw), F32)]
    if pre is not None:
        out_specs.append(tok)
        out_shape.append(jax.ShapeDtypeStruct((n, D), F32))
    res = pl.pallas_call(
        functools.partial(_inproj_body, pre is not None),
        grid=(n // TM,),
        in_specs=in_specs,
        out_specs=out_specs,
        out_shape=out_shape,
        compiler_params=_cparams(("parallel",)),
        name="inproj",
    )(*args)
    return (res[0], res[1]) if pre is not None else (res[0], h)


def _final_body(x_ref, y_ref, g2_ref, lng_ref, lnb_ref, o_ref):
    o_ref[...] = _layer_norm(DN_ALPHA * x_ref[...] + g2_ref[0] * y_ref[...], lng_ref[...], lnb_ref[...])


def _final(h, y, modl, ln_g, ln_b, n_batch):
    n = h.shape[0]
    lat = n // n_batch // TM
    tok = pl.BlockSpec((TM, D), lambda b, j: (b * lat + j, 0))
    vec = pl.BlockSpec((1, D), lambda b, j: (0, 0))
    return pl.pallas_call(
        _final_body,
        grid=(n_batch, lat),
        in_specs=[tok, tok, pl.BlockSpec((1, 1, D), lambda b, j: (5 * MOD_ROWS + b, 0, 0)), vec, vec],
        out_specs=tok,
        out_shape=jax.ShapeDtypeStruct((n, D), F32),
        compiler_params=_cparams(("parallel", "parallel")),
        name="final_norm",
    )(h, y, modl, ln_g, ln_b)


def _vscan_body(mode, nblk, *refs):
    C = VEC_CHUNK
    nh = 4
    if mode == "hgrn":
        qf_ref, vf_ref, zf_ref, qb_ref, vb_ref, zb_ref, g_ref, par_ref, o_ref, st_ref, ob_ref = refs
        dir_refs = ((qf_ref, vf_ref, zf_ref), (qb_ref, vb_ref, zb_ref))
    else:
        (qf_ref, kf_ref, vf_ref, mf_ref, qb_ref, kb_ref, vb_ref, mb_ref, g_ref, wg_ref, par_ref,
         o_ref, st_ref, ob_ref) = refs
        dir_refs = ((qf_ref, kf_ref, vf_ref, mf_ref), (qb_ref, kb_ref, vb_ref, mb_ref))
    rows = o_ref.shape[1]
    rb = qf_ref.shape[1]
    i = pl.program_id(1)
    bi = jnp.where(i == 0, 0, nblk - i)

    @pl.when(i == 0)
    def _():
        st_ref[...] = jnp.zeros(st_ref.shape, F32)

    w = nh * HEAD_W
    ri = lax.broadcasted_iota(jnp.int32, (rb, rb), 0)
    ci = lax.broadcasted_iota(jnp.int32, (rb, rb), 1)
    rid = lax.broadcasted_iota(jnp.int32, (rb, w), 0)
    levels = [C << s for s in range((rb // C).bit_length() - 1)]
    neg = -1e30

    def group_row(x, grp, r):
        g = x.reshape(rb // grp, grp, w)
        return jnp.broadcast_to(g[:, r:r + 1, :], g.shape).reshape(rb, w)

    def nt(a, b):
        return lax.dot_general(a, b, NT_DIMS, preferred_element_type=F32)

    def same_group(grp):
        sh = grp.bit_length() - 1
        return (ri >> sh) == (ci >> sh)

    def direction(d, out_ref, row0):
        refs_d = dir_refs[d]
        q = refs_d[0][0]
        v16 = refs_d[-2 if mode == "gla" else 1][0].astype(BF16)
        if mode == "hgrn":
            z = refs_d[2][0]
            qh = _silu(q) * (HEAD_W ** -0.5)
            la, l1, oml = par_ref[d:d + 1, :], par_ref[2 + d:3 + d, :], par_ref[4 + d:5 + d, :]
            bb = l1 + _log_sigmoid(z)
            log_f = jnp.maximum(la, bb) + jnp.log(1.0 + jnp.exp(-jnp.abs(la - bb)))
            k = oml * jax.nn.sigmoid(-z)
        else:
            qh = q * (GLA_DK ** -0.5)
            k = refs_d[1][0]
            pre = jnp.dot(refs_d[3][0].astype(BF16), wg_ref[d], preferred_element_type=F32)
            log_f = _log_sigmoid(pre + par_ref[d:d + 1, :]) * (1.0 / GLA_GATE_NORM)
        tri = (ri >= ci if d == 0 else ri <= ci).astype(BF16)
        hi, lo = _split_hi_lo(log_f)
        cc = jnp.dot(tri, jnp.concatenate([hi, lo], axis=-1), preferred_element_type=F32)
        cum = cc[:, :w] + cc[:, w:]
        first_row, last_row = (0, rb - 1) if d == 0 else (rb - 1, 0)

        x0 = cum - group_row(cum, C, first_row % C)
        qs = [(qh * jnp.exp(x0)).astype(BF16)]
        ks = [(k * jnp.exp(-x0)).astype(BF16)]
        for m in levels:
            is_query = (rid & m) != 0 if d == 0 else (rid & m) == 0
            bnd = group_row(cum, 2 * m, m - 1 if d == 0 else m)
            qs.append((qh * jnp.exp(jnp.where(is_query, cum - bnd, neg))).astype(BF16))
            ks.append((k * jnp.exp(jnp.where(is_query, neg, bnd - cum))).astype(BF16))
        last = cum[last_row:last_row + 1, :]
        qe = (qh * jnp.exp(cum)).astype(BF16)
        kl = (k * jnp.exp(last - cum)).astype(BF16)
        dec = jnp.exp(last)

        outs = []
        for hh in range(nh):
            sl = slice(hh * HEAD_W, (hh + 1) * HEAD_W)
            p0 = nt(qs[0][:, sl], ks[0][:, sl])
            causal = ri >= ci if d == 0 else ri <= ci
            att = jnp.where(same_group(C), jnp.where(causal, p0, 0.0), 0.0)
            for lv, m in enumerate(levels):
                p = nt(qs[lv + 1][:, sl], ks[lv + 1][:, sl])
                att += p if 2 * m == rb else jnp.where(same_group(2 * m), p, 0.0)
            o = jnp.dot(att.astype(BF16), v16[:, sl], preferred_element_type=F32)
            s_t = st_ref[d * nh + hh]
            o += nt(qe[:, sl], s_t.astype(BF16))
            kv_t = lax.dot_general(v16[:, sl], kl[:, sl], TN_DIMS, preferred_element_type=F32)
            st_ref[d * nh + hh] = s_t * dec[:, sl] + kv_t
            outs.append(o)
        out_ref[row0, :] = jnp.concatenate(outs, axis=-1)

    direction(0, o_ref.at[0], pl.ds(pl.multiple_of(i * rb, rb), rb))
    direction(1, ob_ref, pl.ds(pl.multiple_of(bi * rb, rb), rb))

    @pl.when(i == nblk - 1)
    def _():
        def post(j, carry):
            rs = pl.ds(pl.multiple_of(j * rb, rb), rb)
            for hh in range(nh):
                lanes = pl.ds(hh * HEAD_W, HEAD_W)
                o = o_ref[0, rs, lanes] + ob_ref[rs, lanes]
                g = g_ref[0, rs, lanes]
                if mode == "hgrn":
                    res = _rms(o * jax.nn.sigmoid(g), par_ref[6:7, lanes])
                else:
                    res = _rms(o, par_ref[2:3, lanes]) * _silu(g)
                o_ref[0, rs, lanes] = res
            return carry

        lax.fori_loop(0, rows // rb, post, 0)


def _vscan(mode, y3, col_offsets, g_offset, params, wg=None):
    nb, rows, _ = y3.shape
    width = 4 * HEAD_W
    rb = CTX
    nblk = rows // rb
    fwd = lambda w, off: pl.BlockSpec((1, rb, w), lambda b, i: (b, i, off // w))
    bwd = lambda w, off: pl.BlockSpec((1, rb, w), lambda b, i: (b, jnp.where(i == 0, 0, nblk - i), off // w))
    in_specs, args = [], []
    for mk in (fwd, bwd):
        for off in col_offsets[mk is bwd]:
            in_specs.append(mk(width, off))
            args.append(y3)
        if mode == "gla":
            in_specs.append(mk(HEAD_W, OD_MISC))
            args.append(y3)
    in_specs.append(pl.BlockSpec((1, rows, width), lambda b, i: (b, 0, g_offset // width)))
    args.append(y3)
    if mode == "gla":
        in_specs.append(pl.BlockSpec((2, HEAD_W, width), lambda b, i: (0, 0, 0)))
        args.append(wg)
    in_specs.append(pl.BlockSpec((8, width), lambda b, i: (0, 0)))
    args.append(params)
    return pl.pallas_call(
        functools.partial(_vscan_body, mode, nblk),
        grid=(nb, nblk),
        in_specs=in_specs,
        out_specs=pl.BlockSpec((1, rows, width), lambda b, i: (b, 0, 0)),
        out_shape=jax.ShapeDtypeStruct((nb, rows, width), F32),
        scratch_shapes=[pltpu.VMEM((8, HEAD_W, HEAD_W), F32), pltpu.VMEM((rows, width), F32)],
        compiler_params=_cparams(("parallel", "arbitrary")),
        name="vscan_" + mode,
    )(*args)


def _rope(x, c, sa, sb):
    return x * c + pltpu.roll(x, HEAD_W - 1, 1) * sa + pltpu.roll(x, 1, 1) * sb


def _attn_body(q_ref, k_ref, v_ref, ck_ref, sak_ref, sbk_ref, cq_ref, saq_ref, sbq_ref,
               qg_ref, kg_ref, o_ref, kp_ref, vp_ref):
    qi = pl.program_id(2)
    rows = k_ref.shape[1]
    blk = 256

    @pl.when(qi == 0)
    def _():
        def prep(i, carry):
            rs = pl.ds(pl.multiple_of(i * blk, blk), blk)
            kn = _rms(k_ref[0, rs, :], kg_ref[...])
            kp_ref[rs, :] = _rope(kn, ck_ref[rs, :], sak_ref[rs, :], sbk_ref[rs, :]).astype(BF16)
            vp_ref[rs, :] = v_ref[0, rs, :].astype(BF16)
            return carry
        lax.fori_loop(0, rows // blk, prep, 0)

    def attend(nk):
        outs = []
        for r in range(2):
            qn = _rms(q_ref[0, :, r * HEAD_W:(r + 1) * HEAD_W], qg_ref[...])
            qr = (_rope(qn, cq_ref[...], saq_ref[...], sbq_ref[...]) * (HEAD_W ** -0.5)).astype(BF16)
            s = lax.dot_general(qr, kp_ref[0:nk, :], NT_DIMS, preferred_element_type=F32)
            m = jnp.max(s, axis=-1, keepdims=True)
            p = jnp.exp(s - m)
            l = jnp.sum(p, axis=-1, keepdims=True)
            o = jnp.dot(p.astype(BF16), vp_ref[0:nk, :], preferred_element_type=F32)
            outs.append(o / l)
        o_ref[0] = jnp.concatenate(outs, axis=-1)

    @pl.when(qi == 0)
    def _():
        attend(CTX)

    @pl.when(qi > 0)
    def _():
        attend(rows)


def _attention(y3, tables, q_gain, k_gain):
    nb, rows, _ = y3.shape
    c, sa, sb = tables
    full = pl.BlockSpec((rows, HEAD_W), lambda b, g, i: (0, 0))
    tile = pl.BlockSpec((TM, HEAD_W), lambda b, g, i: (i, 0))
    vec = pl.BlockSpec((1, HEAD_W), lambda b, g, i: (0, 0))
    return pl.pallas_call(
        _attn_body,
        grid=(nb, 2, rows // TM),
        in_specs=[
            pl.BlockSpec((1, TM, 2 * HEAD_W), lambda b, g, i: (b, i, EV_AQ // (2 * HEAD_W) + g)),
            pl.BlockSpec((1, rows, HEAD_W), lambda b, g, i: (b, 0, EV_AK // HEAD_W + g)),
            pl.BlockSpec((1, rows, HEAD_W), lambda b, g, i: (b, 0, EV_AV // HEAD_W + g)),
            full, full, full, tile, tile, tile, vec, vec,
        ],
        out_specs=pl.BlockSpec((1, TM, 2 * HEAD_W), lambda b, g, i: (b, i, g)),
        out_shape=jax.ShapeDtypeStruct((nb, rows, 4 * HEAD_W), F32),
        scratch_shapes=[pltpu.VMEM((rows, HEAD_W), BF16), pltpu.VMEM((rows, HEAD_W), BF16)],
        compiler_params=_cparams(("parallel", "parallel", "arbitrary")),
        name="attention",
    )(y3, y3, y3, c, sa, sb, c, sa, sb, q_gain, k_gain)


def _conv_body(x_ref, w_ref, b_ref, o_ref):
    rows = x_ref.shape[1]
    x = x_ref[0]
    row = lax.broadcasted_iota(jnp.int32, x.shape, 0)
    lo = jnp.where(row < CTX, 0, CTX)
    hi = jnp.where(row < CTX, CTX, rows)
    taps = w_ref.shape[0]
    acc = jnp.zeros(x.shape, F32) + b_ref[...]
    for j in range(taps):
        off = j - taps // 2
        xs = x if off == 0 else pltpu.roll(x, (-off) % rows, 0)
        src = row + off
        xs = jnp.where(src >= lo, jnp.where(src < hi, xs, 0.0), 0.0)
        acc += xs * w_ref[j:j + 1, :]
    o_ref[0] = _silu(acc)


def _ssd_conv(y3, conv_w, conv_b):
    nb, rows, _ = y3.shape
    nc = conv_w.shape[1]
    return pl.pallas_call(
        _conv_body,
        grid=(nb, nc // HEAD_W),
        in_specs=[
            pl.BlockSpec((1, rows, HEAD_W), lambda b, j: (b, 0, OD_XS // HEAD_W + j)),
            pl.BlockSpec((conv_w.shape[0], HEAD_W), lambda b, j: (0, j)),
            pl.BlockSpec((1, HEAD_W), lambda b, j: (0, j)),
        ],
        out_specs=pl.BlockSpec((1, rows, HEAD_W), lambda b, j: (b, 0, j)),
        out_shape=jax.ShapeDtypeStruct((nb, rows, nc), F32),
        compiler_params=_cparams(("parallel", "parallel")),
        name="ssd_conv",
    )(y3, conv_w, conv_b.reshape(1, nc))


def _ssd_body(n_ctx_chunks, n_chunks, x_ref, bm_ref, cm_ref, z_ref, misc_ref, sel_ref, par_ref,
              par2_ref, o_ref, st_ref, ob_ref):
    C = SSD_CHUNK
    P = SSD_DH
    rows = o_ref.shape[1]
    w = 4 * HEAD_W
    gw = 4 * P
    st_ref[...] = jnp.zeros(st_ref.shape, F32)
    ri = lax.broadcasted_iota(jnp.int32, (C, C), 0)
    ci = lax.broadcasted_iota(jnp.int32, (C, C), 1)
    masks = (ri >= ci, ri <= ci)
    tri = tuple(mk.astype(BF16) for mk in masks)

    def narrow(a):
        return jnp.concatenate([a[:, hh * HEAD_W:hh * HEAD_W + P] for hh in range(4)], axis=-1)

    def chunk(d, g, r0, misc2, out_ref):
        rs = pl.ds(r0, C)
        x = x_ref[0, rs, g * gw:(g + 1) * gw]
        bmat = bm_ref[0, rs, g * HEAD_W:(g + 1) * HEAD_W]
        cb16 = cm_ref[0, rs, g * HEAD_W:(g + 1) * HEAD_W].astype(BF16)
        b16 = bmat.astype(BF16)
        bt16 = bmat.T.astype(BF16)
        scores = lax.dot_general(cb16, b16, NT_DIMS, preferred_element_type=F32)
        dd = jnp.dot(misc2, sel_ref[g, d], preferred_element_type=F32)
        dt = dd[:C] + dd[C:] + par_ref[g, d:d + 1, :]
        dt = jnp.maximum(dt, 0.0) + jnp.log(1.0 + jnp.exp(-jnp.abs(dt)))
        la = dt * par_ref[g, 2 + d:3 + d, :]
        lh, ll = _split_hi_lo(la)
        cc = jnp.dot(tri[d], jnp.concatenate([lh, ll], axis=-1), preferred_element_type=F32)
        cum = cc[:, :w] + cc[:, w:]
        last = cum[C - 1:C, :] if d == 0 else cum[0:1, :]
        xdt = x * narrow(dt)
        s_t = st_ref[2 * g + d]
        y_in = jnp.dot(cb16, s_t.astype(BF16), preferred_element_type=F32) * narrow(jnp.exp(cum))
        xw = (xdt * narrow(jnp.exp(last - cum))).astype(BF16)
        st_ref[2 * g + d] = s_t * narrow(jnp.exp(last)) + jnp.dot(bt16, xw, preferred_element_type=F32)
        xdt16 = xdt.astype(BF16)
        ys = []
        for hh in range(4):
            cum_h = cum[:, hh * HEAD_W:(hh + 1) * HEAD_W]
            seg = jnp.where(masks[d], cum_h - cum_h.T, 0.0)
            dec = jnp.where(masks[d], jnp.exp(seg), 0.0)
            ys.append(jnp.dot((scores * dec).astype(BF16), xdt16[:, hh * P:(hh + 1) * P],
                              preferred_element_type=F32))
        out_ref[rs, g * gw:(g + 1) * gw] = jnp.concatenate(ys, axis=-1) + y_in

    def step(t, carry):
        cf = t
        cb = jnp.where(t < n_ctx_chunks, n_ctx_chunks - 1 - t, n_chunks + n_ctx_chunks - 1 - t)
        for d, c, out_ref in ((0, cf, o_ref.at[0]), (1, cb, ob_ref)):
            r0 = pl.multiple_of(c * C, C)
            mh, ml = _split_hi_lo(misc_ref[0, pl.ds(r0, C), :])
            misc2 = jnp.concatenate([mh, ml], axis=0)
            for g in range(2):
                chunk(d, g, r0, misc2, out_ref)
        return carry

    lax.fori_loop(0, n_chunks, step, 0)

    blk = 256

    def post(i, carry):
        rs = pl.ds(pl.multiple_of(i * blk, blk), blk)
        for g in range(2):
            cols = slice(g * gw, (g + 1) * gw)
            y = o_ref[0, rs, cols] + ob_ref[rs, cols] + par2_ref[g, 0:1, :] * x_ref[0, rs, cols]
            o_ref[0, rs, cols] = _rms(y * _silu(z_ref[0, rs, cols]), par2_ref[g, 1:2, :])
        return carry

    lax.fori_loop(0, rows // blk, post, 0)


def _ssd_scan(xbc, y3, sel, par, par2):
    nb, rows, _ = xbc.shape
    gw = 4 * SSD_DH
    full = lambda a: pl.BlockSpec(a.shape, lambda b: (0,) * a.ndim)
    return pl.pallas_call(
        functools.partial(_ssd_body, CTX // SSD_CHUNK, rows // SSD_CHUNK),
        grid=(nb,),
        in_specs=[
            pl.BlockSpec((1, rows, 2 * gw), lambda b: (b, 0, 0)),
            pl.BlockSpec((1, rows, 2 * HEAD_W), lambda b: (b, 0, gw // HEAD_W)),
            pl.BlockSpec((1, rows, 2 * HEAD_W), lambda b: (b, 0, gw // HEAD_W + 1)),
            pl.BlockSpec((1, rows, 2 * gw), lambda b: (b, 0, OD_Z // (2 * gw))),
            pl.BlockSpec((1, rows, HEAD_W), lambda b: (b, 0, OD_MISC // HEAD_W)),
            full(sel), full(par), full(par2),
        ],
        out_specs=pl.BlockSpec((1, rows, 2 * gw), lambda b: (b, 0, 0)),
        out_shape=jax.ShapeDtypeStruct((nb, rows, 2 * gw), F32),
        scratch_shapes=[pltpu.VMEM((4, HEAD_W, gw), F32), pltpu.VMEM((rows, 2 * gw), F32)],
        compiler_params=_cparams(("parallel",)),
        name="ssd_scan",
    )(xbc, xbc, xbc, y3, y3, sel, par, par2)


def _outproj_body(h_ref, a_ref, b_ref, g1_ref, sh2_ref, sc2_ref, wo_ref, lng_ref, lnb_ref, wq_ref,
                  sk_ref, h1_ref, hm_ref, st_ref):
    mix = jnp.concatenate([a_ref[...], b_ref[...]], axis=-1).astype(BF16)
    o = jnp.dot(mix, wo_ref[...], preferred_element_type=F32)
    h1 = _layer_norm(DN_ALPHA * h_ref[...] + g1_ref[0] * o, lng_ref[...], lnb_ref[...])
    h1_ref[...] = h1
    hm = (h1 * (1.0 + sc2_ref[0]) + sh2_ref[0]).astype(BF16)
    hm_ref[...] = hm
    qry = jnp.dot(hm, wq_ref[...], preferred_element_type=F32).astype(BF16)
    for j in range(2 * PEER_HEADS):
        qj = qry[:, j * PEER_NKEYS:(j + 1) * PEER_NKEYS]
        st_ref[j * PEER_NKEYS:(j + 1) * PEER_NKEYS, :] = lax.dot_general(
            sk_ref[j], qj, NT_DIMS, preferred_element_type=F32)


def _outproj(h, mix_a, mix_b, modl, w_out, ln_g, ln_b, wq, sk, n_batch, latent_only=False):
    n = h.shape[0]
    tpb = n // n_batch // TM
    if latent_only:
        lat = tpb - 1
        n_out = n_batch * lat * TM
        src = lambda j: (j // lat) * tpb + 1 + j % lat
        mod = lambda k: pl.BlockSpec((1, 1, D), lambda j: (k * MOD_ROWS + j // lat, 0, 0))
    else:
        n_out = n
        src = lambda j: j
        mod = lambda k: _mod_spec(k, n_batch, tpb)
    tok_in = pl.BlockSpec((TM, D), lambda j: (src(j), 0))
    half = pl.BlockSpec((TM, D // 2), lambda j: (src(j), 0))
    tok = pl.BlockSpec((TM, D), lambda j: (j, 0))
    vec = pl.BlockSpec((1, D), lambda j: (0, 0))
    nq = wq.shape[1]
    return pl.pallas_call(
        _outproj_body,
        grid=(n_out // TM,),
        in_specs=[tok_in, half, half, mod(2), mod(3), mod(4), pl.BlockSpec((D, D), lambda j: (0, 0)), vec, vec,
                  pl.BlockSpec((D, nq), lambda j: (0, 0)),
                  pl.BlockSpec(sk.shape, lambda j: (0, 0, 0))],
        out_specs=[tok, tok, pl.BlockSpec((nq, TM), lambda j: (0, j))],
        out_shape=[jax.ShapeDtypeStruct((n_out, D), F32), jax.ShapeDtypeStruct((n_out, D), BF16),
                   jax.ShapeDtypeStruct((nq, n_out), F32)],
        compiler_params=_cparams(("parallel",)),
        name="outproj",
    )(h, mix_a, mix_b, modl, modl, modl, w_out, ln_g, ln_b, wq, sk)


def _sort16_network():
    comps = []

    def merge(lo, hi, r):
        step = r * 2
        if step < hi - lo:
            merge(lo, hi, step)
            merge(lo + r, hi, step)
            comps.extend((j, j + r) for j in range(lo + r, hi - r, step))
        else:
            comps.append((lo, lo + r))

    def sort(lo, hi):
        if hi - lo >= 1:
            mid = lo + (hi - lo) // 2
            sort(lo, mid)
            sort(mid + 1, hi)
            merge(lo, hi, 1)

    sort(0, 15)
    return comps


SORT16 = _sort16_network()
BITONIC16 = [(j, j + s) for s in (8, 4, 2, 1) for j in range(16) if not j & s]


def _compare_exchange(v, comps):
    v = list(v)
    for a, b in comps:
        v[a], v[b] = jnp.maximum(v[a], v[b]), jnp.minimum(v[a], v[b])
    return v


def _merge_top16(a, b_rev):
    c = [a[k] if b_rev[k] is None else jnp.maximum(a[k], b_rev[k]) for k in range(16)]
    return _compare_exchange(c, BITONIC16)


def _top16_keys(ref, row0, col):
    v = [ref[row0 + 8 * j:row0 + 8 * j + 8, col] for j in range(16)]
    v = _compare_exchange(v, SORT16)
    for shift in (4, 2, 1):
        rolled = [pltpu.roll(x, shift, 0) for x in v]
        v = _merge_top16(v, rolled[::-1])
    return v


def _stats_body(st_ref, o_ref):
    k = PEER_TOPK
    nk = PEER_NKEYS
    sub = lax.broadcasted_iota(jnp.int32, (8, 128), 0)

    def column(tc, carry):
        col = pl.ds(pl.multiple_of(tc * 128, 128), 128)
        tops = []
        for side in range(2):
            packed = None
            for h in range(PEER_HEADS):
                a = _top16_keys(st_ref, (2 * h + side) * nk, col)
                packed = a if h == 0 else [jnp.where(sub == h, a[r], packed[r]) for r in range(k)]
            tops.append(packed)
        t1, t2 = tops
        best = [t1[0] + t2[r] for r in range(k)]
        for r1 in range(1, k):
            ln = k // (r1 + 1)
            lst = [t1[r1] + t2[r] for r in range(ln)]
            best = _merge_top16(best, [None] * (k - ln) + lst[::-1])
        z = 1.0
        for r in range(1, k):
            z = z + jnp.exp(best[r] - best[0])
        o_ref[0, :, col] = t1[0]
        o_ref[1, :, col] = t2[0] + jnp.log(z)
        o_ref[2, :, col] = best[k - 1]
        for r in range(k):
            o_ref[3 + r, :, col] = t2[r]
        return carry

    lax.fori_loop(0, o_ref.shape[2] // 128, column, 0)


def _peer_stats(st):
    n = st.shape[1]
    ts = 512
    return pl.pallas_call(
        _stats_body,
        grid=(n // ts,),
        in_specs=[pl.BlockSpec((st.shape[0], ts), lambda i: (0, i))],
        out_specs=pl.BlockSpec((3 + PEER_TOPK, PEER_HEADS, ts), lambda i: (0, 0, i)),
        out_shape=jax.ShapeDtypeStruct((3 + PEER_TOPK, PEER_HEADS, n), F32),
        compiler_params=_cparams(("parallel",)),
        name="peer_stats",
    )(st)


def _count_prefix(pred, rows):
    pick = lambda a, b, c: jnp.where(c, b, a)
    c8 = pred(rows[7])
    c4 = pred(pick(rows[3], rows[11], c8))
    c2 = pred(pick(pick(rows[1], rows[5], c4), pick(rows[9], rows[13], c4), c8))
    ev = [rows[2 * j] for j in range(8)]
    c1 = pred(pick(pick(pick(ev[0], ev[1], c2), pick(ev[2], ev[3], c2), c4),
                   pick(pick(ev[4], ev[5], c2), pick(ev[6], ev[7], c2), c4), c8))
    c16 = pred(rows[15])
    val = lambda c, v: jnp.where(c, v, 0.0)
    return val(c8, 8.0) + val(c4, 4.0) + val(c2, 2.0) + val(c1, 1.0) + val(c16, 1.0)


def _expert_body(hm_ref, st_ref, stats_ref, u_ref, vt_ref, y_ref, e1_ref, n1_ref, e2_ref, r2_ref,
                 a_ref, p_ref, acc_ref, hmt_ref):
    e = pl.program_id(1)
    ne = pl.num_programs(1)
    t = hm_ref.shape[0]
    nk = PEER_NKEYS
    k = PEER_TOPK
    m = PEER_EB // nk
    ntc = t // 128

    @pl.when(e == 0)
    def _():
        def prep(tc, carry):
            col = pl.ds(pl.multiple_of(tc * 128, 128), 128)
            for h in range(PEER_HEADS):
                rows = slice(h * nk, (h + 1) * nk)
                s1 = st_ref[2 * h * nk:(2 * h + 1) * nk, col]
                s2 = st_ref[(2 * h + 1) * nk:(2 * h + 2) * nk, col]
                tau = stats_ref[2, h:h + 1, col]
                t2 = [stats_ref[3 + r, h:h + 1, col] for r in range(k)]
                n1 = _count_prefix(lambda thr: s1 + thr >= tau, t2)
                r2 = _count_prefix(lambda thr: thr > s2, t2)
                e1_ref[rows, col] = jnp.exp(s1 - stats_ref[0, h:h + 1, col])
                n1_ref[rows, col] = n1
                e2_ref[rows, col] = jnp.exp(s2 - stats_ref[1, h:h + 1, col]).astype(BF16)
                r2_ref[rows, col] = r2.astype(BF16)
            return carry

        lax.fori_loop(0, ntc, prep, 0)
        acc_ref[...] = jnp.zeros(acc_ref.shape, F32)
        hmt_ref[...] = hm_ref[...].astype(F32).T.astype(BF16)

    a_ref[...] = jax.nn.gelu(jnp.dot(u_ref[...], hmt_ref[...], preferred_element_type=F32),
                             approximate=True).astype(BF16)

    cw = PEER_CW

    def build(tc, carry):
        col = pl.ds(pl.multiple_of(tc * cw, cw), cw)
        own = [pl.ds(pl.multiple_of(h * nk + e * m, m), m) for h in range(PEER_HEADS)]
        e1 = [e1_ref[own[h], col].astype(BF16) for h in range(PEER_HEADS)]
        n1 = [n1_ref[own[h], col].astype(BF16) for h in range(PEER_HEADS)]
        zero = jnp.zeros((), BF16)
        group = 2
        for g0 in range(0, m, group):
            gates = [jnp.zeros((nk, cw), BF16) for _ in range(group)]
            for h in range(PEER_HEADS):
                rows2 = slice(h * nk, (h + 1) * nk)
                r2 = r2_ref[rows2, col]
                e2 = e2_ref[rows2, col]
                for j in range(group):
                    n1b = jnp.broadcast_to(n1[h][g0 + j:g0 + j + 1, :], (nk, cw))
                    e1b = jnp.broadcast_to(e1[h][g0 + j:g0 + j + 1, :], (nk, cw))
                    gates[j] = gates[j] + e1b * jnp.where(r2 < n1b, e2, zero)
            for j in range(group):
                rows = slice((g0 + j) * nk, (g0 + j + 1) * nk)
                p_ref[rows, col] = a_ref[rows, col] * gates[j]
        return carry

    lax.fori_loop(0, t // cw, build, 0)
    acc_ref[...] += jnp.dot(vt_ref[...], p_ref[...], preferred_element_type=F32)

    @pl.when(e == ne - 1)
    def _():
        y_ref[...] = acc_ref[...].T


def _peer_experts(hm, st, stats, u, vt):
    n = hm.shape[0]
    ne = u.shape[0] // PEER_EB
    assert (PEER_EB // PEER_NKEYS) % 8 == 0
    t = PEER_T
    half = (PEER_HEADS * PEER_NKEYS, t)
    return pl.pallas_call(
        _expert_body,
        grid=(n // t, ne),
        in_specs=[
            pl.BlockSpec((t, D), lambda i, e: (i, 0)),
            pl.BlockSpec((st.shape[0], t), lambda i, e: (0, i)),
            pl.BlockSpec((stats.shape[0], PEER_HEADS, t), lambda i, e: (0, 0, i)),
            pl.BlockSpec((PEER_EB, D), lambda i, e: (e, 0)),
            pl.BlockSpec((D, PEER_EB), lambda i, e: (0, e)),
        ],
        out_specs=pl.BlockSpec((t, D), lambda i, e: (i, 0)),
        out_shape=jax.ShapeDtypeStruct((n, D), F32),
        scratch_shapes=[pltpu.VMEM(half, F32), pltpu.VMEM(half, F32), pltpu.VMEM(half, BF16),
                        pltpu.VMEM(half, BF16), pltpu.VMEM((PEER_EB, t), BF16),
                        pltpu.VMEM((PEER_EB, t), BF16), pltpu.VMEM((D, t), F32),
                        pltpu.VMEM((D, t), BF16)],
        compiler_params=_cparams(("parallel", "arbitrary")),
        name="peer_experts",
    )(hm, st, stats, u, vt)


def _rope_tables(rows_lat):
    pairs = HEAD_W // 4
    t = jnp.arange(rows_lat)
    inv = ROPE_THETA ** (-jnp.arange(pairs, dtype=F32) / pairs)
    ang = jnp.concatenate([(t // GRID_W).astype(F32)[:, None] * inv,
                           (t % GRID_W).astype(F32)[:, None] * inv], axis=-1)
    cos, sin = jnp.cos(ang), jnp.sin(ang)
    zero = jnp.zeros_like(sin)
    c = jnp.stack([cos, cos], -1).reshape(rows_lat, HEAD_W)
    sa = jnp.stack([-sin, zero], -1).reshape(rows_lat, HEAD_W)
    sb = jnp.stack([zero, sin], -1).reshape(rows_lat, HEAD_W)
    ident = jnp.ones((CTX, HEAD_W), F32)
    none = jnp.zeros((CTX, HEAD_W), F32)
    return (jnp.concatenate([ident, c]), jnp.concatenate([none, sa]), jnp.concatenate([none, sb]))


def _odd_column_map():
    src = -np.ones((OD_W,), np.int64)
    for h in range(4):
        src[OD_Q + h * HEAD_W:OD_Q + h * HEAD_W + GLA_DK] = np.arange(h * GLA_DK, (h + 1) * GLA_DK)
        src[OD_K + h * HEAD_W:OD_K + h * HEAD_W + GLA_DK] = 256 + np.arange(h * GLA_DK, (h + 1) * GLA_DK)
    src[OD_V:OD_V + 512] = 512 + np.arange(512)
    src[OD_G:OD_G + 512] = 1024 + np.arange(512)
    src[OD_Z:OD_Z + 512] = 1568 + np.arange(512)
    src[OD_XS:OD_XS + 512] = 2080 + np.arange(512)
    src[OD_BM:OD_BM + 256] = 2592 + np.arange(256)
    src[OD_CM:OD_CM + 256] = 2848 + np.arange(256)
    src[OD_MISC + MISC_LRF:OD_MISC + MISC_LRF + 16] = 1536 + np.arange(16)
    src[OD_MISC + MISC_LRB:OD_MISC + MISC_LRB + 16] = 1552 + np.arange(16)
    src[OD_MISC + MISC_DTF:OD_MISC + MISC_DTF + 8] = 3104 + np.arange(8)
    src[OD_MISC + MISC_DTB:OD_MISC + MISC_DTB + 8] = 3112 + np.arange(8)
    return src


def _relayout_odd_w(w):
    src = _odd_column_map()
    cols = jnp.take(w, jnp.asarray(np.maximum(src, 0)), axis=1)
    return jnp.where(jnp.asarray(src >= 0)[None, :], cols, 0.0)


def _pad_rows(rows_list, width):
    out = [jnp.broadcast_to(jnp.asarray(r, F32).reshape(1, width), (1, width)) for r in rows_list]
    out += [jnp.zeros((1, width), F32)] * (8 - len(out))
    return jnp.concatenate(out, axis=0)


def _hgrn_params(lb, gain):
    la = jnp.log(jnp.maximum(lb, LB_FLOOR))
    l1 = jnp.log1p(-lb)
    oml = 1.0 - lb
    return _pad_rows([la[0], la[1], l1[0], l1[1], oml[0], oml[1], jnp.tile(gain, 4)], 4 * HEAD_W)


def _gla_params(gate_w, gate_b, gain):
    wg = jnp.zeros((2, HEAD_W, 4 * HEAD_W), F32)
    gb = jnp.zeros((2, 4 * HEAD_W), F32)
    for h in range(4):
        s = slice(h * GLA_DK, (h + 1) * GLA_DK)
        dst = slice(h * HEAD_W, h * HEAD_W + GLA_DK)
        for d, off in enumerate((MISC_LRF, MISC_LRB)):
            wg = wg.at[d, off:off + 16, dst].set(gate_w[d][:, s])
            gb = gb.at[d, dst].set(gate_b[d, s])
    return wg.astype(BF16), _pad_rows([gb[0], gb[1], jnp.tile(gain, 4)], 4 * HEAD_W)


def _ssd_params(dt_bias, a_log, d_skip, norm_g):
    sel = np.zeros((2, 2, HEAD_W, 4 * HEAD_W), np.float32)
    for g in range(2):
        for d, off in enumerate((MISC_DTF, MISC_DTB)):
            for hh in range(4):
                sel[g, d, off + 4 * g + hh, hh * HEAD_W:(hh + 1) * HEAD_W] = 1.0
    par, par2 = [], []
    neg_a = -jnp.exp(a_log.astype(F32))
    for g in range(2):
        hs = slice(4 * g, 4 * g + 4)
        rep = lambda v: jnp.repeat(v[hs], HEAD_W)
        par.append(_pad_rows([rep(dt_bias[0]), rep(dt_bias[1]), rep(neg_a[0]), rep(neg_a[1])], 4 * HEAD_W))
        par2.append(_pad_rows([jnp.repeat(d_skip[hs], SSD_DH), norm_g[g * 256:(g + 1) * 256]], 4 * SSD_DH))
    return jnp.asarray(sel, BF16), jnp.stack(par), jnp.stack(par2)


def kernel(x, c, ctx, c_ctx, mod_w, mod_b, ln_g, ln_b, ev_w_in, ev_w_out, hg_lb_logits, hg_norm_g, at_q_norm_g, at_k_norm_g, od_w_in, od_w_out, gla_gate_w, gla_gate_b, gla_norm_g, ssd_conv_w, ssd_conv_b, ssd_dt_bias, ssd_a_log, ssd_d, ssd_norm_g, peer_wq, peer_subkeys, peer_u, peer_v):
    nb, seq, _ = x.shape
    assert ctx.shape[1] == CTX and seq % TM == 0 and nb + 1 <= MOD_ROWS
    rows = CTX + seq
    n = nb * rows

    cc = jnp.concatenate([c, c_ctx[None, :], jnp.zeros((MOD_ROWS - nb - 1, D), F32)], axis=0)
    mods = _modulation(cc, mod_w, mod_b)
    mods = mods.reshape(DEPTH, MOD_ROWS, N_MOD, D).transpose(0, 2, 1, 3).reshape(DEPTH, N_MOD * MOD_ROWS, 1, D)

    sm = jax.nn.softmax(hg_lb_logits.astype(F32), axis=1)
    hg_lb = jnp.cumsum(sm, axis=1) - sm[:, :1]
    tables = _rope_tables(seq)

    h = jnp.concatenate([ctx, x], axis=1).reshape(n, D)
    pre = None
    for l in range(DEPTH):
        j = l // 2
        modl = mods[l]
        if l % 2 == 0:
            y, h = _inproj(h, ev_w_in[j].astype(BF16), modl, nb, pre)
            y3 = y.reshape(nb, rows, EV_W)
            mix_a = _vscan("hgrn", y3, ((EV_Q, EV_I, EV_ZF), (EV_Q, EV_I, EV_ZB)), EV_G,
                           _hgrn_params(hg_lb[:, j], hg_norm_g[j]))
            mix_b = _attention(y3, tables, at_q_norm_g[j].reshape(1, HEAD_W), at_k_norm_g[j].reshape(1, HEAD_W))
            w_out = ev_w_out[j]
        else:
            y, h = _inproj(h, _relayout_odd_w(od_w_in[j]).astype(BF16), modl, nb, pre)
            y3 = y.reshape(nb, rows, OD_W)
            wg, gpar = _gla_params(gla_gate_w[j], gla_gate_b[j], gla_norm_g[j])
            mix_a = _vscan("gla", y3, ((OD_Q, OD_K, OD_V),) * 2, OD_G, gpar, wg)
            xbc = _ssd_conv(y3, ssd_conv_w[j], ssd_conv_b[j])
            sel, spar, spar2 = _ssd_params(ssd_dt_bias[j], ssd_a_log[j], ssd_d[j], ssd_norm_g[j])
            mix_b = _ssd_scan(xbc, y3, sel, spar, spar2)
            w_out = od_w_out[j]
        lg1, lb1 = ln_g[l, 0].reshape(1, D), ln_b[l, 0].reshape(1, D)
        sk = peer_subkeys[l].reshape(2 * PEER_HEADS, PEER_NKEYS, PEER_NKEYS).astype(BF16)
        h1, hm, st = _outproj(h, mix_a.reshape(n, D // 2), mix_b.reshape(n, D // 2), modl,
                              w_out.astype(BF16), lg1, lb1, peer_wq[l].astype(BF16), sk, nb,
                              latent_only=(l == DEPTH - 1))
        stats = _peer_stats(st)
        yp = _peer_experts(hm, st, stats, peer_u[l].astype(BF16), peer_v[l].astype(BF16).T)
        pre = (yp, modl, ln_g[l, 1].reshape(1, D), ln_b[l, 1].reshape(1, D))
        h = h1
    out = _final(h, pre[0], pre[1], pre[2], pre[3], nb)
    return out.reshape(nb, seq, D)
```
